```python
import jax, jax.numpy as jnp
from jax import lax
import numpy as np

D_MODEL = 1024
BATCH = 8
SEQ = 2048
DEPTH = 2

MEM_TOKENS = 256
HEAD_DIM = 64
CONV_CH = 3 * D_MODEL // 8
N_DIL_HEADS = (3 * D_MODEL // 8) // HEAD_DIM
N_MEM_HEADS = 4
DIL_WIDTH = N_DIL_HEADS * HEAD_DIM
MEM_WIDTH = N_MEM_HEADS * HEAD_DIM
MIX_WIDTH = CONV_CH + DIL_WIDTH + MEM_WIDTH
IN_SPLITS = [CONV_CH, 2 * CONV_CH, 3 * CONV_CH,
             3 * CONV_CH + DIL_WIDTH, 3 * CONV_CH + 2 * DIL_WIDTH, 3 * CONV_CH + 3 * DIL_WIDTH]
IN_PROJ_WIDTH = 3 * CONV_CH + 3 * DIL_WIDTH + MEM_WIDTH
CONV_KSIZE = 3
DILATED_PATTERNS = ((128, 1), (512, 4), (2048, 16))
N_EXPERTS = 16
D_FF = 2 * D_MODEL
EC_CAPACITY = 2
RMS_EPS = 1e-6
NEG_INF = -1e30

kernel_name = "hybrid_conv_dilattn_mem_ecmoe_encoder"


def rmsnorm(x, g):
    xf = x.astype(jnp.float32)
    y = xf * lax.rsqrt(jnp.mean(xf * xf, axis=-1, keepdims=True) + RMS_EPS)
    return (y * g.astype(jnp.float32)).astype(x.dtype)


def alibi_slopes(n):
    return jnp.asarray([2.0 ** (-8.0 * (h + 1) / n) for h in range(n)], dtype=jnp.float32)


def split_heads(t, n):
    b, s, _ = t.shape
    return t.reshape(b, s, n, HEAD_DIM).transpose(0, 2, 1, 3)


def merge_heads(t):
    b, h, s, d = t.shape
    return t.transpose(0, 2, 1, 3).reshape(b, s, h * d)


def short_conv_mixer(xc, b_gate, c_gate, conv_w):
    s = xc.shape[1]
    u = c_gate * xc
    up = jnp.pad(u, ((0, 0), (1, 1), (0, 0)))
    conv = conv_w[0] * up[:, :s] + conv_w[1] * up[:, 1:s + 1] + conv_w[2] * up[:, 2:]
    return b_gate * conv


def dilated_branch(q, k, v, slopes, window, dil):
    bsz, h, s, hd = q.shape
    half = window // (2 * dil)
    L = s // dil
    nb = -(-L // half)
    Lp = nb * half

    def to_res(t):
        return t.reshape(bsz, h, L, dil, hd).transpose(0, 1, 3, 2, 4)

    qr, kr, vr = to_res(q), to_res(k), to_res(v)
    qr = jnp.pad(qr, ((0, 0), (0, 0), (0, 0), (0, Lp - L), (0, 0)))
    pad_k = ((0, 0), (0, 0), (0, 0), (half, Lp - L + half), (0, 0))
    kr, vr = jnp.pad(kr, pad_k), jnp.pad(vr, pad_k)
    qb = qr.reshape(bsz, h, dil, nb, half, hd)

    def band(t):
        tb = t.reshape(bsz, h, dil, nb + 2, half, hd)
        return jnp.concatenate([tb[:, :, :, :-2], tb[:, :, :, 1:-1], tb[:, :, :, 2:]], axis=4)

    kb, vb = band(kr), band(vr)
    scores = jnp.einsum('bhrnqd,bhrnkd->bhrnqk', qb, kb) * (HEAD_DIM ** -0.5)
    qi = jnp.arange(nb)[:, None, None] * half + jnp.arange(half)[None, :, None]
    kj = jnp.arange(nb)[:, None, None] * half - half + jnp.arange(3 * half)[None, None, :]
    rel = kj - qi
    valid = (jnp.abs(rel) <= half) & (kj >= 0) & (kj < L)
    dist = (jnp.abs(rel) * dil).astype(jnp.float32)
    bias = -slopes[:, None, None, None] * dist[None]
    scores = jnp.where(valid[None, None, None], scores + bias[None, :, None], NEG_INF)
    m = jnp.max(scores, axis=-1, keepdims=True)
    p = jnp.exp(scores - m)
    l = jnp.sum(p, axis=-1, keepdims=True)
    o = jnp.einsum('bhrnqk,bhrnkd->bhrnqd', p, vb) / l
    lse = (m + jnp.log(l))[..., 0]
    o = o.reshape(bsz, h, dil, Lp, hd)[:, :, :, :L].transpose(0, 1, 3, 2, 4).reshape(bsz, h, s, hd)
    lse = lse.reshape(bsz, h, dil, Lp)[..., :L].transpose(0, 1, 3, 2).reshape(bsz, h, s)
    return o, lse


def dilated_attention(q, k, v, slopes):
    outs, lses = [], []
    for window, dil in DILATED_PATTERNS:
        o, lse = dilated_branch(q, k, v, slopes, window, dil)
        outs.append(o)
        lses.append(lse)
    w = jax.nn.softmax(jnp.stack(lses, axis=0), axis=0)
    return jnp.sum(w[..., None] * jnp.stack(outs, axis=0), axis=0)


def memory_attention(qm, km, vm):
    scores = jnp.einsum('bhqd,bhkd->bhqk', qm.astype(jnp.float32), km.astype(jnp.float32)) * (HEAD_DIM ** -0.5)
    p = jax.nn.softmax(scores, axis=-1)
    return jnp.einsum('bhqk,bhkd->bhqd', p, vm.astype(jnp.float32))


def parallel_mixer(xn, mem_n, w_in, conv_w, w_mem_kv, w_out, slopes):
    h = xn @ w_in
    xc, bg, cg, q, k, v, qm = jnp.split(h, IN_SPLITS, axis=-1)
    conv_out = short_conv_mixer(xc, bg, cg, conv_w)
    f32 = jnp.float32
    dil = dilated_attention(split_heads(q, N_DIL_HEADS).astype(f32),
                            split_heads(k, N_DIL_HEADS).astype(f32),
                            split_heads(v, N_DIL_HEADS).astype(f32), slopes)
    dil_out = merge_heads(dil).astype(xn.dtype)
    km, vm = jnp.split(mem_n @ w_mem_kv, 2, axis=-1)
    mo = memory_attention(split_heads(qm, N_MEM_HEADS), split_heads(km, N_MEM_HEADS),
                          split_heads(vm, N_MEM_HEADS))
    mem_out = merge_heads(mo).astype(xn.dtype)
    return jnp.concatenate([conv_out, dil_out, mem_out], axis=-1) @ w_out


def expert_choice_ffn(xn, w_router, w_gate, w_up, w_down):
    bsz, s, d = xn.shape
    cap = EC_CAPACITY * s // N_EXPERTS
    aff = jax.nn.softmax((xn @ w_router).astype(jnp.float32), axis=-1)
    g, idx = lax.top_k(aff.transpose(0, 2, 1), cap)
    xg = jax.vmap(lambda xb, ib: xb[ib])(xn, idx)
    hid = jax.nn.silu(jnp.einsum('becd,edf->becf', xg, w_gate)) * jnp.einsum('becd,edf->becf', xg, w_up)
    y = jnp.einsum('becf,efd->becd', hid, w_down) * g[..., None].astype(xn.dtype)
    return jax.vmap(lambda yb, ib: jax.ops.segment_sum(yb.reshape(-1, d), ib.reshape(-1), num_segments=s))(y, idx)


def setup_inputs(seed: int = 0) -> dict:
    key = jax.random.key(seed)
    ks = jax.random.split(key, 14)
    f32 = jnp.float32
    D, L, E, F = D_MODEL, DEPTH, N_EXPERTS, D_FF
    nrm = lambda k, shape, scale: (jax.random.normal(k, shape, f32) * scale).astype(f32)
    return {
        "x": nrm(ks[0], (BATCH, SEQ, D), 1.0),
        "mem": nrm(ks[1], (BATCH, MEM_TOKENS, D), 1.0),
        "mem_norm": 1.0 + nrm(ks[2], (D,), 0.02),
        "norm_mix": 1.0 + nrm(ks[3], (L, D), 0.02),
        "w_in": nrm(ks[4], (L, D, IN_PROJ_WIDTH), D ** -0.5),
        "conv_w": nrm(ks[5], (L, CONV_KSIZE, CONV_CH), 0.5),
        "w_mem_kv": nrm(ks[6], (L, D, 2 * MEM_WIDTH), D ** -0.5),
        "w_out": nrm(ks[7], (L, MIX_WIDTH, D), MIX_WIDTH ** -0.5),
        "norm_ffn": 1.0 + nrm(ks[8], (L, D), 0.02),
        "w_router": nrm(ks[9], (L, D, E), D ** -0.5),
        "w_gate": nrm(ks[10], (L, E, D, F), D ** -0.5),
        "w_up": nrm(ks[11], (L, E, D, F), D ** -0.5),
        "w_down": nrm(ks[12], (L, E, F, D), F ** -0.5),
        "norm_final": 1.0 + nrm(ks[13], (D,), 0.02),
    }


def reference(x, mem, mem_norm, norm_mix, w_in, conv_w, w_mem_kv, w_out, norm_ffn,
              w_router, w_gate, w_up, w_down, norm_final):
    slopes = alibi_slopes(N_DIL_HEADS)
    mem_n = rmsnorm(mem, mem_norm)
    for l in range(DEPTH):
        x = x + parallel_mixer(rmsnorm(x, norm_mix[l]), mem_n, w_in[l], conv_w[l],
                               w_mem_kv[l], w_out[l], slopes)
        x = x + expert_choice_ffn(rmsnorm(x, norm_ffn[l]), w_router[l], w_gate[l], w_up[l], w_down[l])
    return rmsnorm(x, norm_final)
```

```python
import functools

import jax
import jax.numpy as jnp
from jax import lax
from jax.experimental import pallas as pl
from jax.experimental.pallas import tpu as pltpu

F32 = jnp.float32
BF16 = jnp.bfloat16

HEAD_DIM = 64
N_MEM_HEADS = 4
DILATED_PATTERNS = ((128, 1), (512, 4), (2048, 16))
HALF_WINDOW = 64
N_EXPERTS = 16
EC_CAPACITY = 2
RMS_EPS = 1e-6
NEG_INF = -1e30

LANES = 128
ROW_TILE = 512
ATT_TILE = 128
V7X_VMEM_LIMIT_BYTES = 56 * 1024 * 1024


def _params(*semantics):
    return pltpu.CompilerParams(dimension_semantics=semantics,
                                vmem_limit_bytes=V7X_VMEM_LIMIT_BYTES)


def _rms(x, g):
    ms = jnp.mean(x * x, axis=-1, keepdims=True)
    return x * lax.rsqrt(ms + RMS_EPS) * g


def _nt_dot(a, b, **kw):
    return lax.dot_general(a, b, (((1,), (1,)), ((), ())), preferred_element_type=F32, **kw)


def _inproj_kernel(x_ref, g_ref, w_ref, hc_ref, hq_ref):
    xn = _rms(x_ref[...], g_ref[...]).astype(BF16)
    n_conv = hc_ref.shape[1]
    chunk = 2 * LANES
    for c in range(0, w_ref.shape[1], chunk):
        h = jnp.dot(xn, w_ref[:, c:c + chunk], preferred_element_type=F32)
        if c + chunk <= n_conv:
            hc_ref[:, c:c + chunk] = h
        elif c >= n_conv:
            hq_ref[:, c - n_conv:c - n_conv + chunk] = h.astype(BF16)
        else:
            k = n_conv - c
            hc_ref[:, c:n_conv] = h[:, :k]
            hq_ref[:, 0:chunk - k] = h[:, k:].astype(BF16)


def _inproj(x2, g, w, n_conv):
    n, d = x2.shape
    n_all = w.shape[1]
    return pl.pallas_call(
        _inproj_kernel,
        grid=(n // ROW_TILE,),
        in_specs=[pl.BlockSpec((ROW_TILE, d), lambda i: (i, 0)),
                  pl.BlockSpec((1, d), lambda i: (0, 0)),
                  pl.BlockSpec((d, n_all), lambda i: (0, 0))],
        out_specs=[pl.BlockSpec((ROW_TILE, n_conv), lambda i: (i, 0)),
                   pl.BlockSpec((ROW_TILE, n_all - n_conv), lambda i: (i, 0))],
        out_shape=[jax.ShapeDtypeStruct((n, n_conv), F32),
                   jax.ShapeDtypeStruct((n, n_all - n_conv), BF16)],
        compiler_params=_params("arbitrary"),
        name="inproj",
    )(x2, g, w)


def _conv_kernel(xc_ref, bg_ref, cg_ref, w_ref, o_ref):
    u = cg_ref[0] * xc_ref[0]
    s = u.shape[0]
    row = lax.broadcasted_iota(jnp.int32, u.shape, 0)
    prev = jnp.where(row == 0, 0.0, pltpu.roll(u, 1, 0))
    nxt = jnp.where(row == s - 1, 0.0, pltpu.roll(u, s - 1, 0))
    w = w_ref[...]
    conv = w[0:1] * prev + w[1:2] * u + w[2:3] * nxt
    o_ref[0] = (bg_ref[0] * conv).astype(BF16)


def _conv(hc3, conv_w, conv_ch):
    b, s, _ = hc3.shape
    nblk = conv_ch // LANES
    blk = (1, s, LANES)
    return pl.pallas_call(
        _conv_kernel,
        grid=(b, nblk),
        in_specs=[pl.BlockSpec(blk, lambda i, j: (i, 0, j)),
                  pl.BlockSpec(blk, lambda i, j: (i, 0, nblk + j)),
                  pl.BlockSpec(blk, lambda i, j: (i, 0, 2 * nblk + j)),
                  pl.BlockSpec((conv_w.shape[0], LANES), lambda i, j: (0, j))],
        out_specs=pl.BlockSpec(blk, lambda i, j: (i, 0, j)),
        out_shape=jax.ShapeDtypeStruct((b, s, conv_ch), BF16),
        compiler_params=_params("arbitrary", "arbitrary"),
        name="shortconv",
    )(hc3, hc3, hc3, conv_w)


def _dil_cases(seq, dil):
    tps = (seq // dil) // ATT_TILE
    return tps, (("single",) if tps == 1 else ("first", "interior", "last"))


def _dil_kernel(slopes_ref, q_ref, k_ref, v_ref, o_ref,
                natq, natk, natv, qp0, qp1, kp, vp, bias, *partials):
    seq = q_ref.shape[1]
    pad = HALF_WINDOW
    tq = ATT_TILE
    nums, mxs, sums = partials[0:3], partials[3:6], partials[6:9]
    pair = pl.program_id(1)
    slope0 = slopes_ref[2 * pair]
    slope1 = slopes_ref[2 * pair + 1]
    lane = lax.broadcasted_iota(jnp.int32, (1, LANES), 1)
    head0 = lane < HEAD_DIM
    m0 = head0.astype(F32)
    m1 = 1.0 - m0

    natq[...] = q_ref[0].astype(F32) * (HEAD_DIM ** -0.5)
    natk[...] = k_ref[0].astype(F32)
    natv[...] = v_ref[0].astype(F32)

    kp[0:pad, :] = jnp.zeros((pad, LANES), BF16)
    kp[pad + seq:pad + seq + pad, :] = jnp.zeros((pad, LANES), BF16)
    vp[0:pad, :] = jnp.zeros((pad, 2 * LANES), BF16)
    vp[pad + seq:pad + seq + pad, :] = jnp.zeros((pad, 2 * LANES), BF16)
    vp[pad:pad + seq, LANES:2 * LANES] = jnp.ones((seq, LANES), BF16)

    row = lax.broadcasted_iota(jnp.int32, (2 * tq, 2 * tq), 0)
    col = lax.broadcasted_iota(jnp.int32, (2 * tq, 2 * tq), 1)
    kc = col - pad
    arel = jnp.abs(kc - (row & (tq - 1)))
    band = arel <= HALF_WINDOW
    slope_rows = jnp.where(row < tq, slope0, slope1)
    case_base = []
    idx = 0
    for _, dil in DILATED_PATTERNS:
        _, cases = _dil_cases(seq, dil)
        case_base.append(idx)
        dist_bias = -slope_rows * (arel * dil).astype(F32)
        for case in cases:
            valid = band
            if case in ("first", "single"):
                valid = valid & (kc >= 0)
            if case in ("last", "single"):
                valid = valid & (kc < tq)
            bias[idx] = jnp.where(valid, dist_bias, NEG_INF)
            idx += 1

    for pi, (_, dil) in enumerate(DILATED_PATTERNS):
        seg = seq // dil
        log_seg = seg.bit_length() - 1
        tps, _ = _dil_cases(seq, dil)

        for r in range(dil):
            if dil == 1:
                qv, kv, vv = natq[...], natk[...], natv[...]
            else:
                qv = natq[pl.ds(r, seg, stride=dil), :]
                kv = natk[pl.ds(r, seg, stride=dil), :]
                vv = natv[pl.ds(r, seg, stride=dil), :]
            qp0[r * seg:(r + 1) * seg, :] = (qv * m0).astype(BF16)
            qp1[r * seg:(r + 1) * seg, :] = (qv * m1).astype(BF16)
            kp[pad + r * seg:pad + (r + 1) * seg, :] = kv.astype(BF16)
            vp[pad + r * seg:pad + (r + 1) * seg, 0:LANES] = vv.astype(BF16)

        def tile(m, carry, pi=pi, dil=dil, seg=seg, log_seg=log_seg, tps=tps):
            r0 = pl.multiple_of(m * tq, tq)
            qc = jnp.concatenate([qp0[pl.ds(r0, tq), :], qp1[pl.ds(r0, tq), :]], axis=0)
            kt = kp[pl.ds(r0, 2 * tq), :]
            vt = vp[pl.ds(r0, 2 * tq), :]
            if tps == 1:
                case = case_base[pi]
            else:
                pos = m & (tps - 1)
                case = case_base[pi] + jnp.where(pos == 0, 0, jnp.where(pos == tps - 1, 2, 1))
            s = _nt_dot(qc, kt) + bias[case]
            mx = jnp.max(s, axis=-1, keepdims=True)
            pe = jnp.exp(s - mx)
            pv = jnp.dot(pe.astype(BF16), vt, preferred_element_type=F32)
            num = jnp.where(head0, pv[0:tq, 0:LANES], pv[tq:2 * tq, 0:LANES])
            rsum = jnp.where(head0, pv[0:tq, LANES:2 * LANES], pv[tq:2 * tq, LANES:2 * LANES])
            mxb = jnp.where(head0, mx[0:tq], mx[tq:2 * tq])
            if dil == 1:
                dst = pl.ds(r0, tq)
            else:
                start = (r0 & (seg - 1)) * dil + (r0 >> log_seg)
                dst = pl.ds(start, tq, stride=dil)
            nums[pi][dst, :] = num
            sums[pi][dst, :] = rsum
            mxs[pi][dst, :] = mxb
            return carry

        lax.fori_loop(0, seq // tq, tile, 0, unroll=2)

    def merge(c, carry):
        rows = pl.ds(pl.multiple_of(c * 256, 256), 256)
        a = [mxs[i][rows, :] for i in range(3)]
        top = jnp.maximum(jnp.maximum(a[0], a[1]), a[2])
        e = [jnp.exp(a[i] - top) for i in range(3)]
        den = e[0] * sums[0][rows, :] + e[1] * sums[1][rows, :] + e[2] * sums[2][rows, :]
        num = e[0] * nums[0][rows, :] + e[1] * nums[1][rows, :] + e[2] * nums[2][rows, :]
        o_ref[0, rows, :] = (num / den).astype(BF16)
        return carry

    lax.fori_loop(0, seq // 256, merge, 0)


def _dilated(hq3, slopes, dil_width):
    b, s, _ = hq3.shape
    npair = dil_width // LANES
    blk = (1, s, LANES)
    n_cases = sum(len(_dil_cases(s, d)[1]) for _, d in DILATED_PATTERNS)
    pad = HALF_WINDOW
    scratch = [pltpu.VMEM((s, LANES), F32)] * 3
    scratch += [pltpu.VMEM((s, LANES), BF16)] * 2
    scratch += [pltpu.VMEM((s + 2 * pad, LANES), BF16), pltpu.VMEM((s + 2 * pad, 2 * LANES), BF16)]
    scratch += [pltpu.VMEM((n_cases, 2 * ATT_TILE, 2 * ATT_TILE), F32)]
    scratch += [pltpu.VMEM((s, LANES), F32)] * 9
    return pl.pallas_call(
        _dil_kernel,
        grid_spec=pltpu.PrefetchScalarGridSpec(
            num_scalar_prefetch=1,
            grid=(b, npair),
            in_specs=[pl.BlockSpec(blk, lambda i, j, sl: (i, 0, j)),
                      pl.BlockSpec(blk, lambda i, j, sl: (i, 0, npair + j)),
                      pl.BlockSpec(blk, lambda i, j, sl: (i, 0, 2 * npair + j))],
            out_specs=pl.BlockSpec(blk, lambda i, j, sl: (i, 0, j)),
            scratch_shapes=scratch),
        out_shape=jax.ShapeDtypeStruct((b, s, dil_width), BF16),
        compiler_params=_params("arbitrary", "arbitrary"),
        name="dilated_attn",
    )(slopes, hq3, hq3, hq3)


def _memkv_kernel(m_ref, g_ref, w_ref, o_ref):
    mn = _rms(m_ref[...], g_ref[...]).astype(BF16)
    o_ref[...] = jnp.dot(mn, w_ref[...], preferred_element_type=F32).astype(BF16)


def _memkv(mem2, g, w):
    n, d = mem2.shape
    nw = w.shape[1]
    return pl.pallas_call(
        _memkv_kernel,
        grid=(n // ROW_TILE,),
        in_specs=[pl.BlockSpec((ROW_TILE, d), lambda i: (i, 0)),
                  pl.BlockSpec((1, d), lambda i: (0, 0)),
                  pl.BlockSpec((d, nw), lambda i: (0, 0))],
        out_specs=pl.BlockSpec((ROW_TILE, nw), lambda i: (i, 0)),
        out_shape=jax.ShapeDtypeStruct((n, nw), BF16),
        compiler_params=_params("arbitrary"),
        name="mem_kv",
    )(mem2, g, w)


def _memattn_kernel(q_ref, km_ref, vm_ref, o_ref, qp0, qp1, vext):
    seq = q_ref.shape[1]
    tq = 2 * ATT_TILE
    lane = lax.broadcasted_iota(jnp.int32, (1, LANES), 1)
    head0 = lane < HEAD_DIM
    m0 = head0.astype(F32)
    q = q_ref[0].astype(F32) * (HEAD_DIM ** -0.5)
    qp0[...] = (q * m0).astype(BF16)
    qp1[...] = (q * (1.0 - m0)).astype(BF16)
    vext[:, 0:LANES] = vm_ref[0]
    vext[:, LANES:2 * LANES] = jnp.ones((vext.shape[0], LANES), BF16)
    km = km_ref[0]

    def tile(m, carry):
        rows = pl.ds(pl.multiple_of(m * tq, tq), tq)
        qc = jnp.concatenate([qp0[rows, :], qp1[rows, :]], axis=0)
        s = _nt_dot(qc, km)
        pe = jnp.exp(s - jnp.max(s, axis=-1, keepdims=True))
        pv = jnp.dot(pe.astype(BF16), vext[...], preferred_element_type=F32)
        o = pv[:, 0:LANES] / pv[:, LANES:2 * LANES]
        o_ref[0, rows, :] = jnp.where(head0, o[0:tq], o[tq:2 * tq]).astype(BF16)
        return carry

    lax.fori_loop(0, seq // tq, tile, 0)


def _memattn(hq3, kvm3, layer, qm_block0, mem_width):
    b, s, _ = hq3.shape
    m = kvm3.shape[1]
    npair = mem_width // LANES
    kv_blocks = 2 * npair
    return pl.pallas_call(
        _memattn_kernel,
        grid=(b, npair),
        in_specs=[pl.BlockSpec((1, s, LANES), lambda i, j: (i, 0, qm_block0 + j)),
                  pl.BlockSpec((1, m, LANES), lambda i, j: (i, 0, layer * kv_blocks + j)),
                  pl.BlockSpec((1, m, LANES), lambda i, j: (i, 0, layer * kv_blocks + npair + j))],
        out_specs=pl.BlockSpec((1, s, LANES), lambda i, j: (i, 0, j)),
        out_shape=jax.ShapeDtypeStruct((b, s, mem_width), BF16),
        scratch_shapes=[pltpu.VMEM((s, LANES), BF16), pltpu.VMEM((s, LANES), BF16),
                        pltpu.VMEM((m, 2 * LANES), BF16)],
        compiler_params=_params("arbitrary", "arbitrary"),
        name="mem_attn",
    )(hq3, kvm3, kvm3)


def _outproj_kernel(x_ref, c_ref, d_ref, m_ref, w_ref, g_ref, wr_ref, x1_ref, xn_ref, aff_ref):
    cat = jnp.concatenate([c_ref[...], d_ref[...], m_ref[...]], axis=1)
    chunk = 2 * LANES
    for c in range(0, w_ref.shape[1], chunk):
        x1_ref[:, c:c + chunk] = x_ref[:, c:c + chunk] + jnp.dot(
            cat, w_ref[:, c:c + chunk], preferred_element_type=F32)
    xn = _rms(x1_ref[...], g_ref[...])
    xn_ref[...] = xn.astype(BF16)
    logits = _nt_dot(wr_ref[...], xn, precision=lax.Precision.HIGHEST)
    ex = jnp.exp(logits - jnp.max(logits, axis=0, keepdims=True))
    aff_ref[...] = ex / jnp.sum(ex, axis=0, keepdims=True)


def _outproj(x2, conv_o, dil_o, mem_o, w, g, wr_t):
    n, d = x2.shape
    e = wr_t.shape[0]
    row = lambda i: (i, 0)
    fixed = lambda i: (0, 0)
    return pl.pallas_call(
        _outproj_kernel,
        grid=(n // ROW_TILE,),
        in_specs=[pl.BlockSpec((ROW_TILE, d), row),
                  pl.BlockSpec((ROW_TILE, conv_o.shape[1]), row),
                  pl.BlockSpec((ROW_TILE, dil_o.shape[1]), row),
                  pl.BlockSpec((ROW_TILE, mem_o.shape[1]), row),
                  pl.BlockSpec(w.shape, fixed),
                  pl.BlockSpec((1, d), fixed),
                  pl.BlockSpec((e, d), fixed)],
        out_specs=[pl.BlockSpec((ROW_TILE, d), row),
                   pl.BlockSpec((ROW_TILE, d), row),
                   pl.BlockSpec((e, ROW_TILE), lambda i: (0, i))],
        out_shape=[jax.ShapeDtypeStruct((n, d), F32),
                   jax.ShapeDtypeStruct((n, d), BF16),
                   jax.ShapeDtypeStruct((e, n), F32)],
        compiler_params=_params("arbitrary"),
        name="outproj_router",
    )(x2, conv_o, dil_o, mem_o, w, g, wr_t)


def _excl_cumsum_lanes(mask_bf16):
    rows, s = mask_bf16.shape
    chunk = 2 * LANES
    acc = jnp.zeros((rows, s), F32)
    for c in range(0, s, chunk):
        src = lax.broadcasted_iota(jnp.int32, (chunk, s), 0) + c
        dst = lax.broadcasted_iota(jnp.int32, (chunk, s), 1)
        tri = jnp.where(src < dst, 1.0, 0.0).astype(BF16)
        acc = acc + jnp.dot(mask_bf16[:, c:c + chunk], tri, preferred_element_type=F32)
    return acc


def _topk_kernel(aff_ref, slot_ref, *, cap):
    aff = aff_ref[...]
    rows = aff.shape[0]
    bits = jnp.zeros((rows, 1), jnp.int32)
    for bit in range(30, -1, -1):
        cand = bits | (1 << bit)
        cnt = jnp.sum(jnp.where(aff >= pltpu.bitcast(cand, F32), 1.0, 0.0), axis=1, keepdims=True)
        bits = jnp.where(cnt >= cap, cand, bits)
    thr = pltpu.bitcast(bits, F32)
    gt = aff > thr
    eq = aff == thr
    need = cap - jnp.sum(jnp.where(gt, 1.0, 0.0), axis=1, keepdims=True)
    rank_eq = _excl_cumsum_lanes(jnp.where(eq, 1.0, 0.0).astype(BF16))
    sel = jnp.where(gt, 1.0, jnp.where(eq & (rank_eq < need), 1.0, 0.0))
    rank = _excl_cumsum_lanes(sel.astype(BF16))
    slot_ref[...] = jnp.where(sel > 0.5, rank, -1.0).astype(jnp.int32)


def _topk(aff_rows, cap):
    return pl.pallas_call(
        functools.partial(_topk_kernel, cap=cap),
        out_shape=jax.ShapeDtypeStruct(aff_rows.shape, jnp.int32),
        compiler_params=pltpu.CompilerParams(vmem_limit_bytes=V7X_VMEM_LIMIT_BYTES),
        name="expert_topk",
    )(aff_rows)


def _dispatch_kernel(slot_ref, xn_ref, xg_ref, *, cap):
    slot = slot_ref[0]
    sidx = lax.broadcasted_iota(jnp.int32, (cap, slot.shape[1]), 0)
    onehot = jnp.where(slot == sidx, 1.0, 0.0).astype(BF16)
    xg_ref[0] = jnp.dot(onehot, xn_ref[0], preferred_element_type=F32).astype(BF16)


def _dispatch(slot3, xn3, n_exp, cap):
    b, s, d = xn3.shape
    return pl.pallas_call(
        functools.partial(_dispatch_kernel, cap=cap),
        grid=(b, n_exp),
        in_specs=[pl.BlockSpec((1, 1, s), lambda i, e: (e * b + i, 0, 0)),
                  pl.BlockSpec((1, s, d), lambda i, e: (i, 0, 0))],
        out_specs=pl.BlockSpec((1, cap, d), lambda i, e: (e, i, 0)),
        out_shape=jax.ShapeDtypeStruct((n_exp, b * cap, d), BF16),
        compiler_params=_params("arbitrary", "arbitrary"),
        name="dispatch",
    )(slot3, xn3)


FF_CHUNK = 512
EXPERT_ROWS = 256


def _expert_kernel(xg_ref, wg_ref, wu_ref, wd_ref, slot_ref, aff_ref, y_ref,
                   acc, wgb, wub, wdb, *, cap):
    f = pl.program_id(1)
    wgb[...] = wg_ref[0, 0].astype(BF16)
    wub[...] = wu_ref[0, 0].astype(BF16)
    wdb[...] = wd_ref[0, 0].astype(BF16)

    @pl.when(f == 0)
    def _():
        acc[...] = jnp.zeros(acc.shape, F32)

    def rows_chunk(m, carry):
        rows = pl.ds(pl.multiple_of(m * EXPERT_ROWS, EXPERT_ROWS), EXPERT_ROWS)
        xm = xg_ref[0, rows, :]
        gate = jnp.dot(xm, wgb[...], preferred_element_type=F32)
        up = jnp.dot(xm, wub[...], preferred_element_type=F32)
        hid = (gate * (1.0 / (1.0 + jnp.exp(-gate))) * up).astype(BF16)
        acc[rows, :] += jnp.dot(hid, wdb[...], preferred_element_type=F32)
        return carry

    lax.fori_loop(0, acc.shape[0] // EXPERT_ROWS, rows_chunk, 0)

    @pl.when(f == pl.num_programs(1) - 1)
    def _():
        for bi in range(slot_ref.shape[0]):
            slot = slot_ref[bi:bi + 1, :]
            sidx = lax.broadcasted_iota(jnp.int32, (cap, slot.shape[1]), 0)
            gate_col = jnp.sum(jnp.where(slot == sidx, aff_ref[bi:bi + 1, :], 0.0),
                               axis=1, keepdims=True)
            rows = slice(bi * cap, (bi + 1) * cap)
            y_ref[0, rows, :] = (acc[rows, :] * gate_col).astype(BF16)


def _experts(xg, w_gate, w_up, w_down, layer, slot_rows, aff_rows, cap):
    n_exp, rows, d = xg.shape
    d_ff = w_gate.shape[3]
    nb = rows // cap
    s = slot_rows.shape[1]
    return pl.pallas_call(
        functools.partial(_expert_kernel, cap=cap),
        grid=(n_exp, d_ff // FF_CHUNK),
        in_specs=[pl.BlockSpec((1, rows, d), lambda e, f: (e, 0, 0)),
                  pl.BlockSpec((1, 1, d, FF_CHUNK), lambda e, f: (layer, e, 0, f)),
                  pl.BlockSpec((1, 1, d, FF_CHUNK), lambda e, f: (layer, e, 0, f)),
                  pl.BlockSpec((1, 1, FF_CHUNK, d), lambda e, f: (layer, e, f, 0)),
                  pl.BlockSpec((nb, s), lambda e, f: (e, 0)),
                  pl.BlockSpec((nb, s), lambda e, f: (e, 0))],
        out_specs=pl.BlockSpec((1, rows, d), lambda e, f: (e, 0, 0)),
        out_shape=jax.ShapeDtypeStruct((n_exp, rows, d), BF16),
        scratch_shapes=[pltpu.VMEM((rows, d), F32),
                        pltpu.VMEM((d, FF_CHUNK), BF16),
                        pltpu.VMEM((d, FF_CHUNK), BF16),
                        pltpu.VMEM((FF_CHUNK, d), BF16)],
        compiler_params=_params("arbitrary", "arbitrary"),
        name="expert_ffn",
    )(xg, w_gate, w_up, w_down, slot_rows, aff_rows)


COMBINE_GROUP = 4


def _combine_kernel(x_ref, slott_ref, y_ref, g_ref, o_ref, *, cap, final):
    slott = slott_ref[0]
    t, n_exp = slott.shape
    d = x_ref.shape[2]
    sidx = lax.broadcasted_iota(jnp.int32, (t, cap), 1)
    acc = x_ref[0]
    for e0 in range(0, n_exp, COMBINE_GROUP):
        parts = [jnp.where(slott[:, e:e + 1] == sidx, 1.0, 0.0).astype(BF16)
                 for e in range(e0, e0 + COMBINE_GROUP)]
        onehot_t = jnp.concatenate(parts, axis=1)
        yg = y_ref[e0:e0 + COMBINE_GROUP, 0].reshape(COMBINE_GROUP * cap, d)
        acc = acc + jnp.dot(onehot_t, yg, preferred_element_type=F32)
    if final:
        acc = _rms(acc, g_ref[...])
    o_ref[0] = acc


def _combine(x3, slott, y4, g, cap, final):
    b, s, d = x3.shape
    n_exp = y4.shape[0]
    return pl.pallas_call(
        functools.partial(_combine_kernel, cap=cap, final=final),
        grid=(b, s // ROW_TILE),
        in_specs=[pl.BlockSpec((1, ROW_TILE, d), lambda i, t: (i, t, 0)),
                  pl.BlockSpec((1, ROW_TILE, n_exp), lambda i, t: (i, t, 0)),
                  pl.BlockSpec((n_exp, 1, cap, d), lambda i, t: (0, i, 0, 0)),
                  pl.BlockSpec((1, d), lambda i, t: (0, 0))],
        out_specs=pl.BlockSpec((1, ROW_TILE, d), lambda i, t: (i, t, 0)),
        out_shape=jax.ShapeDtypeStruct((b, s, d), F32),
        compiler_params=_params("arbitrary", "arbitrary"),
        name="combine",
    )(x3, slott, y4, g)


def kernel(x, mem, mem_norm, norm_mix, w_in, conv_w, w_mem_kv, w_out, norm_ffn,
           w_router, w_gate, w_up, w_down, norm_final):
    b, s, d = x.shape
    depth = w_in.shape[0]
    n_exp = w_router.shape[2]
    conv_ch = conv_w.shape[2]
    mem_width = w_mem_kv.shape[2] // 2
    dil_width = (w_in.shape[2] - 3 * conv_ch - mem_width) // 3
    n_dil_heads = dil_width // HEAD_DIM
    cap = EC_CAPACITY * s // n_exp
    assert n_exp == N_EXPERTS and mem_width == N_MEM_HEADS * HEAD_DIM
    assert all(w // (2 * dl) == HALF_WINDOW for w, dl in DILATED_PATTERNS)

    slopes = jnp.asarray([2.0 ** (-8.0 * (h + 1) / n_dil_heads) for h in range(n_dil_heads)], F32)
    kv_all = jnp.transpose(w_mem_kv, (1, 0, 2)).reshape(d, depth * 2 * mem_width).astype(BF16)
    kvm3 = _memkv(mem.reshape(-1, d), mem_norm.reshape(1, d), kv_all).reshape(b, mem.shape[1], -1)

    x2 = x.reshape(b * s, d)
    out = None
    for l in range(depth):
        hc, hq = _inproj(x2, norm_mix[l].reshape(1, d), w_in[l].astype(BF16), 3 * conv_ch)
        hq3 = hq.reshape(b, s, -1)
        conv_o = _conv(hc.reshape(b, s, -1), conv_w[l], conv_ch)
        dil_o = _dilated(hq3, slopes, dil_width)
        mem_o = _memattn(hq3, kvm3, l, 3 * dil_width // LANES, mem_width)
        x1, xn, aff_t = _outproj(x2, conv_o.reshape(b * s, -1), dil_o.reshape(b * s, -1),
                                 mem_o.reshape(b * s, -1), w_out[l].astype(BF16),
                                 norm_ffn[l].reshape(1, d), jnp.transpose(w_router[l]))
        aff_rows = aff_t.reshape(n_exp * b, s)
        slot_rows = _topk(aff_rows, cap)
        xg = _dispatch(slot_rows.reshape(n_exp * b, 1, s), xn.reshape(b, s, d), n_exp, cap)
        y = _experts(xg, w_gate, w_up, w_down, l, slot_rows, aff_rows, cap)
        slott = jnp.transpose(slot_rows.reshape(n_exp, b, s), (1, 2, 0))
        final = l == depth - 1
        x3 = _combine(x1.reshape(b, s, d), slott, y.reshape(n_exp, b, cap, d),
                      norm_final.reshape(1, d), cap, final)
        x2 = x3.reshape(b * s, d)
        out = x3
    return out
```

```python
import functools

import jax
import jax.numpy as jnp
from jax import lax
from jax.experimental import pallas as pl
from jax.experimental.pallas import tpu as pltpu

F32 = jnp.float32
BF16 = jnp.bfloat16

HEAD_DIM = 64
N_MEM_HEADS = 4
DILATED_PATTERNS = ((128, 1), (512, 4), (2048, 16))
HALF_WINDOW = 64
N_EXPERTS = 16
EC_CAPACITY = 2
RMS_EPS = 1e-6
NEG_INF = -1e30

LANES = 128
ROW_TILE = 512
ATT_TILE = 128
V7X_VMEM_LIMIT_BYTES = 56 * 1024 * 1024


def _params(*semantics):
    return pltpu.CompilerParams(dimension_semantics=semantics,
                                vmem_limit_bytes=V7X_VMEM_LIMIT_BYTES)


def _rms(x, g):
    ms = jnp.mean(x * x, axis=-1, keepdims=True)
    return x * lax.rsqrt(ms + RMS_EPS) * g


def _nt_dot(a, b, **kw):
    return lax.dot_general(a, b, (((1,), (1,)), ((), ())), preferred_element_type=F32, **kw)


def _inproj_kernel(x_ref, g_ref, w_ref, hc_ref, hq_ref):
    xn = _rms(x_ref[...], g_ref[...]).astype(BF16)
    n_conv = hc_ref.shape[1]
    chunk = 2 * LANES
    for c in range(0, w_ref.shape[1], chunk):
        h = jnp.dot(xn, w_ref[:, c:c + chunk], preferred_element_type=F32)
        if c + chunk <= n_conv:
            hc_ref[:, c:c + chunk] = h
        elif c >= n_conv:
            hq_ref[:, c - n_conv:c - n_conv + chunk] = h.astype(BF16)
        else:
            k = n_conv - c
            hc_ref[:, c:n_conv] = h[:, :k]
            hq_ref[:, 0:chunk - k] = h[:, k:].astype(BF16)


def _inproj(x2, g, w, n_conv):
    n, d = x2.shape
    n_all = w.shape[1]
    return pl.pallas_call(
        _inproj_kernel,
        grid=(n // ROW_TILE,),
        in_specs=[pl.BlockSpec((ROW_TILE, d), lambda i: (i, 0)),
                  pl.BlockSpec((1, d), lambda i: (0, 0)),
                  pl.BlockSpec((d, n_all), lambda i: (0, 0))],
        out_specs=[pl.BlockSpec((ROW_TILE, n_conv), lambda i: (i, 0)),
                   pl.BlockSpec((ROW_TILE, n_all - n_conv), lambda i: (i, 0))],
        out_shape=[jax.ShapeDtypeStruct((n, n_conv), F32),
                   jax.ShapeDtypeStruct((n, n_all - n_conv), BF16)],
        compiler_params=_params("arbitrary"),
        name="inproj",
    )(x2, g, w)


def _conv_kernel(xc_ref, bg_ref, cg_ref, w_ref, o_ref):
    u = cg_ref[0] * xc_ref[0]
    s = u.shape[0]
    row = lax.broadcasted_iota(jnp.int32, u.shape, 0)
    prev = jnp.where(row == 0, 0.0, pltpu.roll(u, 1, 0))
    nxt = jnp.where(row == s - 1, 0.0, pltpu.roll(u, s - 1, 0))
    w = w_ref[...]
    conv = w[0:1] * prev + w[1:2] * u + w[2:3] * nxt
    o_ref[0] = (bg_ref[0] * conv).astype(BF16)


def _conv(hc3, conv_w, conv_ch):
    b, s, _ = hc3.shape
    nblk = conv_ch // LANES
    blk = (1, s, LANES)
    return pl.pallas_call(
        _conv_kernel,
        grid=(b, nblk),
        in_specs=[pl.BlockSpec(blk, lambda i, j: (i, 0, j)),
                  pl.BlockSpec(blk, lambda i, j: (i, 0, nblk + j)),
                  pl.BlockSpec(blk, lambda i, j: (i, 0, 2 * nblk + j)),
                  pl.BlockSpec((conv_w.shape[0], LANES), lambda i, j: (0, j))],
        out_specs=pl.BlockSpec(blk, lambda i, j: (i, 0, j)),
        out_shape=jax.ShapeDtypeStruct((b, s, conv_ch), BF16),
        compiler_params=_params("arbitrary", "arbitrary"),
        name="shortconv",
    )(hc3, hc3, hc3, conv_w)


DIL_TILE_UNROLL = 8


def _dil_cases(seq, dil):
    tps = (seq // dil) // ATT_TILE
    return tps, (("single",) if tps == 1 else ("first", "interior", "last"))


def _dil_kernel(slopes_ref, q_ref, k_ref, v_ref, o_ref,
                natq, natk, natv, perq, perk, perv, qp0, qp1, kp, vp, bias,
                num1, mx1, sum1, num2, mx2, sum2, num3, mx3, sum3, onat):
    seq = q_ref.shape[1]
    pad = HALF_WINDOW
    tq = ATT_TILE
    step = DILATED_PATTERNS[1][1]
    seg2 = seq // step
    seg3 = seg2 // step
    log_seg2 = seg2.bit_length() - 1
    log_step = step.bit_length() - 1
    pair = pl.program_id(0)
    lane = lax.broadcasted_iota(jnp.int32, (1, LANES), 1)
    head0 = lane < HEAD_DIM
    m0 = head0.astype(F32)
    m1 = 1.0 - m0

    @pl.when(pl.program_id(1) == 0)
    def _():
        kp[0:pad, :] = jnp.zeros((pad, LANES), BF16)
        kp[pad + seq:pad + seq + pad, :] = jnp.zeros((pad, LANES), BF16)
        vp[0:pad, :] = jnp.zeros((pad, 2 * LANES), BF16)
        vp[pad + seq:pad + seq + pad, :] = jnp.zeros((pad, 2 * LANES), BF16)
        vp[pad:pad + seq, LANES:2 * LANES] = jnp.ones((seq, LANES), BF16)
        row = lax.broadcasted_iota(jnp.int32, (2 * tq, 2 * tq), 0)
        col = lax.broadcasted_iota(jnp.int32, (2 * tq, 2 * tq), 1)
        kc = col - pad
        arel = jnp.abs(kc - (row & (tq - 1)))
        band = arel <= HALF_WINDOW
        slope_rows = jnp.where(row < tq, slopes_ref[2 * pair], slopes_ref[2 * pair + 1])
        idx = 0
        for _, dil in DILATED_PATTERNS:
            dist_bias = -slope_rows * (arel * dil).astype(F32)
            for case in _dil_cases(seq, dil)[1]:
                valid = band
                if case in ("first", "single"):
                    valid = valid & (kc >= 0)
                if case in ("last", "single"):
                    valid = valid & (kc < tq)
                bias[idx] = jnp.where(valid, dist_bias, NEG_INF)
                idx += 1

    natq[...] = q_ref[0].astype(F32) * (HEAD_DIM ** -0.5)
    natk[...] = k_ref[0].astype(F32)
    natv[...] = v_ref[0].astype(F32)

    def set_operands(rows, qv, kv, vv):
        lo, hi = rows
        qp0[lo:hi, :] = (qv * m0).astype(BF16)
        qp1[lo:hi, :] = (qv * m1).astype(BF16)
        kp[pad + lo:pad + hi, :] = kv.astype(BF16)
        vp[pad + lo:pad + hi, 0:LANES] = vv.astype(BF16)

    def run_tiles(case0, tps, dests):
        num_ref, mx_ref, sum_ref = dests

        def tile(m, carry):
            r0 = pl.multiple_of(m * tq, tq)
            qc = jnp.concatenate([qp0[pl.ds(r0, tq), :], qp1[pl.ds(r0, tq), :]], axis=0)
            kt = kp[pl.ds(r0, 2 * tq), :]
            vt = vp[pl.ds(r0, 2 * tq), :]
            if tps == 1:
                case = case0
                dst = pl.ds((m >> log_step) * seg2 + (m & (step - 1)), tq, stride=step)
            else:
                pos = m & (tps - 1)
                case = case0 + jnp.where(pos == 0, 0, jnp.where(pos == tps - 1, 2, 1))
                dst = pl.ds(r0, tq)
            s = _nt_dot(qc, kt) + bias[case]
            mx = jnp.max(s, axis=-1, keepdims=True)
            pe = jnp.exp(s - mx)
            pv = jnp.dot(pe.astype(BF16), vt, preferred_element_type=F32)
            num_ref[dst, :] = jnp.where(head0, pv[0:tq, 0:LANES], pv[tq:2 * tq, 0:LANES])
            sum_ref[dst, :] = jnp.where(head0, pv[0:tq, LANES:2 * LANES],
                                        pv[tq:2 * tq, LANES:2 * LANES])
            mx_ref[dst, :] = jnp.where(head0, mx[0:tq], mx[tq:2 * tq])
            return carry

        lax.fori_loop(0, seq // tq, tile, 0, unroll=DIL_TILE_UNROLL)

    case0 = 0
    set_operands((0, seq), natq[...], natk[...], natv[...])
    tps1, cases1 = _dil_cases(seq, DILATED_PATTERNS[0][1])
    run_tiles(case0, tps1, (num1, mx1, sum1))
    case0 += len(cases1)

    for b in range(step):
        rows = (b * seg2, (b + 1) * seg2)
        qv = natq[pl.ds(b, seg2, stride=step), :]
        kv = natk[pl.ds(b, seg2, stride=step), :]
        vv = natv[pl.ds(b, seg2, stride=step), :]
        perq[rows[0]:rows[1], :] = qv
        perk[rows[0]:rows[1], :] = kv
        perv[rows[0]:rows[1], :] = vv
        set_operands(rows, qv, kv, vv)
    tps2, cases2 = _dil_cases(seq, DILATED_PATTERNS[1][1])
    run_tiles(case0, tps2, (num2, mx2, sum2))
    case0 += len(cases2)

    for sgm in range(step * step):
        b, a = divmod(sgm, step)
        src = pl.ds(b * seg2 + a, seg3, stride=step)
        set_operands((sgm * seg3, (sgm + 1) * seg3), perq[src, :], perk[src, :], perv[src, :])
    tps3, _ = _dil_cases(seq, DILATED_PATTERNS[2][1])
    assert tps3 == 1 and seg3 == tq
    run_tiles(case0, tps3, (num3, mx3, sum3))

    def merge(c, carry):
        p0 = pl.multiple_of(c * tq, tq)
        per_rows = pl.ds(p0, tq)
        nat_rows = pl.ds((p0 & (seg2 - 1)) * step + (p0 >> log_seg2), tq, stride=step)
        a1, a2, a3 = mx1[nat_rows, :], mx2[per_rows, :], mx3[per_rows, :]
        top = jnp.maximum(jnp.maximum(a1, a2), a3)
        e1, e2, e3 = jnp.exp(a1 - top), jnp.exp(a2 - top), jnp.exp(a3 - top)
        den = e1 * sum1[nat_rows, :] + e2 * sum2[per_rows, :] + e3 * sum3[per_rows, :]
        num = e1 * num1[nat_rows, :] + e2 * num2[per_rows, :] + e3 * num3[per_rows, :]
        onat[nat_rows, :] = num / den
        return carry

    lax.fori_loop(0, seq // tq, merge, 0, unroll=2)
    o_ref[0] = onat[...].astype(BF16)


def _dilated(hq3, slopes, dil_width):
    b, s, _ = hq3.shape
    npair = dil_width // LANES
    blk = (1, s, LANES)
    dils = [d for _, d in DILATED_PATTERNS]
    assert dils[0] == 1 and dils[2] == dils[1] * dils[1]
    n_cases = sum(len(_dil_cases(s, d)[1]) for d in dils)
    pad = HALF_WINDOW
    scratch = [pltpu.VMEM((s, LANES), F32)] * 6
    scratch += [pltpu.VMEM((s, LANES), BF16)] * 2
    scratch += [pltpu.VMEM((s + 2 * pad, LANES), BF16), pltpu.VMEM((s + 2 * pad, 2 * LANES), BF16)]
    scratch += [pltpu.VMEM((n_cases, 2 * ATT_TILE, 2 * ATT_TILE), F32)]
    scratch += [pltpu.VMEM((s, LANES), F32)] * 10
    return pl.pallas_call(
        _dil_kernel,
        grid_spec=pltpu.PrefetchScalarGridSpec(
            num_scalar_prefetch=1,
            grid=(npair, b),
            in_specs=[pl.BlockSpec(blk, lambda j, i, sl: (i, 0, j)),
                      pl.BlockSpec(blk, lambda j, i, sl: (i, 0, npair + j)),
                      pl.BlockSpec(blk, lambda j, i, sl: (i, 0, 2 * npair + j))],
            out_specs=pl.BlockSpec(blk, lambda j, i, sl: (i, 0, j)),
            scratch_shapes=scratch),
        out_shape=jax.ShapeDtypeStruct((b, s, dil_width), BF16),
        compiler_params=_params("arbitrary", "arbitrary"),
        name="dilated_attn",
    )(slopes, hq3, hq3, hq3)


def _memkv_kernel(m_ref, g_ref, w_ref, o_ref):
    mn = _rms(m_ref[...], g_ref[...]).astype(BF16)
    o_ref[...] = jnp.dot(mn, w_ref[...], preferred_element_type=F32).astype(BF16)


def _memkv(mem2, g, w):
    n, d = mem2.shape
    nw = w.shape[1]
    return pl.pallas_call(
        _memkv_kernel,
        grid=(n // ROW_TILE,),
        in_specs=[pl.BlockSpec((ROW_TILE, d), lambda i: (i, 0)),
                  pl.BlockSpec((1, d), lambda i: (0, 0)),
                  pl.BlockSpec((d, nw), lambda i: (0, 0))],
        out_specs=pl.BlockSpec((ROW_TILE, nw), lambda i: (i, 0)),
        out_shape=jax.ShapeDtypeStruct((n, nw), BF16),
        compiler_params=_params("arbitrary"),
        name="mem_kv",
    )(mem2, g, w)


def _memattn_kernel(q_ref, km_ref, vm_ref, o_ref, qp0, qp1, vext):
    seq = q_ref.shape[1]
    tq = 2 * ATT_TILE
    lane = lax.broadcasted_iota(jnp.int32, (1, LANES), 1)
    head0 = lane < HEAD_DIM
    m0 = head0.astype(F32)
    q = q_ref[0].astype(F32) * (HEAD_DIM ** -0.5)
    qp0[...] = (q * m0).astype(BF16)
    qp1[...] = (q * (1.0 - m0)).astype(BF16)
    vext[:, 0:LANES] = vm_ref[0]
    vext[:, LANES:2 * LANES] = jnp.ones((vext.shape[0], LANES), BF16)
    km = km_ref[0]

    def tile(m, carry):
        rows = pl.ds(pl.multiple_of(m * tq, tq), tq)
        qc = jnp.concatenate([qp0[rows, :], qp1[rows, :]], axis=0)
        s = _nt_dot(qc, km)
        pe = jnp.exp(s - jnp.max(s, axis=-1, keepdims=True))
        pv = jnp.dot(pe.astype(BF16), vext[...], preferred_element_type=F32)
        o = pv[:, 0:LANES] / pv[:, LANES:2 * LANES]
        o_ref[0, rows, :] = jnp.where(head0, o[0:tq], o[tq:2 * tq]).astype(BF16)
        return carry

    lax.fori_loop(0, seq // tq, tile, 0, unroll=4)


def _memattn(hq3, kvm3, layer, qm_block0, mem_width):
    b, s, _ = hq3.shape
    m = kvm3.shape[1]
    npair = mem_width // LANES
    kv_blocks = 2 * npair
    return pl.pallas_call(
        _memattn_kernel,
        grid=(b, npair),
        in_specs=[pl.BlockSpec((1, s, LANES), lambda i, j: (i, 0, qm_block0 + j)),
                  pl.BlockSpec((1, m, LANES), lambda i, j: (i, 0, layer * kv_blocks + j)),
                  pl.BlockSpec((1, m, LANES), lambda i, j: (i, 0, layer * kv_blocks + npair + j))],
        out_specs=pl.BlockSpec((1, s, LANES), lambda i, j: (i, 0, j)),
        out_shape=jax.ShapeDtypeStruct((b, s, mem_width), BF16),
        scratch_shapes=[pltpu.VMEM((s, LANES), BF16), pltpu.VMEM((s, LANES), BF16),
                        pltpu.VMEM((m, 2 * LANES), BF16)],
        compiler_params=_params("arbitrary", "arbitrary"),
        name="mem_attn",
    )(hq3, kvm3, kvm3)


def _outproj_kernel(x_ref, c_ref, d_ref, m_ref, w_ref, g_ref, wr_ref, x1_ref, xn_ref, aff_ref):
    cat = jnp.concatenate([c_ref[...], d_ref[...], m_ref[...]], axis=1)
    chunk = 2 * LANES
    for c in range(0, w_ref.shape[1], chunk):
        x1_ref[:, c:c + chunk] = x_ref[:, c:c + chunk] + jnp.dot(
            cat, w_ref[:, c:c + chunk], preferred_element_type=F32)
    xn = _rms(x1_ref[...], g_ref[...])
    xn_ref[...] = xn.astype(BF16)
    logits = _nt_dot(wr_ref[...], xn, precision=lax.Precision.HIGHEST)
    ex = jnp.exp(logits - jnp.max(logits, axis=0, keepdims=True))
    aff_ref[...] = ex / jnp.sum(ex, axis=0, keepdims=True)


def _outproj(x2, conv_o, dil_o, mem_o, w, g, wr_t):
    n, d = x2.shape
    e = wr_t.shape[0]
    row = lambda i: (i, 0)
    fixed = lambda i: (0, 0)
    return pl.pallas_call(
        _outproj_kernel,
        grid=(n // ROW_TILE,),
        in_specs=[pl.BlockSpec((ROW_TILE, d), row),
                  pl.BlockSpec((ROW_TILE, conv_o.shape[1]), row),
                  pl.BlockSpec((ROW_TILE, dil_o.shape[1]), row),
                  pl.BlockSpec((ROW_TILE, mem_o.shape[1]), row),
                  pl.BlockSpec(w.shape, fixed),
                  pl.BlockSpec((1, d), fixed),
                  pl.BlockSpec((e, d), fixed)],
        out_specs=[pl.BlockSpec((ROW_TILE, d), row),
                   pl.BlockSpec((ROW_TILE, d), row),
                   pl.BlockSpec((e, ROW_TILE), lambda i: (0, i))],
        out_shape=[jax.ShapeDtypeStruct((n, d), F32),
                   jax.ShapeDtypeStruct((n, d), BF16),
                   jax.ShapeDtypeStruct((e, n), F32)],
        compiler_params=_params("arbitrary"),
        name="outproj_router",
    )(x2, conv_o, dil_o, mem_o, w, g, wr_t)


def _excl_cumsum_lanes(mask_bf16):
    rows, s = mask_bf16.shape
    chunk = 2 * LANES
    acc = jnp.zeros((rows, s), F32)
    for c in range(0, s, chunk):
        src = lax.broadcasted_iota(jnp.int32, (chunk, s), 0) + c
        dst = lax.broadcasted_iota(jnp.int32, (chunk, s), 1)
        tri = jnp.where(src < dst, 1.0, 0.0).astype(BF16)
        acc = acc + jnp.dot(mask_bf16[:, c:c + chunk], tri, preferred_element_type=F32)
    return acc


def _topk_kernel(aff_ref, slot_ref, *, cap):
    aff = aff_ref[...]
    rows = aff.shape[0]
    bits = jnp.zeros((rows, 1), jnp.int32)
    for bit in range(30, -1, -1):
        cand = bits | (1 << bit)
        cnt = jnp.sum(jnp.where(aff >= pltpu.bitcast(cand, F32), 1.0, 0.0), axis=1, keepdims=True)
        bits = jnp.where(cnt >= cap, cand, bits)
    thr = pltpu.bitcast(bits, F32)
    gt = aff > thr
    eq = aff == thr
    need = cap - jnp.sum(jnp.where(gt, 1.0, 0.0), axis=1, keepdims=True)
    rank_eq = _excl_cumsum_lanes(jnp.where(eq, 1.0, 0.0).astype(BF16))
    sel = jnp.where(gt, 1.0, jnp.where(eq & (rank_eq < need), 1.0, 0.0))
    rank = _excl_cumsum_lanes(sel.astype(BF16))
    slot_ref[...] = jnp.where(sel > 0.5, rank, -1.0).astype(jnp.int32)


def _topk(aff_rows, cap):
    return pl.pallas_call(
        functools.partial(_topk_kernel, cap=cap),
        out_shape=jax.ShapeDtypeStruct(aff_rows.shape, jnp.int32),
        compiler_params=pltpu.CompilerParams(vmem_limit_bytes=V7X_VMEM_LIMIT_BYTES),
        name="expert_topk",
    )(aff_rows)


def _dispatch_kernel(slot_ref, xn_ref, xg_ref, *, cap):
    slot = slot_ref[0]
    sidx = lax.broadcasted_iota(jnp.int32, (cap, slot.shape[1]), 0)
    onehot = jnp.where(slot == sidx, 1.0, 0.0).astype(BF16)
    xg_ref[0] = jnp.dot(onehot, xn_ref[0], preferred_element_type=F32).astype(BF16)


def _dispatch(slot3, xn3, n_exp, cap):
    b, s, d = xn3.shape
    return pl.pallas_call(
        functools.partial(_dispatch_kernel, cap=cap),
        grid=(b, n_exp),
        in_specs=[pl.BlockSpec((1, 1, s), lambda i, e: (e * b + i, 0, 0)),
                  pl.BlockSpec((1, s, d), lambda i, e: (i, 0, 0))],
        out_specs=pl.BlockSpec((1, cap, d), lambda i, e: (e, i, 0)),
        out_shape=jax.ShapeDtypeStruct((n_exp, b * cap, d), BF16),
        compiler_params=_params("arbitrary", "arbitrary"),
        name="dispatch",
    )(slot3, xn3)


FF_CHUNK = 512
EXPERT_ROWS = 256


def _expert_kernel(xg_ref, wg_ref, wu_ref, wd_ref, slot_ref, aff_ref, y_ref,
                   acc, wgb, wub, wdb, *, cap):
    f = pl.program_id(1)
    wgb[...] = wg_ref[0, 0].astype(BF16)
    wub[...] = wu_ref[0, 0].astype(BF16)
    wdb[...] = wd_ref[0, 0].astype(BF16)

    @pl.when(f == 0)
    def _():
        acc[...] = jnp.zeros(acc.shape, F32)

    for r0 in range(0, acc.shape[0], EXPERT_ROWS):
        rows = slice(r0, r0 + EXPERT_ROWS)
        xm = xg_ref[0, rows, :]
        gate = jnp.dot(xm, wgb[...], preferred_element_type=F32)
        up = jnp.dot(xm, wub[...], preferred_element_type=F32)
        hid = (gate * (1.0 / (1.0 + jnp.exp(-gate))) * up).astype(BF16)
        acc[rows, :] += jnp.dot(hid, wdb[...], preferred_element_type=F32)

    @pl.when(f == pl.num_programs(1) - 1)
    def _():
        for bi in range(slot_ref.shape[0]):
            slot = slot_ref[bi:bi + 1, :]
            sidx = lax.broadcasted_iota(jnp.int32, (cap, slot.shape[1]), 0)
            gate_col = jnp.sum(jnp.where(slot == sidx, aff_ref[bi:bi + 1, :], 0.0),
                               axis=1, keepdims=True)
            rows = slice(bi * cap, (bi + 1) * cap)
            y_ref[0, rows, :] = (acc[rows, :] * gate_col).astype(BF16)


def _experts(xg, w_gate, w_up, w_down, layer, slot_rows, aff_rows, cap):
    n_exp, rows, d = xg.shape
    d_ff = w_gate.shape[3]
    nb = rows // cap
    s = slot_rows.shape[1]
    return pl.pallas_call(
        functools.partial(_expert_kernel, cap=cap),
        grid=(n_exp, d_ff // FF_CHUNK),
        in_specs=[pl.BlockSpec((1, rows, d), lambda e, f: (e, 0, 0)),
                  pl.BlockSpec((1, 1, d, FF_CHUNK), lambda e, f: (layer, e, 0, f)),
                  pl.BlockSpec((1, 1, d, FF_CHUNK), lambda e, f: (layer, e, 0, f)),
                  pl.BlockSpec((1, 1, FF_CHUNK, d), lambda e, f: (layer, e, f, 0)),
                  pl.BlockSpec((nb, s), lambda e, f: (e, 0)),
                  pl.BlockSpec((nb, s), lambda e, f: (e, 0))],
        out_specs=pl.BlockSpec((1, rows, d), lambda e, f: (e, 0, 0)),
        out_shape=jax.ShapeDtypeStruct((n_exp, rows, d), BF16),
        scratch_shapes=[pltpu.VMEM((rows, d), F32),
                        pltpu.VMEM((d, FF_CHUNK), BF16),
                        pltpu.VMEM((d, FF_CHUNK), BF16),
                        pltpu.VMEM((FF_CHUNK, d), BF16)],
        compiler_params=_params("arbitrary", "arbitrary"),
        name="expert_ffn",
    )(xg, w_gate, w_up, w_down, slot_rows, aff_rows)


COMBINE_GROUP = 4


def _combine_kernel(x_ref, slott_ref, y_ref, g_ref, o_ref, *, cap, final):
    slott = slott_ref[0]
    t, n_exp = slott.shape
    d = x_ref.shape[2]
    sidx = lax.broadcasted_iota(jnp.int32, (t, cap), 1)
    acc = x_ref[0]
    for e0 in range(0, n_exp, COMBINE_GROUP):
        parts = [jnp.where(slott[:, e:e + 1] == sidx, 1.0, 0.0).astype(BF16)
                 for e in range(e0, e0 + COMBINE_GROUP)]
        onehot_t = jnp.concatenate(parts, axis=1)
        yg = y_ref[e0:e0 + COMBINE_GROUP, 0].reshape(COMBINE_GROUP * cap, d)
        acc = acc + jnp.dot(onehot_t, yg, preferred_element_type=F32)
    if final:
        acc = _rms(acc, g_ref[...])
    o_ref[0] = acc


def _combine(x3, slott, y4, g, cap, final):
    b, s, d = x3.shape
    n_exp = y4.shape[0]
    return pl.pallas_call(
        functools.partial(_combine_kernel, cap=cap, final=final),
        grid=(b, s // ROW_TILE),
        in_specs=[pl.BlockSpec((1, ROW_TILE, d), lambda i, t: (i, t, 0)),
                  pl.BlockSpec((1, ROW_TILE, n_exp), lambda i, t: (i, t, 0)),
                  pl.BlockSpec((n_exp, 1, cap, d), lambda i, t: (0, i, 0, 0)),
                  pl.BlockSpec((1, d), lambda i, t: (0, 0))],
        out_specs=pl.BlockSpec((1, ROW_TILE, d), lambda i, t: (i, t, 0)),
        out_shape=jax.ShapeDtypeStruct((b, s, d), F32),
        compiler_params=_params("arbitrary", "arbitrary"),
        name="combine",
    )(x3, slott, y4, g)


def kernel(x, mem, mem_norm, norm_mix, w_in, conv_w, w_mem_kv, w_out, norm_ffn,
           w_router, w_gate, w_up, w_down, norm_final):
    b, s, d = x.shape
    depth = w_in.shape[0]
    n_exp = w_router.shape[2]
    conv_ch = conv_w.shape[2]
    mem_width = w_mem_kv.shape[2] // 2
    dil_width = (w_in.shape[2] - 3 * conv_ch - mem_width) // 3
    n_dil_heads = dil_width // HEAD_DIM
    cap = EC_CAPACITY * s // n_exp
    assert n_exp == N_EXPERTS and mem_width == N_MEM_HEADS * HEAD_DIM
    assert all(w // (2 * dl) == HALF_WINDOW for w, dl in DILATED_PATTERNS)

    slopes = jnp.asarray([2.0 ** (-8.0 * (h + 1) / n_dil_heads) for h in range(n_dil_heads)], F32)
    kv_all = jnp.transpose(w_mem_kv, (1, 0, 2)).reshape(d, depth * 2 * mem_width).astype(BF16)
    kvm3 = _memkv(mem.reshape(-1, d), mem_norm.reshape(1, d), kv_all).reshape(b, mem.shape[1], -1)

    x2 = x.reshape(b * s, d)
    out = None
    for l in range(depth):
        hc, hq = _inproj(x2, norm_mix[l].reshape(1, d), w_in[l].astype(BF16), 3 * conv_ch)
        hq3 = hq.reshape(b, s, -1)
        conv_o = _conv(hc.reshape(b, s, -1), conv_w[l], conv_ch)
        dil_o = _dilated(hq3, slopes, dil_width)
        mem_o = _memattn(hq3, kvm3, l, 3 * dil_width // LANES, mem_width)
        x1, xn, aff_t = _outproj(x2, conv_o.reshape(b * s, -1), dil_o.reshape(b * s, -1),
                                 mem_o.reshape(b * s, -1), w_out[l].astype(BF16),
                                 norm_ffn[l].reshape(1, d), jnp.transpose(w_router[l]))
        aff_rows = aff_t.reshape(n_exp * b, s)
        slot_rows = _topk(aff_rows, cap)
        xg = _dispatch(slot_rows.reshape(n_exp * b, 1, s), xn.reshape(b, s, d), n_exp, cap)
        y = _experts(xg, w_gate, w_up, w_down, l, slot_rows, aff_rows, cap)
        slott = jnp.transpose(slot_rows.reshape(n_exp, b, s), (1, 2, 0))
        final = l == depth - 1
        x3 = _combine(x1.reshape(b, s, d), slott, y.reshape(n_exp, b, cap, d),
                      norm_final.reshape(1, d), cap, final)
        x2 = x3.reshape(b * s, d)
        out = x3
    return out
```

```python
import functools

import jax
import jax.numpy as jnp
from jax import lax
from jax.experimental import pallas as pl
from jax.experimental.pallas import tpu as pltpu

F32 = jnp.float32
BF16 = jnp.bfloat16

HEAD_DIM = 64
N_MEM_HEADS = 4
DILATED_PATTERNS = ((128, 1), (512, 4), (2048, 16))
HALF_WINDOW = 64
N_EXPERTS = 16
EC_CAPACITY = 2
RMS_EPS = 1e-6
NEG_INF = -1e30

LANES = 128
ROW_TILE = 512
ATT_TILE = 128
V7X_VMEM_LIMIT_BYTES = 56 * 1024 * 1024


def _params(*semantics):
    return pltpu.CompilerParams(dimension_semantics=semantics,
                                vmem_limit_bytes=V7X_VMEM_LIMIT_BYTES)


def _rms(x, g):
    ms = jnp.mean(x * x, axis=-1, keepdims=True)
    return x * lax.rsqrt(ms + RMS_EPS) * g


def _nt_dot(a, b, **kw):
    return lax.dot_general(a, b, (((1,), (1,)), ((), ())), preferred_element_type=F32, **kw)


def _inproj_kernel(x_ref, g_ref, w_ref, hc_ref, hq_ref):
    xn = _rms(x_ref[...], g_ref[...]).astype(BF16)
    n_conv = hc_ref.shape[1]
    chunk = 2 * LANES
    for c in range(0, w_ref.shape[1], chunk):
        h = jnp.dot(xn, w_ref[:, c:c + chunk], preferred_element_type=F32)
        if c + chunk <= n_conv:
            hc_ref[:, c:c + chunk] = h
        elif c >= n_conv:
            hq_ref[:, c - n_conv:c - n_conv + chunk] = h.astype(BF16)
        else:
            k = n_conv - c
            hc_ref[:, c:n_conv] = h[:, :k]
            hq_ref[:, 0:chunk - k] = h[:, k:].astype(BF16)


def _inproj(x2, g, w, n_conv):
    n, d = x2.shape
    n_all = w.shape[1]
    return pl.pallas_call(
        _inproj_kernel,
        grid=(n // ROW_TILE,),
        in_specs=[pl.BlockSpec((ROW_TILE, d), lambda i: (i, 0)),
                  pl.BlockSpec((1, d), lambda i: (0, 0)),
                  pl.BlockSpec((d, n_all), lambda i: (0, 0))],
        out_specs=[pl.BlockSpec((ROW_TILE, n_conv), lambda i: (i, 0)),
                   pl.BlockSpec((ROW_TILE, n_all - n_conv), lambda i: (i, 0))],
        out_shape=[jax.ShapeDtypeStruct((n, n_conv), F32),
                   jax.ShapeDtypeStruct((n, n_all - n_conv), BF16)],
        compiler_params=_params("arbitrary"),
        name="inproj",
    )(x2, g, w)


def _conv_kernel(xc_ref, bg_ref, cg_ref, w_ref, o_ref):
    u = cg_ref[0] * xc_ref[0]
    s = u.shape[0]
    row = lax.broadcasted_iota(jnp.int32, u.shape, 0)
    prev = jnp.where(row == 0, 0.0, pltpu.roll(u, 1, 0))
    nxt = jnp.where(row == s - 1, 0.0, pltpu.roll(u, s - 1, 0))
    w = w_ref[...]
    conv = w[0:1] * prev + w[1:2] * u + w[2:3] * nxt
    o_ref[0] = (bg_ref[0] * conv).astype(BF16)


def _conv(hc3, conv_w, conv_ch):
    b, s, _ = hc3.shape
    nblk = conv_ch // LANES
    blk = (1, s, LANES)
    return pl.pallas_call(
        _conv_kernel,
        grid=(b, nblk),
        in_specs=[pl.BlockSpec(blk, lambda i, j: (i, 0, j)),
                  pl.BlockSpec(blk, lambda i, j: (i, 0, nblk + j)),
                  pl.BlockSpec(blk, lambda i, j: (i, 0, 2 * nblk + j)),
                  pl.BlockSpec((conv_w.shape[0], LANES), lambda i, j: (0, j))],
        out_specs=pl.BlockSpec(blk, lambda i, j: (i, 0, j)),
        out_shape=jax.ShapeDtypeStruct((b, s, conv_ch), BF16),
        compiler_params=_params("arbitrary", "arbitrary"),
        name="shortconv",
    )(hc3, hc3, hc3, conv_w)


DIL_TILE_UNROLL = 8


def _dil_cases(seq, dil):
    tps = (seq // dil) // ATT_TILE
    return tps, (("single",) if tps == 1 else ("first", "interior", "last"))


def _dil_kernel(slopes_ref, q_ref, k_ref, v_ref, o_ref,
                natq, natk, natv, perq, perk, perv, qp0, qp1, kp, vp, bias,
                num1, mx1, sum1, num2, mx2, sum2, num3, mx3, sum3, onat):
    seq = q_ref.shape[1]
    pad = HALF_WINDOW
    tq = ATT_TILE
    step = DILATED_PATTERNS[1][1]
    seg2 = seq // step
    seg3 = seg2 // step
    log_seg2 = seg2.bit_length() - 1
    log_step = step.bit_length() - 1
    pair = pl.program_id(0)
    lane = lax.broadcasted_iota(jnp.int32, (1, LANES), 1)
    head0 = lane < HEAD_DIM
    m0 = head0.astype(F32)
    m1 = 1.0 - m0

    @pl.when(pl.program_id(1) == 0)
    def _():
        kp[0:pad, :] = jnp.zeros((pad, LANES), BF16)
        kp[pad + seq:pad + seq + pad, :] = jnp.zeros((pad, LANES), BF16)
        vp[0:pad, :] = jnp.zeros((pad, 2 * LANES), BF16)
        vp[pad + seq:pad + seq + pad, :] = jnp.zeros((pad, 2 * LANES), BF16)
        vp[pad:pad + seq, LANES:2 * LANES] = jnp.ones((seq, LANES), BF16)
        row = lax.broadcasted_iota(jnp.int32, (2 * tq, 2 * tq), 0)
        col = lax.broadcasted_iota(jnp.int32, (2 * tq, 2 * tq), 1)
        kc = col - pad
        arel = jnp.abs(kc - (row & (tq - 1)))
        band = arel <= HALF_WINDOW
        slope_rows = jnp.where(row < tq, slopes_ref[2 * pair], slopes_ref[2 * pair + 1])
        idx = 0
        for _, dil in DILATED_PATTERNS:
            dist_bias = -slope_rows * (arel * dil).astype(F32)
            for case in _dil_cases(seq, dil)[1]:
                valid = band
                if case in ("first", "single"):
                    valid = valid & (kc >= 0)
                if case in ("last", "single"):
                    valid = valid & (kc < tq)
                bias[idx] = jnp.where(valid, dist_bias, NEG_INF)
                idx += 1

    natq[...] = q_ref[0].astype(F32) * (HEAD_DIM ** -0.5)
    natk[...] = k_ref[0].astype(F32)
    natv[...] = v_ref[0].astype(F32)

    def set_operands(rows, qv, kv, vv):
        lo, hi = rows
        qp0[lo:hi, :] = (qv * m0).astype(BF16)
        qp1[lo:hi, :] = (qv * m1).astype(BF16)
        kp[pad + lo:pad + hi, :] = kv.astype(BF16)
        vp[pad + lo:pad + hi, 0:LANES] = vv.astype(BF16)

    def run_tiles(case0, tps, dests):
        num_ref, mx_ref, sum_ref = dests

        def tile(m, carry):
            r0 = pl.multiple_of(m * tq, tq)
            qc = jnp.concatenate([qp0[pl.ds(r0, tq), :], qp1[pl.ds(r0, tq), :]], axis=0)
            kt = kp[pl.ds(r0, 2 * tq), :]
            vt = vp[pl.ds(r0, 2 * tq), :]
            if tps == 1:
                case = case0
                dst = pl.ds((m >> log_step) * seg2 + (m & (step - 1)), tq, stride=step)
            else:
                pos = m & (tps - 1)
                case = case0 + jnp.where(pos == 0, 0, jnp.where(pos == tps - 1, 2, 1))
                dst = pl.ds(r0, tq)
            s = _nt_dot(qc, kt) + bias[case]
            mx = jnp.max(s, axis=-1, keepdims=True)
            pe = jnp.exp(s - mx)
            pv = jnp.dot(pe.astype(BF16), vt, preferred_element_type=F32)
            num_ref[dst, :] = jnp.where(head0, pv[0:tq, 0:LANES], pv[tq:2 * tq, 0:LANES])
            sum_ref[dst, :] = jnp.where(head0, pv[0:tq, LANES:2 * LANES],
                                        pv[tq:2 * tq, LANES:2 * LANES])
            mx_ref[dst, :] = jnp.where(head0, mx[0:tq], mx[tq:2 * tq])
            return carry

        lax.fori_loop(0, seq // tq, tile, 0, unroll=DIL_TILE_UNROLL)

    case0 = 0
    set_operands((0, seq), natq[...], natk[...], natv[...])
    tps1, cases1 = _dil_cases(seq, DILATED_PATTERNS[0][1])
    run_tiles(case0, tps1, (num1, mx1, sum1))
    case0 += len(cases1)

    for b in range(step):
        rows = (b * seg2, (b + 1) * seg2)
        qv = natq[pl.ds(b, seg2, stride=step), :]
        kv = natk[pl.ds(b, seg2, stride=step), :]
        vv = natv[pl.ds(b, seg2, stride=step), :]
        perq[rows[0]:rows[1], :] = qv
        perk[rows[0]:rows[1], :] = kv
        perv[rows[0]:rows[1], :] = vv
        set_operands(rows, qv, kv, vv)
    tps2, cases2 = _dil_cases(seq, DILATED_PATTERNS[1][1])
    run_tiles(case0, tps2, (num2, mx2, sum2))
    case0 += len(cases2)

    for sgm in range(step * step):
        b, a = divmod(sgm, step)
        src = pl.ds(b * seg2 + a, seg3, stride=step)
        set_operands((sgm * seg3, (sgm + 1) * seg3), perq[src, :], perk[src, :], perv[src, :])
    tps3, _ = _dil_cases(seq, DILATED_PATTERNS[2][1])
    assert tps3 == 1 and seg3 == tq
    run_tiles(case0, tps3, (num3, mx3, sum3))

    def merge(c, carry):
        p0 = pl.multiple_of(c * tq, tq)
        per_rows = pl.ds(p0, tq)
        nat_rows = pl.ds((p0 & (seg2 - 1)) * step + (p0 >> log_seg2), tq, stride=step)
        a1, a2, a3 = mx1[nat_rows, :], mx2[per_rows, :], mx3[per_rows, :]
        top = jnp.maximum(jnp.maximum(a1, a2), a3)
        e1, e2, e3 = jnp.exp(a1 - top), jnp.exp(a2 - top), jnp.exp(a3 - top)
        den = e1 * sum1[nat_rows, :] + e2 * sum2[per_rows, :] + e3 * sum3[per_rows, :]
        num = e1 * num1[nat_rows, :] + e2 * num2[per_rows, :] + e3 * num3[per_rows, :]
        onat[nat_rows, :] = num / den
        return carry

    lax.fori_loop(0, seq // tq, merge, 0, unroll=2)
    o_ref[0] = onat[...].astype(BF16)


def _dilated(hq3, slopes, dil_width):
    b, s, _ = hq3.shape
    npair = dil_width // LANES
    blk = (1, s, LANES)
    dils = [d for _, d in DILATED_PATTERNS]
    assert dils[0] == 1 and dils[2] == dils[1] * dils[1]
    n_cases = sum(len(_dil_cases(s, d)[1]) for d in dils)
    pad = HALF_WINDOW
    scratch = [pltpu.VMEM((s, LANES), F32)] * 6
    scratch += [pltpu.VMEM((s, LANES), BF16)] * 2
    scratch += [pltpu.VMEM((s + 2 * pad, LANES), BF16), pltpu.VMEM((s + 2 * pad, 2 * LANES), BF16)]
    scratch += [pltpu.VMEM((n_cases, 2 * ATT_TILE, 2 * ATT_TILE), F32)]
    scratch += [pltpu.VMEM((s, LANES), F32)] * 10
    return pl.pallas_call(
        _dil_kernel,
        grid_spec=pltpu.PrefetchScalarGridSpec(
            num_scalar_prefetch=1,
            grid=(npair, b),
            in_specs=[pl.BlockSpec(blk, lambda j, i, sl: (i, 0, j)),
                      pl.BlockSpec(blk, lambda j, i, sl: (i, 0, npair + j)),
                      pl.BlockSpec(blk, lambda j, i, sl: (i, 0, 2 * npair + j))],
            out_specs=pl.BlockSpec(blk, lambda j, i, sl: (i, 0, j)),
            scratch_shapes=scratch),
        out_shape=jax.ShapeDtypeStruct((b, s, dil_width), BF16),
        compiler_params=_params("arbitrary", "arbitrary"),
        name="dilated_attn",
    )(slopes, hq3, hq3, hq3)


def _memkv_kernel(m_ref, g_ref, w_ref, o_ref):
    mn = _rms(m_ref[...], g_ref[...]).astype(BF16)
    o_ref[...] = jnp.dot(mn, w_ref[...], preferred_element_type=F32).astype(BF16)


def _memkv(mem2, g, w):
    n, d = mem2.shape
    nw = w.shape[1]
    return pl.pallas_call(
        _memkv_kernel,
        grid=(n // ROW_TILE,),
        in_specs=[pl.BlockSpec((ROW_TILE, d), lambda i: (i, 0)),
                  pl.BlockSpec((1, d), lambda i: (0, 0)),
                  pl.BlockSpec((d, nw), lambda i: (0, 0))],
        out_specs=pl.BlockSpec((ROW_TILE, nw), lambda i: (i, 0)),
        out_shape=jax.ShapeDtypeStruct((n, nw), BF16),
        compiler_params=_params("arbitrary"),
        name="mem_kv",
    )(mem2, g, w)


def _memattn_kernel(q_ref, km_ref, vm_ref, o_ref, qp0, qp1, vext):
    seq = q_ref.shape[1]
    tq = 2 * ATT_TILE
    lane = lax.broadcasted_iota(jnp.int32, (1, LANES), 1)
    head0 = lane < HEAD_DIM
    m0 = head0.astype(F32)
    q = q_ref[0].astype(F32) * (HEAD_DIM ** -0.5)
    qp0[...] = (q * m0).astype(BF16)
    qp1[...] = (q * (1.0 - m0)).astype(BF16)
    vext[:, 0:LANES] = vm_ref[0]
    vext[:, LANES:2 * LANES] = jnp.ones((vext.shape[0], LANES), BF16)
    km = km_ref[0]

    def tile(m, carry):
        rows = pl.ds(pl.multiple_of(m * tq, tq), tq)
        qc = jnp.concatenate([qp0[rows, :], qp1[rows, :]], axis=0)
        s = _nt_dot(qc, km)
        pe = jnp.exp(s - jnp.max(s, axis=-1, keepdims=True))
        pv = jnp.dot(pe.astype(BF16), vext[...], preferred_element_type=F32)
        o = pv[:, 0:LANES] / pv[:, LANES:2 * LANES]
        o_ref[0, rows, :] = jnp.where(head0, o[0:tq], o[tq:2 * tq]).astype(BF16)
        return carry

    lax.fori_loop(0, seq // tq, tile, 0, unroll=4)


def _memattn(hq3, kvm3, layer, qm_block0, mem_width):
    b, s, _ = hq3.shape
    m = kvm3.shape[1]
    npair = mem_width // LANES
    kv_blocks = 2 * npair
    return pl.pallas_call(
        _memattn_kernel,
        grid=(b, npair),
        in_specs=[pl.BlockSpec((1, s, LANES), lambda i, j: (i, 0, qm_block0 + j)),
                  pl.BlockSpec((1, m, LANES), lambda i, j: (i, 0, layer * kv_blocks + j)),
                  pl.BlockSpec((1, m, LANES), lambda i, j: (i, 0, layer * kv_blocks + npair + j))],
        out_specs=pl.BlockSpec((1, s, LANES), lambda i, j: (i, 0, j)),
        out_shape=jax.ShapeDtypeStruct((b, s, mem_width), BF16),
        scratch_shapes=[pltpu.VMEM((s, LANES), BF16), pltpu.VMEM((s, LANES), BF16),
                        pltpu.VMEM((m, 2 * LANES), BF16)],
        compiler_params=_params("arbitrary", "arbitrary"),
        name="mem_attn",
    )(hq3, kvm3, kvm3)


def _outproj_kernel(x_ref, c_ref, d_ref, m_ref, w_ref, g_ref, wr_ref, x1_ref, xn_ref, aff_ref):
    cat = jnp.concatenate([c_ref[...], d_ref[...], m_ref[...]], axis=1)
    chunk = 2 * LANES
    for c in range(0, w_ref.shape[1], chunk):
        x1_ref[:, c:c + chunk] = x_ref[:, c:c + chunk] + jnp.dot(
            cat, w_ref[:, c:c + chunk], preferred_element_type=F32)
    xn = _rms(x1_ref[...], g_ref[...])
    xn_ref[...] = xn
    logits = _nt_dot(wr_ref[...], xn, precision=lax.Precision.HIGHEST)
    ex = jnp.exp(logits - jnp.max(logits, axis=0, keepdims=True))
    aff_ref[...] = ex / jnp.sum(ex, axis=0, keepdims=True)


def _outproj(x2, conv_o, dil_o, mem_o, w, g, wr_t):
    n, d = x2.shape
    e = wr_t.shape[0]
    row = lambda i: (i, 0)
    fixed = lambda i: (0, 0)
    return pl.pallas_call(
        _outproj_kernel,
        grid=(n // ROW_TILE,),
        in_specs=[pl.BlockSpec((ROW_TILE, d), row),
                  pl.BlockSpec((ROW_TILE, conv_o.shape[1]), row),
                  pl.BlockSpec((ROW_TILE, dil_o.shape[1]), row),
                  pl.BlockSpec((ROW_TILE, mem_o.shape[1]), row),
                  pl.BlockSpec(w.shape, fixed),
                  pl.BlockSpec((1, d), fixed),
                  pl.BlockSpec((e, d), fixed)],
        out_specs=[pl.BlockSpec((ROW_TILE, d), row),
                   pl.BlockSpec((ROW_TILE, d), row),
                   pl.BlockSpec((e, ROW_TILE), lambda i: (0, i))],
        out_shape=[jax.ShapeDtypeStruct((n, d), F32),
                   jax.ShapeDtypeStruct((n, d), F32),
                   jax.ShapeDtypeStruct((e, n), F32)],
        compiler_params=_params("arbitrary"),
        name="outproj_router",
    )(x2, conv_o, dil_o, mem_o, w, g, wr_t)


TOK_DIGIT_BITS = 6
BF16_SUBLANES = 16


def _excl_cumsum_lanes(mask_bf16):
    rows, s = mask_bf16.shape
    chunk = 2 * LANES
    acc = jnp.zeros((rows, s), F32)
    for c in range(0, s, chunk):
        src = lax.broadcasted_iota(jnp.int32, (chunk, s), 0) + c
        dst = lax.broadcasted_iota(jnp.int32, (chunk, s), 1)
        tri = jnp.where(src < dst, 1.0, 0.0).astype(BF16)
        acc = acc + jnp.dot(mask_bf16[:, c:c + chunk], tri, preferred_element_type=F32)
    return acc


def _topk_kernel(aff_ref, slot_ref, idx_ref, gate_ref, *, cap, n_batch):
    aff = aff_ref[...]
    rows, seq = aff.shape
    bits = jnp.zeros((rows, 1), jnp.int32)
    for bit in range(30, -1, -1):
        cand = bits | (1 << bit)
        cnt = jnp.sum(jnp.where(aff >= pltpu.bitcast(cand, F32), 1.0, 0.0), axis=1, keepdims=True)
        bits = jnp.where(cnt >= cap, cand, bits)
    thr = pltpu.bitcast(bits, F32)
    gt = aff > thr
    eq = aff == thr
    need = cap - jnp.sum(jnp.where(gt, 1.0, 0.0), axis=1, keepdims=True)
    rank_eq = _excl_cumsum_lanes(jnp.where(eq, 1.0, 0.0).astype(BF16))
    sel = jnp.where(gt, 1.0, jnp.where(eq & (rank_eq < need), 1.0, 0.0))
    rank = _excl_cumsum_lanes(sel.astype(BF16))
    slot_ref[...] = jnp.where(sel > 0.5, rank, -1.0).astype(jnp.int32)

    tok = lax.broadcasted_iota(jnp.int32, (1, seq), 1)
    tok_hi = (tok >> TOK_DIGIT_BITS).astype(F32)
    tok_lo = (tok & ((1 << TOK_DIGIT_BITS) - 1)).astype(F32)
    sidx = lax.broadcasted_iota(jnp.int32, (cap, seq), 0)
    n_pick = 5
    zeros = jnp.zeros((BF16_SUBLANES - n_pick, seq), F32)

    def invert(r, carry):
        a = aff_ref[pl.ds(r, 1), :]
        a_hi = a.astype(BF16).astype(F32)
        a_mid = (a - a_hi).astype(BF16).astype(F32)
        a_lo = a - a_hi - a_mid
        lhs = jnp.concatenate([tok_hi, tok_lo, a_hi, a_mid, a_lo, zeros], axis=0).astype(BF16)
        onehot = jnp.where(slot_ref[pl.ds(r, 1), :] == sidx, 1.0, 0.0).astype(BF16)
        picked = _nt_dot(lhs, onehot)
        token = (picked[0:1] * (1 << TOK_DIGIT_BITS) + picked[1:2]).astype(jnp.int32)
        idx_ref[pl.ds(r, 1), :] = token + lax.rem(r, n_batch) * seq
        gate_ref[pl.ds(r, 1), :] = picked[2:3] + picked[3:4] + picked[4:5]
        return carry

    lax.fori_loop(0, rows, invert, 0)


def _topk(aff_rows, cap, n_batch):
    rows, _ = aff_rows.shape
    return pl.pallas_call(
        functools.partial(_topk_kernel, cap=cap, n_batch=n_batch),
        out_shape=[jax.ShapeDtypeStruct(aff_rows.shape, jnp.int32),
                   jax.ShapeDtypeStruct((rows, cap), jnp.int32),
                   jax.ShapeDtypeStruct((rows, cap), F32)],
        compiler_params=pltpu.CompilerParams(vmem_limit_bytes=V7X_VMEM_LIMIT_BYTES),
        name="expert_topk",
    )(aff_rows)


FF_CHUNK = 512
EXPERT_ROWS = 256


def _expert_kernel(idx_ref, xn_hbm, wg_ref, wu_ref, wd_ref, gate_ref, y_ref,
                   xbuf, sem, acc, wgb, wub, wdb, *, cap, n_f):
    e = pl.program_id(0)
    f = pl.program_id(1)
    n_exp = pl.num_programs(0)
    rows = acc.shape[0]
    slot = lax.rem(e, 2)
    nxt = 1 - slot
    e_nxt = jnp.minimum(e + 1, n_exp - 1)

    def row_copy(src_row, buf, dst_row):
        return pltpu.make_async_copy(xn_hbm.at[pl.ds(src_row, 1)],
                                     xbuf.at[buf, pl.ds(dst_row, 1)], sem.at[buf])

    def wait_rows(buf):
        pltpu.make_async_copy(xn_hbm.at[pl.ds(0, rows)], xbuf.at[buf], sem.at[buf]).wait()

    @pl.when((e == 0) & (f == 0))
    def _():
        def first(j, carry):
            row_copy(idx_ref[0, j], 0, j).start()
            return carry
        lax.fori_loop(0, rows, first, 0, unroll=8)

    @pl.when(f == 0)
    def _():
        wait_rows(slot)
        acc[...] = jnp.zeros(acc.shape, F32)

    wgb[...] = wg_ref[0, 0].astype(BF16)
    wub[...] = wu_ref[0, 0].astype(BF16)
    wdb[...] = wd_ref[0, 0].astype(BF16)

    per_step = rows // n_f
    per_chunk = per_step // (rows // EXPERT_ROWS)
    base = pl.multiple_of(f * per_step, per_step)
    for ci, r0 in enumerate(range(0, rows, EXPERT_ROWS)):
        for k in range(ci * per_chunk, (ci + 1) * per_chunk):
            row_copy(idx_ref[e_nxt, base + k], nxt, base + k).start()
        rsl = slice(r0, r0 + EXPERT_ROWS)
        xm = xbuf[slot, rsl, :].astype(BF16)
        gate = jnp.dot(xm, wgb[...], preferred_element_type=F32)
        up = jnp.dot(xm, wub[...], preferred_element_type=F32)
        hid = (gate * (1.0 / (1.0 + jnp.exp(-gate))) * up).astype(BF16)
        acc[rsl, :] += jnp.dot(hid, wdb[...], preferred_element_type=F32)

    @pl.when(f == n_f - 1)
    def _():
        eye = (lax.broadcasted_iota(jnp.int32, (cap, cap), 0)
               == lax.broadcasted_iota(jnp.int32, (cap, cap), 1))
        for bi in range(gate_ref.shape[0]):
            gate_col = jnp.sum(jnp.where(eye, gate_ref[bi:bi + 1, :], 0.0), axis=1, keepdims=True)
            rsl = slice(bi * cap, (bi + 1) * cap)
            y_ref[0, rsl, :] = (acc[rsl, :] * gate_col).astype(BF16)

    @pl.when((e == n_exp - 1) & (f == n_f - 1))
    def _():
        wait_rows(nxt)


def _experts(idx2, xn2, w_gate, w_up, w_down, layer, gate_rows, cap):
    n_exp, rows = idx2.shape
    d = xn2.shape[1]
    d_ff = w_gate.shape[3]
    nb = rows // cap
    n_f = d_ff // FF_CHUNK
    assert rows % (n_f * (rows // EXPERT_ROWS)) == 0
    return pl.pallas_call(
        functools.partial(_expert_kernel, cap=cap, n_f=n_f),
        grid_spec=pltpu.PrefetchScalarGridSpec(
            num_scalar_prefetch=1,
            grid=(n_exp, n_f),
            in_specs=[pl.BlockSpec(memory_space=pl.ANY),
                      pl.BlockSpec((1, 1, d, FF_CHUNK), lambda e, f, ix: (layer, e, 0, f)),
                      pl.BlockSpec((1, 1, d, FF_CHUNK), lambda e, f, ix: (layer, e, 0, f)),
                      pl.BlockSpec((1, 1, FF_CHUNK, d), lambda e, f, ix: (layer, e, f, 0)),
                      pl.BlockSpec((nb, cap), lambda e, f, ix: (e, 0))],
            out_specs=pl.BlockSpec((1, rows, d), lambda e, f, ix: (e, 0, 0)),
            scratch_shapes=[pltpu.VMEM((2, rows, d), F32),
                            pltpu.SemaphoreType.DMA((2,)),
                            pltpu.VMEM((rows, d), F32),
                            pltpu.VMEM((d, FF_CHUNK), BF16),
                            pltpu.VMEM((d, FF_CHUNK), BF16),
                            pltpu.VMEM((FF_CHUNK, d), BF16)]),
        out_shape=jax.ShapeDtypeStruct((n_exp, rows, d), BF16),
        compiler_params=_params("arbitrary", "arbitrary"),
        name="expert_ffn",
    )(idx2, xn2, w_gate, w_up, w_down, gate_rows)


COMBINE_GROUP = 4


def _combine_kernel(x_ref, slott_ref, y_ref, g_ref, o_ref, *, cap, final):
    slott = slott_ref[0]
    t, n_exp = slott.shape
    d = x_ref.shape[2]
    sidx = lax.broadcasted_iota(jnp.int32, (t, cap), 1)
    acc = x_ref[0]
    for e0 in range(0, n_exp, COMBINE_GROUP):
        parts = [jnp.where(slott[:, e:e + 1] == sidx, 1.0, 0.0).astype(BF16)
                 for e in range(e0, e0 + COMBINE_GROUP)]
        onehot_t = jnp.concatenate(parts, axis=1)
        yg = y_ref[e0:e0 + COMBINE_GROUP, 0].reshape(COMBINE_GROUP * cap, d)
        acc = acc + jnp.dot(onehot_t, yg, preferred_element_type=F32)
    if final:
        acc = _rms(acc, g_ref[...])
    o_ref[0] = acc


def _combine(x3, slott, y4, g, cap, final):
    b, s, d = x3.shape
    n_exp = y4.shape[0]
    return pl.pallas_call(
        functools.partial(_combine_kernel, cap=cap, final=final),
        grid=(b, s // ROW_TILE),
        in_specs=[pl.BlockSpec((1, ROW_TILE, d), lambda i, t: (i, t, 0)),
                  pl.BlockSpec((1, ROW_TILE, n_exp), lambda i, t: (i, t, 0)),
                  pl.BlockSpec((n_exp, 1, cap, d), lambda i, t: (0, i, 0, 0)),
                  pl.BlockSpec((1, d), lambda i, t: (0, 0))],
        out_specs=pl.BlockSpec((1, ROW_TILE, d), lambda i, t: (i, t, 0)),
        out_shape=jax.ShapeDtypeStruct((b, s, d), F32),
        compiler_params=_params("arbitrary", "arbitrary"),
        name="combine",
    )(x3, slott, y4, g)


def kernel(x, mem, mem_norm, norm_mix, w_in, conv_w, w_mem_kv, w_out, norm_ffn,
           w_router, w_gate, w_up, w_down, norm_final):
    b, s, d = x.shape
    depth = w_in.shape[0]
    n_exp = w_router.shape[2]
    conv_ch = conv_w.shape[2]
    mem_width = w_mem_kv.shape[2] // 2
    dil_width = (w_in.shape[2] - 3 * conv_ch - mem_width) // 3
    n_dil_heads = dil_width // HEAD_DIM
    cap = EC_CAPACITY * s // n_exp
    assert n_exp == N_EXPERTS and mem_width == N_MEM_HEADS * HEAD_DIM
    assert all(w // (2 * dl) == HALF_WINDOW for w, dl in DILATED_PATTERNS)

    slopes = jnp.asarray([2.0 ** (-8.0 * (h + 1) / n_dil_heads) for h in range(n_dil_heads)], F32)
    kv_all = jnp.transpose(w_mem_kv, (1, 0, 2)).reshape(d, depth * 2 * mem_width).astype(BF16)
    kvm3 = _memkv(mem.reshape(-1, d), mem_norm.reshape(1, d), kv_all).reshape(b, mem.shape[1], -1)

    x2 = x.reshape(b * s, d)
    out = None
    for l in range(depth):
        hc, hq = _inproj(x2, norm_mix[l].reshape(1, d), w_in[l].astype(BF16), 3 * conv_ch)
        hq3 = hq.reshape(b, s, -1)
        conv_o = _conv(hc.reshape(b, s, -1), conv_w[l], conv_ch)
        dil_o = _dilated(hq3, slopes, dil_width)
        mem_o = _memattn(hq3, kvm3, l, 3 * dil_width // LANES, mem_width)
        x1, xn, aff_t = _outproj(x2, conv_o.reshape(b * s, -1), dil_o.reshape(b * s, -1),
                                 mem_o.reshape(b * s, -1), w_out[l].astype(BF16),
                                 norm_ffn[l].reshape(1, d), jnp.transpose(w_router[l]))
        aff_rows = aff_t.reshape(n_exp * b, s)
        slot_rows, idx_rows, gate_rows = _topk(aff_rows, cap, b)
        y = _experts(idx_rows.reshape(n_exp, b * cap), xn, w_gate, w_up, w_down, l, gate_rows, cap)
        slott = jnp.transpose(slot_rows.reshape(n_exp, b, s), (1, 2, 0))
        final = l == depth - 1
        x3 = _combine(x1.reshape(b, s, d), slott, y.reshape(n_exp, b, cap, d),
                      norm_final.reshape(1, d), cap, final)
        x2 = x3.reshape(b * s, d)
        out = x3
    return out
```

```python
import functools

import jax
import jax.numpy as jnp
from jax import lax
from jax.experimental import pallas as pl
from jax.experimental.pallas import tpu as pltpu

F32 = jnp.float32
BF16 = jnp.bfloat16

HEAD_DIM = 64
N_MEM_HEADS = 4
DILATED_PATTERNS = ((128, 1), (512, 4), (2048, 16))
HALF_WINDOW = 64
N_EXPERTS = 16
EC_CAPACITY = 2
RMS_EPS = 1e-6
NEG_INF = -1e30

LANES = 128
ROW_TILE = 512
ATT_TILE = 128
V7X_VMEM_LIMIT_BYTES = 56 * 1024 * 1024


def _params(*semantics):
    return pltpu.CompilerParams(dimension_semantics=semantics,
                                vmem_limit_bytes=V7X_VMEM_LIMIT_BYTES)


def _rms(x, g):
    ms = jnp.mean(x * x, axis=-1, keepdims=True)
    return x * lax.rsqrt(ms + RMS_EPS) * g


def _nt_dot(a, b, **kw):
    return lax.dot_general(a, b, (((1,), (1,)), ((), ())), preferred_element_type=F32, **kw)


def _inproj_kernel(x_ref, g_ref, w_ref, hc_ref, hq_ref):
    xn = _rms(x_ref[...], g_ref[...]).astype(BF16)
    n_conv = hc_ref.shape[1]
    chunk = 2 * LANES
    for c in range(0, w_ref.shape[1], chunk):
        h = jnp.dot(xn, w_ref[:, c:c + chunk], preferred_element_type=F32)
        if c + chunk <= n_conv:
            hc_ref[:, c:c + chunk] = h
        elif c >= n_conv:
            hq_ref[:, c - n_conv:c - n_conv + chunk] = h.astype(BF16)
        else:
            k = n_conv - c
            hc_ref[:, c:n_conv] = h[:, :k]
            hq_ref[:, 0:chunk - k] = h[:, k:].astype(BF16)


def _inproj(x2, g, w, n_conv):
    n, d = x2.shape
    n_all = w.shape[1]
    return pl.pallas_call(
        _inproj_kernel,
        grid=(n // ROW_TILE,),
        in_specs=[pl.BlockSpec((ROW_TILE, d), lambda i: (i, 0)),
                  pl.BlockSpec((1, d), lambda i: (0, 0)),
                  pl.BlockSpec((d, n_all), lambda i: (0, 0))],
        out_specs=[pl.BlockSpec((ROW_TILE, n_conv), lambda i: (i, 0)),
                   pl.BlockSpec((ROW_TILE, n_all - n_conv), lambda i: (i, 0))],
        out_shape=[jax.ShapeDtypeStruct((n, n_conv), F32),
                   jax.ShapeDtypeStruct((n, n_all - n_conv), BF16)],
        compiler_params=_params("arbitrary"),
        name="inproj",
    )(x2, g, w)


def _conv_kernel(xc_ref, bg_ref, cg_ref, w_ref, o_ref):
    u = cg_ref[0] * xc_ref[0]
    s = u.shape[0]
    row = lax.broadcasted_iota(jnp.int32, u.shape, 0)
    prev = jnp.where(row == 0, 0.0, pltpu.roll(u, 1, 0))
    nxt = jnp.where(row == s - 1, 0.0, pltpu.roll(u, s - 1, 0))
    w = w_ref[...]
    conv = w[0:1] * prev + w[1:2] * u + w[2:3] * nxt
    o_ref[0] = (bg_ref[0] * conv).astype(BF16)


def _conv(hc3, conv_w, conv_ch):
    b, s, _ = hc3.shape
    nblk = conv_ch // LANES
    blk = (1, s, LANES)
    return pl.pallas_call(
        _conv_kernel,
        grid=(b, nblk),
        in_specs=[pl.BlockSpec(blk, lambda i, j: (i, 0, j)),
                  pl.BlockSpec(blk, lambda i, j: (i, 0, nblk + j)),
                  pl.BlockSpec(blk, lambda i, j: (i, 0, 2 * nblk + j)),
                  pl.BlockSpec((conv_w.shape[0], LANES), lambda i, j: (0, j))],
        out_specs=pl.BlockSpec(blk, lambda i, j: (i, 0, j)),
        out_shape=jax.ShapeDtypeStruct((b, s, conv_ch), BF16),
        compiler_params=_params("arbitrary", "arbitrary"),
        name="shortconv",
    )(hc3, hc3, hc3, conv_w)


DIL_TILE_UNROLL = 16


def _dil_cases(seq, dil):
    tps = (seq // dil) // ATT_TILE
    return tps, (("single",) if tps == 1 else ("first", "interior", "last"))


def _dil_kernel(slopes_ref, q_ref, k_ref, v_ref, o_ref,
                natq, natk, natv, perq, perk, perv, bias,
                num1, mx1, sum1, num2, mx2, sum2, num3, mx3, sum3, onat, *operands):
    seq = q_ref.shape[1]
    pad = HALF_WINDOW
    tq = ATT_TILE
    step = DILATED_PATTERNS[1][1]
    seg2 = seq // step
    seg3 = seg2 // step
    log_seg2 = seg2.bit_length() - 1
    log_step = step.bit_length() - 1
    pair = pl.program_id(0)
    lane = lax.broadcasted_iota(jnp.int32, (1, LANES), 1)
    head0 = lane < HEAD_DIM
    m0 = head0.astype(F32)
    m1 = 1.0 - m0

    @pl.when(pl.program_id(1) == 0)
    def _():
        for pi in range(len(DILATED_PATTERNS)):
            _, _, kp, vp = operands[4 * pi:4 * pi + 4]
            kp[0:pad, :] = jnp.zeros((pad, LANES), BF16)
            kp[pad + seq:pad + seq + pad, :] = jnp.zeros((pad, LANES), BF16)
            vp[0:pad, :] = jnp.zeros((pad, 2 * LANES), BF16)
            vp[pad + seq:pad + seq + pad, :] = jnp.zeros((pad, 2 * LANES), BF16)
            vp[pad:pad + seq, LANES:2 * LANES] = jnp.ones((seq, LANES), BF16)
        row = lax.broadcasted_iota(jnp.int32, (2 * tq, 2 * tq), 0)
        col = lax.broadcasted_iota(jnp.int32, (2 * tq, 2 * tq), 1)
        kc = col - pad
        arel = jnp.abs(kc - (row & (tq - 1)))
        band = arel <= HALF_WINDOW
        slope_rows = jnp.where(row < tq, slopes_ref[2 * pair], slopes_ref[2 * pair + 1])
        idx = 0
        for _, dil in DILATED_PATTERNS:
            dist_bias = -slope_rows * (arel * dil).astype(F32)
            for case in _dil_cases(seq, dil)[1]:
                valid = band
                if case in ("first", "single"):
                    valid = valid & (kc >= 0)
                if case in ("last", "single"):
                    valid = valid & (kc < tq)
                bias[idx] = jnp.where(valid, dist_bias, NEG_INF)
                idx += 1

    natq[...] = q_ref[0].astype(F32) * (HEAD_DIM ** -0.5)
    natk[...] = k_ref[0].astype(F32)
    natv[...] = v_ref[0].astype(F32)

    def set_operands(pi, rows, qv, kv, vv):
        qp0, qp1, kp, vp = operands[4 * pi:4 * pi + 4]
        lo, hi = rows
        qp0[lo:hi, :] = (qv * m0).astype(BF16)
        qp1[lo:hi, :] = (qv * m1).astype(BF16)
        kp[pad + lo:pad + hi, :] = kv.astype(BF16)
        vp[pad + lo:pad + hi, 0:LANES] = vv.astype(BF16)

    def run_tiles(pi, case0, tps, dests):
        num_ref, mx_ref, sum_ref = dests
        qp0, qp1, kp, vp = operands[4 * pi:4 * pi + 4]

        def tile(m, carry):
            r0 = pl.multiple_of(m * tq, tq)
            qc = jnp.concatenate([qp0[pl.ds(r0, tq), :], qp1[pl.ds(r0, tq), :]], axis=0)
            kt = kp[pl.ds(r0, 2 * tq), :]
            vt = vp[pl.ds(r0, 2 * tq), :]
            if tps == 1:
                case = case0
                dst = pl.ds((m >> log_step) * seg2 + (m & (step - 1)), tq, stride=step)
            else:
                pos = m & (tps - 1)
                case = case0 + jnp.where(pos == 0, 0, jnp.where(pos == tps - 1, 2, 1))
                dst = pl.ds(r0, tq)
            s = _nt_dot(qc, kt) + bias[case]
            mx = jnp.max(s, axis=-1, keepdims=True)
            pe = jnp.exp(s - mx)
            pv = jnp.dot(pe.astype(BF16), vt, preferred_element_type=F32)
            num_ref[dst, :] = jnp.where(head0, pv[0:tq, 0:LANES], pv[tq:2 * tq, 0:LANES])
            sum_ref[dst, :] = jnp.where(head0, pv[0:tq, LANES:2 * LANES],
                                        pv[tq:2 * tq, LANES:2 * LANES])
            mx_ref[dst, :] = jnp.where(head0, mx[0:tq], mx[tq:2 * tq])
            return carry

        lax.fori_loop(0, seq // tq, tile, 0, unroll=DIL_TILE_UNROLL)

    case0 = 0
    set_operands(0, (0, seq), natq[...], natk[...], natv[...])
    tps1, cases1 = _dil_cases(seq, DILATED_PATTERNS[0][1])
    run_tiles(0, case0, tps1, (num1, mx1, sum1))
    case0 += len(cases1)

    for b in range(step):
        rows = (b * seg2, (b + 1) * seg2)
        qv = natq[pl.ds(b, seg2, stride=step), :]
        kv = natk[pl.ds(b, seg2, stride=step), :]
        vv = natv[pl.ds(b, seg2, stride=step), :]
        perq[rows[0]:rows[1], :] = qv
        perk[rows[0]:rows[1], :] = kv
        perv[rows[0]:rows[1], :] = vv
        set_operands(1, rows, qv, kv, vv)
    tps2, cases2 = _dil_cases(seq, DILATED_PATTERNS[1][1])
    run_tiles(1, case0, tps2, (num2, mx2, sum2))
    case0 += len(cases2)

    for sgm in range(step * step):
        b, a = divmod(sgm, step)
        src = pl.ds(b * seg2 + a, seg3, stride=step)
        set_operands(2, (sgm * seg3, (sgm + 1) * seg3), perq[src, :], perk[src, :], perv[src, :])
    tps3, _ = _dil_cases(seq, DILATED_PATTERNS[2][1])
    assert tps3 == 1 and seg3 == tq
    run_tiles(2, case0, tps3, (num3, mx3, sum3))

    def merge(c, carry):
        p0 = pl.multiple_of(c * tq, tq)
        per_rows = pl.ds(p0, tq)
        nat_rows = pl.ds((p0 & (seg2 - 1)) * step + (p0 >> log_seg2), tq, stride=step)
        a1, a2, a3 = mx1[nat_rows, :], mx2[per_rows, :], mx3[per_rows, :]
        top = jnp.maximum(jnp.maximum(a1, a2), a3)
        e1, e2, e3 = jnp.exp(a1 - top), jnp.exp(a2 - top), jnp.exp(a3 - top)
        den = e1 * sum1[nat_rows, :] + e2 * sum2[per_rows, :] + e3 * sum3[per_rows, :]
        num = e1 * num1[nat_rows, :] + e2 * num2[per_rows, :] + e3 * num3[per_rows, :]
        onat[nat_rows, :] = num / den
        return carry

    lax.fori_loop(0, seq // tq, merge, 0, unroll=2)
    o_ref[0] = onat[...].astype(BF16)


def _dilated(hq3, slopes, dil_width):
    b, s, _ = hq3.shape
    npair = dil_width // LANES
    blk = (1, s, LANES)
    dils = [d for _, d in DILATED_PATTERNS]
    assert dils[0] == 1 and dils[2] == dils[1] * dils[1]
    n_cases = sum(len(_dil_cases(s, d)[1]) for d in dils)
    pad = HALF_WINDOW
    scratch = [pltpu.VMEM((s, LANES), F32)] * 6
    scratch += [pltpu.VMEM((n_cases, 2 * ATT_TILE, 2 * ATT_TILE), F32)]
    scratch += [pltpu.VMEM((s, LANES), F32)] * 10
    scratch += [pltpu.VMEM((s, LANES), BF16), pltpu.VMEM((s, LANES), BF16),
                pltpu.VMEM((s + 2 * pad, LANES), BF16),
                pltpu.VMEM((s + 2 * pad, 2 * LANES), BF16)] * len(dils)
    return pl.pallas_call(
        _dil_kernel,
        grid_spec=pltpu.PrefetchScalarGridSpec(
            num_scalar_prefetch=1,
            grid=(npair, b),
            in_specs=[pl.BlockSpec(blk, lambda j, i, sl: (i, 0, j)),
                      pl.BlockSpec(blk, lambda j, i, sl: (i, 0, npair + j)),
                      pl.BlockSpec(blk, lambda j, i, sl: (i, 0, 2 * npair + j))],
            out_specs=pl.BlockSpec(blk, lambda j, i, sl: (i, 0, j)),
            scratch_shapes=scratch),
        out_shape=jax.ShapeDtypeStruct((b, s, dil_width), BF16),
        compiler_params=_params("arbitrary", "arbitrary"),
        name="dilated_attn",
    )(slopes, hq3, hq3, hq3)


def _memkv_kernel(m_ref, g_ref, w_ref, o_ref):
    mn = _rms(m_ref[...], g_ref[...]).astype(BF16)
    o_ref[...] = jnp.dot(mn, w_ref[...], preferred_element_type=F32).astype(BF16)


def _memkv(mem2, g, w):
    n, d = mem2.shape
    nw = w.shape[1]
    return pl.pallas_call(
        _memkv_kernel,
        grid=(n // ROW_TILE,),
        in_specs=[pl.BlockSpec((ROW_TILE, d), lambda i: (i, 0)),
                  pl.BlockSpec((1, d), lambda i: (0, 0)),
                  pl.BlockSpec((d, nw), lambda i: (0, 0))],
        out_specs=pl.BlockSpec((ROW_TILE, nw), lambda i: (i, 0)),
        out_shape=jax.ShapeDtypeStruct((n, nw), BF16),
        compiler_params=_params("arbitrary"),
        name="mem_kv",
    )(mem2, g, w)


def _memattn_kernel(q_ref, km_ref, vm_ref, o_ref, qp0, qp1, vext):
    seq = q_ref.shape[1]
    tq = 2 * ATT_TILE
    lane = lax.broadcasted_iota(jnp.int32, (1, LANES), 1)
    head0 = lane < HEAD_DIM
    m0 = head0.astype(F32)
    q = q_ref[0].astype(F32) * (HEAD_DIM ** -0.5)
    qp0[...] = (q * m0).astype(BF16)
    qp1[...] = (q * (1.0 - m0)).astype(BF16)
    vext[:, 0:LANES] = vm_ref[0]
    vext[:, LANES:2 * LANES] = jnp.ones((vext.shape[0], LANES), BF16)
    km = km_ref[0]

    def tile(m, carry):
        rows = pl.ds(pl.multiple_of(m * tq, tq), tq)
        qc = jnp.concatenate([qp0[rows, :], qp1[rows, :]], axis=0)
        s = _nt_dot(qc, km)
        pe = jnp.exp(s - jnp.max(s, axis=-1, keepdims=True))
        pv = jnp.dot(pe.astype(BF16), vext[...], preferred_element_type=F32)
        o = pv[:, 0:LANES] / pv[:, LANES:2 * LANES]
        o_ref[0, rows, :] = jnp.where(head0, o[0:tq], o[tq:2 * tq]).astype(BF16)
        return carry

    lax.fori_loop(0, seq // tq, tile, 0, unroll=4)


def _memattn(hq3, kvm3, layer, qm_block0, mem_width):
    b, s, _ = hq3.shape
    m = kvm3.shape[1]
    npair = mem_width // LANES
    kv_blocks = 2 * npair
    return pl.pallas_call(
        _memattn_kernel,
        grid=(b, npair),
        in_specs=[pl.BlockSpec((1, s, LANES), lambda i, j: (i, 0, qm_block0 + j)),
                  pl.BlockSpec((1, m, LANES), lambda i, j: (i, 0, layer * kv_blocks + j)),
                  pl.BlockSpec((1, m, LANES), lambda i, j: (i, 0, layer * kv_blocks + npair + j))],
        out_specs=pl.BlockSpec((1, s, LANES), lambda i, j: (i, 0, j)),
        out_shape=jax.ShapeDtypeStruct((b, s, mem_width), BF16),
        scratch_shapes=[pltpu.VMEM((s, LANES), BF16), pltpu.VMEM((s, LANES), BF16),
                        pltpu.VMEM((m, 2 * LANES), BF16)],
        compiler_params=_params("arbitrary", "arbitrary"),
        name="mem_attn",
    )(hq3, kvm3, kvm3)


OUT_SUB_ROWS = 256


def _outproj_kernel(x_ref, c_ref, d_ref, m_ref, w_ref, g_ref, wr_ref,
                    x1_ref, xn_ref, aff_ref, *, n_exp):
    valid = lax.broadcasted_iota(jnp.int32, (1, LANES), 1) < n_exp
    for r0 in range(0, x_ref.shape[0], OUT_SUB_ROWS):
        rs = slice(r0, r0 + OUT_SUB_ROWS)
        cat = jnp.concatenate([c_ref[rs, :], d_ref[rs, :], m_ref[rs, :]], axis=1)
        x1 = x_ref[rs, :] + jnp.dot(cat, w_ref[...], preferred_element_type=F32)
        x1_ref[rs, :] = x1
        xn = _rms(x1, g_ref[...])
        xn_ref[rs, :] = xn
        xh = xn.astype(BF16)
        xl = (xn - xh.astype(F32)).astype(BF16)
        both = jnp.dot(xh, wr_ref[...], preferred_element_type=F32)
        logits = (both[:, 0:LANES] + both[:, LANES:2 * LANES]
                  + jnp.dot(xl, wr_ref[:, 0:LANES], preferred_element_type=F32))
        logits = jnp.where(valid, logits, NEG_INF)
        ex = jnp.exp(logits - jnp.max(logits, axis=1, keepdims=True))
        aff_ref[rs, :] = ex / jnp.sum(ex, axis=1, keepdims=True)


def _outproj(x2, conv_o, dil_o, mem_o, w, g, wr_pieces, n_exp):
    n, d = x2.shape
    row = lambda i: (i, 0)
    fixed = lambda i: (0, 0)
    return pl.pallas_call(
        functools.partial(_outproj_kernel, n_exp=n_exp),
        grid=(n // ROW_TILE,),
        in_specs=[pl.BlockSpec((ROW_TILE, d), row),
                  pl.BlockSpec((ROW_TILE, conv_o.shape[1]), row),
                  pl.BlockSpec((ROW_TILE, dil_o.shape[1]), row),
                  pl.BlockSpec((ROW_TILE, mem_o.shape[1]), row),
                  pl.BlockSpec(w.shape, fixed),
                  pl.BlockSpec((1, d), fixed),
                  pl.BlockSpec((d, 2 * LANES), fixed)],
        out_specs=[pl.BlockSpec((ROW_TILE, d), row),
                   pl.BlockSpec((ROW_TILE, d), row),
                   pl.BlockSpec((ROW_TILE, LANES), row)],
        out_shape=[jax.ShapeDtypeStruct((n, d), F32),
                   jax.ShapeDtypeStruct((n, d), F32),
                   jax.ShapeDtypeStruct((n, LANES), F32)],
        compiler_params=_params("arbitrary"),
        name="outproj_router",
    )(x2, conv_o, dil_o, mem_o, w, g, wr_pieces)


TOK_DIGIT_BITS = 6
PICK_GATE_LANE0 = 2


def _excl_cumsum_lanes(mask_bf16):
    rows, s = mask_bf16.shape
    chunk = 2 * LANES
    acc = jnp.zeros((rows, s), F32)
    for c in range(0, s, chunk):
        src = lax.broadcasted_iota(jnp.int32, (chunk, s), 0) + c
        dst = lax.broadcasted_iota(jnp.int32, (chunk, s), 1)
        tri = jnp.where(src < dst, 1.0, 0.0).astype(BF16)
        acc = acc + jnp.dot(mask_bf16[:, c:c + chunk], tri, preferred_element_type=F32)
    return acc


def _topk_kernel(aff_ref, afft_ref, slot_ref, pick_ref, rhs, *, cap, n_batch, n_exp):
    aff = aff_ref[...]
    rows, seq = aff.shape
    bits = jnp.zeros((rows, 1), jnp.int32)
    for bit in range(30, -1, -1):
        cand = bits | (1 << bit)
        cnt = jnp.sum(jnp.where(aff >= pltpu.bitcast(cand, F32), 1.0, 0.0), axis=1, keepdims=True)
        bits = jnp.where(cnt >= cap, cand, bits)
    thr = pltpu.bitcast(bits, F32)
    gt = aff > thr
    eq = aff == thr
    need = cap - jnp.sum(jnp.where(gt, 1.0, 0.0), axis=1, keepdims=True)
    rank_eq = _excl_cumsum_lanes(jnp.where(eq, 1.0, 0.0).astype(BF16))
    sel = jnp.where(gt, 1.0, jnp.where(eq & (rank_eq < need), 1.0, 0.0))
    rank = _excl_cumsum_lanes(sel.astype(BF16))
    slot_ref[...] = jnp.where(sel > 0.5, rank, -1.0).astype(jnp.int32)

    lane = lax.broadcasted_iota(jnp.int32, (1, LANES), 1)
    tok = lax.broadcasted_iota(jnp.int32, (seq, LANES), 0)
    tok_part = jnp.where(lane == 0, tok >> TOK_DIGIT_BITS,
                         jnp.where(lane == 1, tok & ((1 << TOK_DIGIT_BITS) - 1), 0)).astype(F32)
    src = lax.broadcasted_iota(jnp.int32, (LANES, LANES), 0)
    dst = lax.broadcasted_iota(jnp.int32, (LANES, LANES), 1)

    def place(piece):
        return jnp.where((src < n_exp) & (dst == src + PICK_GATE_LANE0 + piece * n_exp),
                         1.0, 0.0).astype(BF16)

    for bi in range(n_batch):
        a = afft_ref[bi * seq:(bi + 1) * seq, :]
        a_hi = a.astype(BF16)
        rem = a - a_hi.astype(F32)
        a_mid = rem.astype(BF16)
        a_lo = (rem - a_mid.astype(F32)).astype(BF16)
        rhs[bi] = (tok_part
                   + jnp.dot(a_hi, place(0), preferred_element_type=F32)
                   + jnp.dot(a_mid, place(1), preferred_element_type=F32)
                   + jnp.dot(a_lo, place(2), preferred_element_type=F32)).astype(BF16)

    sidx = lax.broadcasted_iota(jnp.int32, (cap, seq), 0)

    def invert(r, carry):
        onehot = jnp.where(slot_ref[pl.ds(r, 1), :] == sidx, 1.0, 0.0).astype(BF16)
        pick_ref[pl.ds(pl.multiple_of(r * cap, cap), cap), :] = jnp.dot(
            onehot, rhs[lax.rem(r, n_batch)], preferred_element_type=F32)
        return carry

    lax.fori_loop(0, rows, invert, 0, unroll=2)


def _topk(aff_rows, aff_tok, cap, n_batch, n_exp):
    rows, seq = aff_rows.shape
    return pl.pallas_call(
        functools.partial(_topk_kernel, cap=cap, n_batch=n_batch, n_exp=n_exp),
        out_shape=[jax.ShapeDtypeStruct(aff_rows.shape, jnp.int32),
                   jax.ShapeDtypeStruct((rows * cap, LANES), F32)],
        scratch_shapes=[pltpu.VMEM((n_batch, seq, LANES), BF16)],
        compiler_params=pltpu.CompilerParams(vmem_limit_bytes=V7X_VMEM_LIMIT_BYTES),
        name="expert_topk",
    )(aff_rows, aff_tok)


FF_CHUNK = 512
EXPERT_ROWS = 256


def _expert_kernel(idx_ref, xn_hbm, wg_ref, wu_ref, wd_ref, pick_ref, y_ref,
                   xbuf, sem, acc, wgb, wub, wdb, *, cap, n_f):
    e = pl.program_id(0)
    f = pl.program_id(1)
    n_exp = pl.num_programs(0)
    rows = acc.shape[0]
    slot = lax.rem(e, 2)
    nxt = 1 - slot
    e_nxt = jnp.minimum(e + 1, n_exp - 1)

    def row_copy(src_row, buf, dst_row):
        return pltpu.make_async_copy(xn_hbm.at[pl.ds(src_row, 1)],
                                     xbuf.at[buf, pl.ds(dst_row, 1)], sem.at[buf])

    def wait_rows(buf):
        pltpu.make_async_copy(xn_hbm.at[pl.ds(0, rows)], xbuf.at[buf], sem.at[buf]).wait()

    @pl.when((e == 0) & (f == 0))
    def _():
        def first(j, carry):
            row_copy(idx_ref[0, j], 0, j).start()
            return carry
        lax.fori_loop(0, rows, first, 0, unroll=8)

    @pl.when(f == 0)
    def _():
        wait_rows(slot)
        acc[...] = jnp.zeros(acc.shape, F32)

    wgb[...] = wg_ref[0, 0].astype(BF16)
    wub[...] = wu_ref[0, 0].astype(BF16)
    wdb[...] = wd_ref[0, 0].astype(BF16)

    per_step = rows // n_f
    per_chunk = per_step // (rows // EXPERT_ROWS)
    base = pl.multiple_of(f * per_step, per_step)
    for ci, r0 in enumerate(range(0, rows, EXPERT_ROWS)):
        for k in range(ci * per_chunk, (ci + 1) * per_chunk):
            row_copy(idx_ref[e_nxt, base + k], nxt, base + k).start()
        rsl = slice(r0, r0 + EXPERT_ROWS)
        xm = xbuf[slot, rsl, :].astype(BF16)
        gate = jnp.dot(xm, wgb[...], preferred_element_type=F32)
        up = jnp.dot(xm, wub[...], preferred_element_type=F32)
        hid = (gate * (1.0 / (1.0 + jnp.exp(-gate))) * up).astype(BF16)
        acc[rsl, :] += jnp.dot(hid, wdb[...], preferred_element_type=F32)

    @pl.when(f == n_f - 1)
    def _():
        lane = lax.broadcasted_iota(jnp.int32, (1, LANES), 1) - (PICK_GATE_LANE0 + e)
        mine = (lane == 0) | (lane == n_exp) | (lane == 2 * n_exp)
        for r0 in range(0, rows, EXPERT_ROWS):
            rsl = slice(r0, r0 + EXPERT_ROWS)
            gate_col = jnp.sum(jnp.where(mine, pick_ref[rsl, :], 0.0), axis=1, keepdims=True)
            y_ref[0, rsl, :] = (acc[rsl, :] * gate_col).astype(BF16)

    @pl.when((e == n_exp - 1) & (f == n_f - 1))
    def _():
        wait_rows(nxt)


def _experts(idx2, xn2, w_gate, w_up, w_down, layer, picked, cap):
    n_exp, rows = idx2.shape
    d = xn2.shape[1]
    d_ff = w_gate.shape[3]
    n_f = d_ff // FF_CHUNK
    assert rows % (n_f * (rows // EXPERT_ROWS)) == 0
    return pl.pallas_call(
        functools.partial(_expert_kernel, cap=cap, n_f=n_f),
        grid_spec=pltpu.PrefetchScalarGridSpec(
            num_scalar_prefetch=1,
            grid=(n_exp, n_f),
            in_specs=[pl.BlockSpec(memory_space=pl.ANY),
                      pl.BlockSpec((1, 1, d, FF_CHUNK), lambda e, f, ix: (layer, e, 0, f)),
                      pl.BlockSpec((1, 1, d, FF_CHUNK), lambda e, f, ix: (layer, e, 0, f)),
                      pl.BlockSpec((1, 1, FF_CHUNK, d), lambda e, f, ix: (layer, e, f, 0)),
                      pl.BlockSpec((rows, LANES), lambda e, f, ix: (e, 0))],
            out_specs=pl.BlockSpec((1, rows, d), lambda e, f, ix: (e, 0, 0)),
            scratch_shapes=[pltpu.VMEM((2, rows, d), F32),
                            pltpu.SemaphoreType.DMA((2,)),
                            pltpu.VMEM((rows, d), F32),
                            pltpu.VMEM((d, FF_CHUNK), BF16),
                            pltpu.VMEM((d, FF_CHUNK), BF16),
                            pltpu.VMEM((FF_CHUNK, d), BF16)]),
        out_shape=jax.ShapeDtypeStruct((n_exp, rows, d), BF16),
        compiler_params=_params("arbitrary", "arbitrary"),
        name="expert_ffn",
    )(idx2, xn2, w_gate, w_up, w_down, picked)


COMBINE_GROUP = 4


def _combine_kernel(x_ref, slott_ref, y_ref, g_ref, o_ref, *, cap, final):
    slott = slott_ref[0]
    t, n_exp = slott.shape
    d = x_ref.shape[2]
    sidx = lax.broadcasted_iota(jnp.int32, (t, cap), 1)
    acc = x_ref[0]
    for e0 in range(0, n_exp, COMBINE_GROUP):
        parts = [jnp.where(slott[:, e:e + 1] == sidx, 1.0, 0.0).astype(BF16)
                 for e in range(e0, e0 + COMBINE_GROUP)]
        onehot_t = jnp.concatenate(parts, axis=1)
        yg = y_ref[e0:e0 + COMBINE_GROUP, 0].reshape(COMBINE_GROUP * cap, d)
        acc = acc + jnp.dot(onehot_t, yg, preferred_element_type=F32)
    if final:
        acc = _rms(acc, g_ref[...])
    o_ref[0] = acc


def _combine(x3, slott, y4, g, cap, final):
    b, s, d = x3.shape
    n_exp = y4.shape[0]
    return pl.pallas_call(
        functools.partial(_combine_kernel, cap=cap, final=final),
        grid=(b, s // ROW_TILE),
        in_specs=[pl.BlockSpec((1, ROW_TILE, d), lambda i, t: (i, t, 0)),
                  pl.BlockSpec((1, ROW_TILE, n_exp), lambda i, t: (i, t, 0)),
                  pl.BlockSpec((n_exp, 1, cap, d), lambda i, t: (0, i, 0, 0)),
                  pl.BlockSpec((1, d), lambda i, t: (0, 0))],
        out_specs=pl.BlockSpec((1, ROW_TILE, d), lambda i, t: (i, t, 0)),
        out_shape=jax.ShapeDtypeStruct((b, s, d), F32),
        compiler_params=_params("arbitrary", "arbitrary"),
        name="combine",
    )(x3, slott, y4, g)


def kernel(x, mem, mem_norm, norm_mix, w_in, conv_w, w_mem_kv, w_out, norm_ffn,
           w_router, w_gate, w_up, w_down, norm_final):
    b, s, d = x.shape
    depth = w_in.shape[0]
    n_exp = w_router.shape[2]
    conv_ch = conv_w.shape[2]
    mem_width = w_mem_kv.shape[2] // 2
    dil_width = (w_in.shape[2] - 3 * conv_ch - mem_width) // 3
    n_dil_heads = dil_width // HEAD_DIM
    cap = EC_CAPACITY * s // n_exp
    assert n_exp == N_EXPERTS and mem_width == N_MEM_HEADS * HEAD_DIM
    assert all(w // (2 * dl) == HALF_WINDOW for w, dl in DILATED_PATTERNS)

    slopes = jnp.asarray([2.0 ** (-8.0 * (h + 1) / n_dil_heads) for h in range(n_dil_heads)], F32)
    kv_all = jnp.transpose(w_mem_kv, (1, 0, 2)).reshape(d, depth * 2 * mem_width).astype(BF16)
    kvm3 = _memkv(mem.reshape(-1, d), mem_norm.reshape(1, d), kv_all).reshape(b, mem.shape[1], -1)

    x2 = x.reshape(b * s, d)
    out = None
    for l in range(depth):
        hc, hq = _inproj(x2, norm_mix[l].reshape(1, d), w_in[l].astype(BF16), 3 * conv_ch)
        hq3 = hq.reshape(b, s, -1)
        conv_o = _conv(hc.reshape(b, s, -1), conv_w[l], conv_ch)
        dil_o = _dilated(hq3, slopes, dil_width)
        mem_o = _memattn(hq3, kvm3, l, 3 * dil_width // LANES, mem_width)
        wr = jnp.pad(w_router[l], ((0, 0), (0, LANES - n_exp)))
        wr_hi = wr.astype(BF16)
        wr_pieces = jnp.concatenate([wr_hi, (wr - wr_hi.astype(F32)).astype(BF16)], axis=1)
        x1, xn, aff_tok = _outproj(x2, conv_o.reshape(b * s, -1), dil_o.reshape(b * s, -1),
                                   mem_o.reshape(b * s, -1), w_out[l].astype(BF16),
                                   norm_ffn[l].reshape(1, d), wr_pieces, n_exp)
        aff_rows = jnp.transpose(aff_tok[:, :n_exp]).reshape(n_exp * b, s)
        slot_rows, picked = _topk(aff_rows, aff_tok, cap, b, n_exp)
        token = (picked[:, 0] * (1 << TOK_DIGIT_BITS) + picked[:, 1]).astype(jnp.int32)
        idx = token.reshape(n_exp, b, cap) + (jnp.arange(b, dtype=jnp.int32) * s)[None, :, None]
        y = _experts(idx.reshape(n_exp, b * cap), xn, w_gate, w_up, w_down, l, picked, cap)
        slott = jnp.transpose(slot_rows.reshape(n_exp, b, s), (1, 2, 0))
        final = l == depth - 1
        x3 = _combine(x1.reshape(b, s, d), slott, y.reshape(n_exp, b, cap, d),
                      norm_final.reshape(1, d), cap, final)
        x2 = x3.reshape(b * s, d)
        out = x3
    return out
```

```python
import functools

import jax
import jax.numpy as jnp
from jax import lax
from jax.experimental import pallas as pl
from jax.experimental.pallas import tpu as pltpu

F32 = jnp.float32
BF16 = jnp.bfloat16

HEAD_DIM = 64
N_MEM_HEADS = 4
DILATED_PATTERNS = ((128, 1), (512, 4), (2048, 16))
HALF_WINDOW = 64
N_EXPERTS = 16
EC_CAPACITY = 2
RMS_EPS = 1e-6
NEG_INF = -1e30

LANES = 128
ROW_TILE = 512
ATT_TILE = 128
V7X_VMEM_LIMIT_BYTES = 56 * 1024 * 1024


def _params(*semantics):
    return pltpu.CompilerParams(dimension_semantics=semantics,
                                vmem_limit_bytes=V7X_VMEM_LIMIT_BYTES)


def _rms(x, g):
    ms = jnp.mean(x * x, axis=-1, keepdims=True)
    return x * lax.rsqrt(ms + RMS_EPS) * g


def _nt_dot(a, b, **kw):
    return lax.dot_general(a, b, (((1,), (1,)), ((), ())), preferred_element_type=F32, **kw)


def _inproj_kernel(x_ref, g_ref, w_ref, hc_ref, hq_ref):
    xn = _rms(x_ref[...], g_ref[...]).astype(BF16)
    n_conv = hc_ref.shape[1]
    chunk = 2 * LANES
    for c in range(0, w_ref.shape[1], chunk):
        h = jnp.dot(xn, w_ref[:, c:c + chunk], preferred_element_type=F32)
        if c + chunk <= n_conv:
            hc_ref[:, c:c + chunk] = h
        elif c >= n_conv:
            hq_ref[:, c - n_conv:c - n_conv + chunk] = h.astype(BF16)
        else:
            k = n_conv - c
            hc_ref[:, c:n_conv] = h[:, :k]
            hq_ref[:, 0:chunk - k] = h[:, k:].astype(BF16)


def _inproj(x2, g, w, n_conv):
    n, d = x2.shape
    n_all = w.shape[1]
    return pl.pallas_call(
        _inproj_kernel,
        grid=(n // ROW_TILE,),
        in_specs=[pl.BlockSpec((ROW_TILE, d), lambda i: (i, 0)),
                  pl.BlockSpec((1, d), lambda i: (0, 0)),
                  pl.BlockSpec((d, n_all), lambda i: (0, 0))],
        out_specs=[pl.BlockSpec((ROW_TILE, n_conv), lambda i: (i, 0)),
                   pl.BlockSpec((ROW_TILE, n_all - n_conv), lambda i: (i, 0))],
        out_shape=[jax.ShapeDtypeStruct((n, n_conv), F32),
                   jax.ShapeDtypeStruct((n, n_all - n_conv), BF16)],
        compiler_params=_params("arbitrary"),
        name="inproj",
    )(x2, g, w)


def _conv_kernel(xc_ref, bg_ref, cg_ref, w_ref, o_ref):
    u = cg_ref[0] * xc_ref[0]
    s = u.shape[0]
    row = lax.broadcasted_iota(jnp.int32, u.shape, 0)
    prev = jnp.where(row == 0, 0.0, pltpu.roll(u, 1, 0))
    nxt = jnp.where(row == s - 1, 0.0, pltpu.roll(u, s - 1, 0))
    w = w_ref[...]
    conv = w[0:1] * prev + w[1:2] * u + w[2:3] * nxt
    o_ref[0] = (bg_ref[0] * conv).astype(BF16)


def _conv(hc3, conv_w, conv_ch):
    b, s, _ = hc3.shape
    nblk = conv_ch // LANES
    blk = (1, s, LANES)
    return pl.pallas_call(
        _conv_kernel,
        grid=(b, nblk),
        in_specs=[pl.BlockSpec(blk, lambda i, j: (i, 0, j)),
                  pl.BlockSpec(blk, lambda i, j: (i, 0, nblk + j)),
                  pl.BlockSpec(blk, lambda i, j: (i, 0, 2 * nblk + j)),
                  pl.BlockSpec((conv_w.shape[0], LANES), lambda i, j: (0, j))],
        out_specs=pl.BlockSpec(blk, lambda i, j: (i, 0, j)),
        out_shape=jax.ShapeDtypeStruct((b, s, conv_ch), BF16),
        compiler_params=_params("arbitrary", "arbitrary"),
        name="shortconv",
    )(hc3, hc3, hc3, conv_w)


DIL_TILE_UNROLL = 16


def _dil_cases(seq, dil):
    tps = (seq // dil) // ATT_TILE
    return tps, (("single",) if tps == 1 else ("first", "interior", "last"))


def _dil_kernel(slopes_ref, q_ref, k_ref, v_ref, o_ref,
                natq, natk, natv, perq, perk, perv, bias,
                num1, mx1, sum1, num2, mx2, sum2, num3, mx3, sum3, onat, *operands):
    seq = q_ref.shape[1]
    pad = HALF_WINDOW
    tq = ATT_TILE
    step = DILATED_PATTERNS[1][1]
    seg2 = seq // step
    seg3 = seg2 // step
    log_seg2 = seg2.bit_length() - 1
    log_step = step.bit_length() - 1
    pair = pl.program_id(0)
    lane = lax.broadcasted_iota(jnp.int32, (1, LANES), 1)
    head0 = lane < HEAD_DIM
    m0 = head0.astype(F32)
    m1 = 1.0 - m0

    @pl.when(pl.program_id(1) == 0)
    def _():
        for pi in range(len(DILATED_PATTERNS)):
            _, _, kp, vp = operands[4 * pi:4 * pi + 4]
            kp[0:pad, :] = jnp.zeros((pad, LANES), BF16)
            kp[pad + seq:pad + seq + pad, :] = jnp.zeros((pad, LANES), BF16)
            vp[0:pad, :] = jnp.zeros((pad, 2 * LANES), BF16)
            vp[pad + seq:pad + seq + pad, :] = jnp.zeros((pad, 2 * LANES), BF16)
            vp[pad:pad + seq, LANES:2 * LANES] = jnp.ones((seq, LANES), BF16)
        row = lax.broadcasted_iota(jnp.int32, (2 * tq, 2 * tq), 0)
        col = lax.broadcasted_iota(jnp.int32, (2 * tq, 2 * tq), 1)
        kc = col - pad
        arel = jnp.abs(kc - (row & (tq - 1)))
        band = arel <= HALF_WINDOW
        slope_rows = jnp.where(row < tq, slopes_ref[2 * pair], slopes_ref[2 * pair + 1])
        idx = 0
        for _, dil in DILATED_PATTERNS:
            dist_bias = -slope_rows * (arel * dil).astype(F32)
            for case in _dil_cases(seq, dil)[1]:
                valid = band
                if case in ("first", "single"):
                    valid = valid & (kc >= 0)
                if case in ("last", "single"):
                    valid = valid & (kc < tq)
                bias[idx] = jnp.where(valid, dist_bias, NEG_INF)
                idx += 1

    natq[...] = q_ref[0].astype(F32) * (HEAD_DIM ** -0.5)
    natk[...] = k_ref[0].astype(F32)
    natv[...] = v_ref[0].astype(F32)

    def set_operands(pi, rows, qv, kv, vv):
        qp0, qp1, kp, vp = operands[4 * pi:4 * pi + 4]
        lo, hi = rows
        qp0[lo:hi, :] = (qv * m0).astype(BF16)
        qp1[lo:hi, :] = (qv * m1).astype(BF16)
        kp[pad + lo:pad + hi, :] = kv.astype(BF16)
        vp[pad + lo:pad + hi, 0:LANES] = vv.astype(BF16)

    def run_tiles(pi, case0, tps, dests):
        num_ref, mx_ref, sum_ref = dests
        qp0, qp1, kp, vp = operands[4 * pi:4 * pi + 4]

        def tile(m, carry):
            r0 = pl.multiple_of(m * tq, tq)
            qc = jnp.concatenate([qp0[pl.ds(r0, tq), :], qp1[pl.ds(r0, tq), :]], axis=0)
            kt = kp[pl.ds(r0, 2 * tq), :]
            vt = vp[pl.ds(r0, 2 * tq), :]
            if tps == 1:
                case = case0
                dst = pl.ds((m >> log_step) * seg2 + (m & (step - 1)), tq, stride=step)
            else:
                pos = m & (tps - 1)
                case = case0 + jnp.where(pos == 0, 0, jnp.where(pos == tps - 1, 2, 1))
                dst = pl.ds(r0, tq)
            s = _nt_dot(qc, kt) + bias[case]
            mx = jnp.max(s, axis=-1, keepdims=True)
            pe = jnp.exp(s - mx)
            pv = jnp.dot(pe.astype(BF16), vt, preferred_element_type=F32)
            num_ref[dst, :] = jnp.where(head0, pv[0:tq, 0:LANES], pv[tq:2 * tq, 0:LANES])
            sum_ref[dst, :] = jnp.where(head0, pv[0:tq, LANES:2 * LANES],
                                        pv[tq:2 * tq, LANES:2 * LANES])
            mx_ref[dst, :] = jnp.where(head0, mx[0:tq], mx[tq:2 * tq])
            return carry

        lax.fori_loop(0, seq // tq, tile, 0, unroll=DIL_TILE_UNROLL)

    case0 = 0
    set_operands(0, (0, seq), natq[...], natk[...], natv[...])
    tps1, cases1 = _dil_cases(seq, DILATED_PATTERNS[0][1])
    run_tiles(0, case0, tps1, (num1, mx1, sum1))
    case0 += len(cases1)

    for b in range(step):
        rows = (b * seg2, (b + 1) * seg2)
        qv = natq[pl.ds(b, seg2, stride=step), :]
        kv = natk[pl.ds(b, seg2, stride=step), :]
        vv = natv[pl.ds(b, seg2, stride=step), :]
        perq[rows[0]:rows[1], :] = qv
        perk[rows[0]:rows[1], :] = kv
        perv[rows[0]:rows[1], :] = vv
        set_operands(1, rows, qv, kv, vv)
    tps2, cases2 = _dil_cases(seq, DILATED_PATTERNS[1][1])
    run_tiles(1, case0, tps2, (num2, mx2, sum2))
    case0 += len(cases2)

    for sgm in range(step * step):
        b, a = divmod(sgm, step)
        src = pl.ds(b * seg2 + a, seg3, stride=step)
        set_operands(2, (sgm * seg3, (sgm + 1) * seg3), perq[src, :], perk[src, :], perv[src, :])
    tps3, _ = _dil_cases(seq, DILATED_PATTERNS[2][1])
    assert tps3 == 1 and seg3 == tq
    run_tiles(2, case0, tps3, (num3, mx3, sum3))

    def merge(c, carry):
        p0 = pl.multiple_of(c * tq, tq)
        per_rows = pl.ds(p0, tq)
        nat_rows = pl.ds((p0 & (seg2 - 1)) * step + (p0 >> log_seg2), tq, stride=step)
        a1, a2, a3 = mx1[nat_rows, :], mx2[per_rows, :], mx3[per_rows, :]
        top = jnp.maximum(jnp.maximum(a1, a2), a3)
        e1, e2, e3 = jnp.exp(a1 - top), jnp.exp(a2 - top), jnp.exp(a3 - top)
        den = e1 * sum1[nat_rows, :] + e2 * sum2[per_rows, :] + e3 * sum3[per_rows, :]
        num = e1 * num1[nat_rows, :] + e2 * num2[per_rows, :] + e3 * num3[per_rows, :]
        onat[nat_rows, :] = num / den
        return carry

    lax.fori_loop(0, seq // tq, merge, 0, unroll=2)
    o_ref[0] = onat[...].astype(BF16)


def _dilated(hq3, slopes, dil_width):
    b, s, _ = hq3.shape
    npair = dil_width // LANES
    blk = (1, s, LANES)
    dils = [d for _, d in DILATED_PATTERNS]
    assert dils[0] == 1 and dils[2] == dils[1] * dils[1]
    n_cases = sum(len(_dil_cases(s, d)[1]) for d in dils)
    pad = HALF_WINDOW
    scratch = [pltpu.VMEM((s, LANES), F32)] * 6
    scratch += [pltpu.VMEM((n_cases, 2 * ATT_TILE, 2 * ATT_TILE), F32)]
    scratch += [pltpu.VMEM((s, LANES), F32)] * 10
    scratch += [pltpu.VMEM((s, LANES), BF16), pltpu.VMEM((s, LANES), BF16),
                pltpu.VMEM((s + 2 * pad, LANES), BF16),
                pltpu.VMEM((s + 2 * pad, 2 * LANES), BF16)] * len(dils)
    return pl.pallas_call(
        _dil_kernel,
        grid_spec=pltpu.PrefetchScalarGridSpec(
            num_scalar_prefetch=1,
            grid=(npair, b),
            in_specs=[pl.BlockSpec(blk, lambda j, i, sl: (i, 0, j)),
                      pl.BlockSpec(blk, lambda j, i, sl: (i, 0, npair + j)),
                      pl.BlockSpec(blk, lambda j, i, sl: (i, 0, 2 * npair + j))],
            out_specs=pl.BlockSpec(blk, lambda j, i, sl: (i, 0, j)),
            scratch_shapes=scratch),
        out_shape=jax.ShapeDtypeStruct((b, s, dil_width), BF16),
        compiler_params=_params("arbitrary", "arbitrary"),
        name="dilated_attn",
    )(slopes, hq3, hq3, hq3)


def _memkv_kernel(m_ref, g_ref, w_ref, o_ref):
    mn = _rms(m_ref[...], g_ref[...]).astype(BF16)
    o_ref[...] = jnp.dot(mn, w_ref[...], preferred_element_type=F32).astype(BF16)


def _memkv(mem2, g, w):
    n, d = mem2.shape
    nw = w.shape[1]
    return pl.pallas_call(
        _memkv_kernel,
        grid=(n // ROW_TILE,),
        in_specs=[pl.BlockSpec((ROW_TILE, d), lambda i: (i, 0)),
                  pl.BlockSpec((1, d), lambda i: (0, 0)),
                  pl.BlockSpec((d, nw), lambda i: (0, 0))],
        out_specs=pl.BlockSpec((ROW_TILE, nw), lambda i: (i, 0)),
        out_shape=jax.ShapeDtypeStruct((n, nw), BF16),
        compiler_params=_params("arbitrary"),
        name="mem_kv",
    )(mem2, g, w)


def _memattn_kernel(q_ref, km_ref, vm_ref, o_ref, qp0, qp1, vext):
    seq = q_ref.shape[1]
    tq = 2 * ATT_TILE
    lane = lax.broadcasted_iota(jnp.int32, (1, LANES), 1)
    head0 = lane < HEAD_DIM
    m0 = head0.astype(F32)
    q = q_ref[0].astype(F32) * (HEAD_DIM ** -0.5)
    qp0[...] = (q * m0).astype(BF16)
    qp1[...] = (q * (1.0 - m0)).astype(BF16)
    vext[:, 0:LANES] = vm_ref[0]
    vext[:, LANES:2 * LANES] = jnp.ones((vext.shape[0], LANES), BF16)
    km = km_ref[0]

    def tile(m, carry):
        rows = pl.ds(pl.multiple_of(m * tq, tq), tq)
        qc = jnp.concatenate([qp0[rows, :], qp1[rows, :]], axis=0)
        s = _nt_dot(qc, km)
        pe = jnp.exp(s - jnp.max(s, axis=-1, keepdims=True))
        pv = jnp.dot(pe.astype(BF16), vext[...], preferred_element_type=F32)
        o = pv[:, 0:LANES] / pv[:, LANES:2 * LANES]
        o_ref[0, rows, :] = jnp.where(head0, o[0:tq], o[tq:2 * tq]).astype(BF16)
        return carry

    lax.fori_loop(0, seq // tq, tile, 0, unroll=4)


def _memattn(hq3, kvm3, layer, qm_block0, mem_width):
    b, s, _ = hq3.shape
    m = kvm3.shape[1]
    npair = mem_width // LANES
    kv_blocks = 2 * npair
    return pl.pallas_call(
        _memattn_kernel,
        grid=(b, npair),
        in_specs=[pl.BlockSpec((1, s, LANES), lambda i, j: (i, 0, qm_block0 + j)),
                  pl.BlockSpec((1, m, LANES), lambda i, j: (i, 0, layer * kv_blocks + j)),
                  pl.BlockSpec((1, m, LANES), lambda i, j: (i, 0, layer * kv_blocks + npair + j))],
        out_specs=pl.BlockSpec((1, s, LANES), lambda i, j: (i, 0, j)),
        out_shape=jax.ShapeDtypeStruct((b, s, mem_width), BF16),
        scratch_shapes=[pltpu.VMEM((s, LANES), BF16), pltpu.VMEM((s, LANES), BF16),
                        pltpu.VMEM((m, 2 * LANES), BF16)],
        compiler_params=_params("arbitrary", "arbitrary"),
        name="mem_attn",
    )(hq3, kvm3, kvm3)


OUT_SUB_ROWS = 256
TOKEN_TILE_ROWS = 8


def _outproj_kernel(x_ref, c_ref, d_ref, m_ref, w_ref, g_ref, wr_ref,
                    x1_ref, xn_ref, aff_ref, *, n_exp):
    valid = lax.broadcasted_iota(jnp.int32, (1, LANES), 1) < n_exp
    for r0 in range(0, x_ref.shape[0], OUT_SUB_ROWS):
        rs = slice(r0, r0 + OUT_SUB_ROWS)
        cat = jnp.concatenate([c_ref[rs, :], d_ref[rs, :], m_ref[rs, :]], axis=1)
        x1 = x_ref[rs, :] + jnp.dot(cat, w_ref[...], preferred_element_type=F32)
        x1_ref[rs, :] = x1
        xn = _rms(x1, g_ref[...])
        for j in range(xn.shape[1] // LANES):
            xn_ref[pl.ds(r0 * TOKEN_TILE_ROWS + j, OUT_SUB_ROWS, stride=TOKEN_TILE_ROWS), :] = (
                xn[:, j * LANES:(j + 1) * LANES])
        xh = xn.astype(BF16)
        xl = (xn - xh.astype(F32)).astype(BF16)
        both = jnp.dot(xh, wr_ref[...], preferred_element_type=F32)
        logits = (both[:, 0:LANES] + both[:, LANES:2 * LANES]
                  + jnp.dot(xl, wr_ref[:, 0:LANES], preferred_element_type=F32))
        logits = jnp.where(valid, logits, NEG_INF)
        ex = jnp.exp(logits - jnp.max(logits, axis=1, keepdims=True))
        aff_ref[rs, :] = ex / jnp.sum(ex, axis=1, keepdims=True)


def _outproj(x2, conv_o, dil_o, mem_o, w, g, wr_pieces, n_exp):
    n, d = x2.shape
    assert d == TOKEN_TILE_ROWS * LANES
    row = lambda i: (i, 0)
    fixed = lambda i: (0, 0)
    return pl.pallas_call(
        functools.partial(_outproj_kernel, n_exp=n_exp),
        grid=(n // ROW_TILE,),
        in_specs=[pl.BlockSpec((ROW_TILE, d), row),
                  pl.BlockSpec((ROW_TILE, conv_o.shape[1]), row),
                  pl.BlockSpec((ROW_TILE, dil_o.shape[1]), row),
                  pl.BlockSpec((ROW_TILE, mem_o.shape[1]), row),
                  pl.BlockSpec(w.shape, fixed),
                  pl.BlockSpec((1, d), fixed),
                  pl.BlockSpec((d, 2 * LANES), fixed)],
        out_specs=[pl.BlockSpec((ROW_TILE, d), row),
                   pl.BlockSpec((ROW_TILE * TOKEN_TILE_ROWS, LANES), row),
                   pl.BlockSpec((ROW_TILE, LANES), row)],
        out_shape=[jax.ShapeDtypeStruct((n, d), F32),
                   jax.ShapeDtypeStruct((n * TOKEN_TILE_ROWS, LANES), F32),
                   jax.ShapeDtypeStruct((n, LANES), F32)],
        compiler_params=_params("arbitrary"),
        name="outproj_router",
    )(x2, conv_o, dil_o, mem_o, w, g, wr_pieces)


TOK_DIGIT_BITS = 6
PICK_GATE_LANE0 = 2


def _excl_cumsum_lanes(mask_bf16):
    rows, s = mask_bf16.shape
    chunk = 2 * LANES
    acc = jnp.zeros((rows, s), F32)
    for c in range(0, s, chunk):
        src = lax.broadcasted_iota(jnp.int32, (chunk, s), 0) + c
        dst = lax.broadcasted_iota(jnp.int32, (chunk, s), 1)
        tri = jnp.where(src < dst, 1.0, 0.0).astype(BF16)
        acc = acc + jnp.dot(mask_bf16[:, c:c + chunk], tri, preferred_element_type=F32)
    return acc


def _topk_kernel(aff_ref, afft_ref, slot_ref, pick_ref, rhs, *, cap, n_batch, n_exp):
    aff = aff_ref[...]
    rows, seq = aff.shape
    bits = jnp.zeros((rows, 1), jnp.int32)
    for bit in range(30, -1, -1):
        cand = bits | (1 << bit)
        cnt = jnp.sum(jnp.where(aff >= pltpu.bitcast(cand, F32), 1.0, 0.0), axis=1, keepdims=True)
        bits = jnp.where(cnt >= cap, cand, bits)
    thr = pltpu.bitcast(bits, F32)
    gt = aff > thr
    eq = aff == thr
    need = cap - jnp.sum(jnp.where(gt, 1.0, 0.0), axis=1, keepdims=True)
    rank_eq = _excl_cumsum_lanes(jnp.where(eq, 1.0, 0.0).astype(BF16))
    sel = jnp.where(gt, 1.0, jnp.where(eq & (rank_eq < need), 1.0, 0.0))
    rank = _excl_cumsum_lanes(sel.astype(BF16))
    slot_ref[...] = jnp.where(sel > 0.5, rank, -1.0).astype(jnp.int32)

    lane = lax.broadcasted_iota(jnp.int32, (1, LANES), 1)
    tok = lax.broadcasted_iota(jnp.int32, (seq, LANES), 0)
    tok_part = jnp.where(lane == 0, tok >> TOK_DIGIT_BITS,
                         jnp.where(lane == 1, tok & ((1 << TOK_DIGIT_BITS) - 1), 0)).astype(F32)
    src = lax.broadcasted_iota(jnp.int32, (LANES, LANES), 0)
    dst = lax.broadcasted_iota(jnp.int32, (LANES, LANES), 1)

    def place(piece):
        return jnp.where((src < n_exp) & (dst == src + PICK_GATE_LANE0 + piece * n_exp),
                         1.0, 0.0).astype(BF16)

    for bi in range(n_batch):
        a = afft_ref[bi * seq:(bi + 1) * seq, :]
        a_hi = a.astype(BF16)
        rem = a - a_hi.astype(F32)
        a_mid = rem.astype(BF16)
        a_lo = (rem - a_mid.astype(F32)).astype(BF16)
        rhs[bi] = (tok_part
                   + jnp.dot(a_hi, place(0), preferred_element_type=F32)
                   + jnp.dot(a_mid, place(1), preferred_element_type=F32)
                   + jnp.dot(a_lo, place(2), preferred_element_type=F32)).astype(BF16)

    sidx = lax.broadcasted_iota(jnp.int32, (cap, seq), 0)

    def invert(r, carry):
        onehot = jnp.where(slot_ref[pl.ds(r, 1), :] == sidx, 1.0, 0.0).astype(BF16)
        pick_ref[pl.ds(pl.multiple_of(r * cap, cap), cap), :] = jnp.dot(
            onehot, rhs[lax.rem(r, n_batch)], preferred_element_type=F32)
        return carry

    lax.fori_loop(0, rows, invert, 0, unroll=2)


def _topk(aff_rows, aff_tok, cap, n_batch, n_exp):
    rows, seq = aff_rows.shape
    return pl.pallas_call(
        functools.partial(_topk_kernel, cap=cap, n_batch=n_batch, n_exp=n_exp),
        out_shape=[jax.ShapeDtypeStruct(aff_rows.shape, jnp.int32),
                   jax.ShapeDtypeStruct((rows * cap, LANES), F32)],
        scratch_shapes=[pltpu.VMEM((n_batch, seq, LANES), BF16)],
        compiler_params=pltpu.CompilerParams(vmem_limit_bytes=V7X_VMEM_LIMIT_BYTES),
        name="expert_topk",
    )(aff_rows, aff_tok)


FF_CHUNK = 512
EXPERT_ROWS = 256


def _expert_kernel(idx_ref, xn_hbm, wg_ref, wu_ref, wd_ref, pick_ref, y_ref,
                   xbuf, sem, acc, wgb, wub, wdb, *, cap, n_f):
    e = pl.program_id(0)
    f = pl.program_id(1)
    n_exp = pl.num_programs(0)
    rows = acc.shape[0]
    slot = lax.rem(e, 2)
    nxt = 1 - slot
    e_nxt = jnp.minimum(e + 1, n_exp - 1)

    tile = TOKEN_TILE_ROWS

    def row_copy(src_tile_row, buf, dst_row):
        return pltpu.make_async_copy(
            xn_hbm.at[pl.ds(pl.multiple_of(src_tile_row, tile), tile), :],
            xbuf.at[pl.ds(pl.multiple_of((buf * rows + dst_row) * tile, tile), tile), :], sem.at[buf])

    def wait_rows(buf):
        pltpu.make_async_copy(xn_hbm.at[pl.ds(0, rows * tile), :],
                              xbuf.at[pl.ds(pl.multiple_of(buf * rows * tile, tile), rows * tile), :],
                              sem.at[buf]).wait()

    @pl.when((e == 0) & (f == 0))
    def _():
        def first(j, carry):
            row_copy(idx_ref[j], 0, j).start()
            return carry
        lax.fori_loop(0, rows, first, 0, unroll=8)

    @pl.when(f == 0)
    def _():
        wait_rows(slot)
        acc[...] = jnp.zeros(acc.shape, F32)

    wgb[...] = wg_ref[0, 0].astype(BF16)
    wub[...] = wu_ref[0, 0].astype(BF16)
    wdb[...] = wd_ref[0, 0].astype(BF16)

    per_step = rows // n_f
    per_chunk = per_step // (rows // EXPERT_ROWS)
    base = pl.multiple_of(f * per_step, per_step)
    idx_base = e_nxt * rows + base
    for ci, r0 in enumerate(range(0, rows, EXPERT_ROWS)):
        for k in range(ci * per_chunk, (ci + 1) * per_chunk):
            row_copy(idx_ref[idx_base + k], nxt, base + k).start()
        rsl = slice(r0, r0 + EXPERT_ROWS)
        xm = jnp.concatenate(
            [xbuf[pl.ds((slot * rows + r0) * tile + j, EXPERT_ROWS, stride=tile), :]
             for j in range(tile)], axis=1).astype(BF16)
        gate = jnp.dot(xm, wgb[...], preferred_element_type=F32)
        up = jnp.dot(xm, wub[...], preferred_element_type=F32)
        hid = (gate * (1.0 / (1.0 + jnp.exp(-gate))) * up).astype(BF16)
        acc[rsl, :] += jnp.dot(hid, wdb[...], preferred_element_type=F32)

    @pl.when(f == n_f - 1)
    def _():
        lane = lax.broadcasted_iota(jnp.int32, (1, LANES), 1) - (PICK_GATE_LANE0 + e)
        mine = (lane == 0) | (lane == n_exp) | (lane == 2 * n_exp)
        for r0 in range(0, rows, EXPERT_ROWS):
            rsl = slice(r0, r0 + EXPERT_ROWS)
            gate_col = jnp.sum(jnp.where(mine, pick_ref[rsl, :], 0.0), axis=1, keepdims=True)
            y_ref[0, rsl, :] = (acc[rsl, :] * gate_col).astype(BF16)

    @pl.when((e == n_exp - 1) & (f == n_f - 1))
    def _():
        wait_rows(nxt)


def _experts(idx2, xn_tiles, w_gate, w_up, w_down, layer, picked, cap):
    n_exp, rows = idx2.shape
    d = w_gate.shape[2]
    d_ff = w_gate.shape[3]
    n_f = d_ff // FF_CHUNK
    assert rows % (n_f * (rows // EXPERT_ROWS)) == 0
    return pl.pallas_call(
        functools.partial(_expert_kernel, cap=cap, n_f=n_f),
        grid_spec=pltpu.PrefetchScalarGridSpec(
            num_scalar_prefetch=1,
            grid=(n_exp, n_f),
            in_specs=[pl.BlockSpec(memory_space=pl.ANY),
                      pl.BlockSpec((1, 1, d, FF_CHUNK), lambda e, f, ix: (layer, e, 0, f)),
                      pl.BlockSpec((1, 1, d, FF_CHUNK), lambda e, f, ix: (layer, e, 0, f)),
                      pl.BlockSpec((1, 1, FF_CHUNK, d), lambda e, f, ix: (layer, e, f, 0)),
                      pl.BlockSpec((rows, LANES), lambda e, f, ix: (e, 0))],
            out_specs=pl.BlockSpec((1, rows, d), lambda e, f, ix: (e, 0, 0)),
            scratch_shapes=[pltpu.VMEM((2 * rows * TOKEN_TILE_ROWS, LANES), F32),
                            pltpu.SemaphoreType.DMA((2,)),
                            pltpu.VMEM((rows, d), F32),
                            pltpu.VMEM((d, FF_CHUNK), BF16),
                            pltpu.VMEM((d, FF_CHUNK), BF16),
                            pltpu.VMEM((FF_CHUNK, d), BF16)]),
        out_shape=jax.ShapeDtypeStruct((n_exp, rows, d), BF16),
        compiler_params=_params("arbitrary", "arbitrary"),
        name="expert_ffn",
    )(idx2.reshape(-1), xn_tiles, w_gate, w_up, w_down, picked)


COMBINE_GROUP = 4


def _combine_kernel(x_ref, slott_ref, y_ref, g_ref, o_ref, *, cap, final):
    slott = slott_ref[0]
    t, n_exp = slott.shape
    d = x_ref.shape[2]
    sidx = lax.broadcasted_iota(jnp.int32, (t, cap), 1)
    acc = x_ref[0]
    for e0 in range(0, n_exp, COMBINE_GROUP):
        parts = [jnp.where(slott[:, e:e + 1] == sidx, 1.0, 0.0).astype(BF16)
                 for e in range(e0, e0 + COMBINE_GROUP)]
        onehot_t = jnp.concatenate(parts, axis=1)
        yg = y_ref[e0:e0 + COMBINE_GROUP, 0].reshape(COMBINE_GROUP * cap, d)
        acc = acc + jnp.dot(onehot_t, yg, preferred_element_type=F32)
    if final:
        acc = _rms(acc, g_ref[...])
    o_ref[0] = acc


def _combine(x3, slott, y4, g, cap, final):
    b, s, d = x3.shape
    n_exp = y4.shape[0]
    return pl.pallas_call(
        functools.partial(_combine_kernel, cap=cap, final=final),
        grid=(b, s // ROW_TILE),
        in_specs=[pl.BlockSpec((1, ROW_TILE, d), lambda i, t: (i, t, 0)),
                  pl.BlockSpec((1, ROW_TILE, n_exp), lambda i, t: (i, t, 0)),
                  pl.BlockSpec((n_exp, 1, cap, d), lambda i, t: (0, i, 0, 0)),
                  pl.BlockSpec((1, d), lambda i, t: (0, 0))],
        out_specs=pl.BlockSpec((1, ROW_TILE, d), lambda i, t: (i, t, 0)),
        out_shape=jax.ShapeDtypeStruct((b, s, d), F32),
        compiler_params=_params("arbitrary", "arbitrary"),
        name="combine",
    )(x3, slott, y4, g)


def kernel(x, mem, mem_norm, norm_mix, w_in, conv_w, w_mem_kv, w_out, norm_ffn,
           w_router, w_gate, w_up, w_down, norm_final):
    b, s, d = x.shape
    depth = w_in.shape[0]
    n_exp = w_router.shape[2]
    conv_ch = conv_w.shape[2]
    mem_width = w_mem_kv.shape[2] // 2
    dil_width = (w_in.shape[2] - 3 * conv_ch - mem_width) // 3
    n_dil_heads = dil_width // HEAD_DIM
    cap = EC_CAPACITY * s // n_exp
    assert n_exp == N_EXPERTS and mem_width == N_MEM_HEADS * HEAD_DIM
    assert all(w // (2 * dl) == HALF_WINDOW for w, dl in DILATED_PATTERNS)

    slopes = jnp.asarray([2.0 ** (-8.0 * (h + 1) / n_dil_heads) for h in range(n_dil_heads)], F32)
    kv_all = jnp.transpose(w_mem_kv, (1, 0, 2)).reshape(d, depth * 2 * mem_width).astype(BF16)
    kvm3 = _memkv(mem.reshape(-1, d), mem_norm.reshape(1, d), kv_all).reshape(b, mem.shape[1], -1)

    x2 = x.reshape(b * s, d)
    out = None
    for l in range(depth):
        hc, hq = _inproj(x2, norm_mix[l].reshape(1, d), w_in[l].astype(BF16), 3 * conv_ch)
        hq3 = hq.reshape(b, s, -1)
        conv_o = _conv(hc.reshape(b, s, -1), conv_w[l], conv_ch)
        dil_o = _dilated(hq3, slopes, dil_width)
        mem_o = _memattn(hq3, kvm3, l, 3 * dil_width // LANES, mem_width)
        wr = jnp.pad(w_router[l], ((0, 0), (0, LANES - n_exp)))
        wr_hi = wr.astype(BF16)
        wr_pieces = jnp.concatenate([wr_hi, (wr - wr_hi.astype(F32)).astype(BF16)], axis=1)
        x1, xn, aff_tok = _outproj(x2, conv_o.reshape(b * s, -1), dil_o.reshape(b * s, -1),
                                   mem_o.reshape(b * s, -1), w_out[l].astype(BF16),
                                   norm_ffn[l].reshape(1, d), wr_pieces, n_exp)
        aff_rows = jnp.transpose(aff_tok[:, :n_exp]).reshape(n_exp * b, s)
        slot_rows, picked = _topk(aff_rows, aff_tok, cap, b, n_exp)
        token = (picked[:, 0] * (1 << TOK_DIGIT_BITS) + picked[:, 1]).astype(jnp.int32)
        idx = token.reshape(n_exp, b, cap) + (jnp.arange(b, dtype=jnp.int32) * s)[None, :, None]
        y = _experts(idx.reshape(n_exp, b * cap) * TOKEN_TILE_ROWS, xn, w_gate, w_up, w_down, l,
                     picked, cap)
        slott = jnp.transpose(slot_rows.reshape(n_exp, b, s), (1, 2, 0))
        final = l == depth - 1
        x3 = _combine(x1.reshape(b, s, d), slott, y.reshape(n_exp, b, cap, d),
                      norm_final.reshape(1, d), cap, final)
        x2 = x3.reshape(b * s, d)
        out = x3
    return out
```

```python
import functools

import jax
import jax.numpy as jnp
from jax import lax
from jax.experimental import pallas as pl
from jax.experimental.pallas import tpu as pltpu

F32 = jnp.float32
BF16 = jnp.bfloat16

HEAD_DIM = 64
N_MEM_HEADS = 4
DILATED_PATTERNS = ((128, 1), (512, 4), (2048, 16))
HALF_WINDOW = 64
N_EXPERTS = 16
EC_CAPACITY = 2
RMS_EPS = 1e-6
NEG_INF = -1e30

LANES = 128
ROW_TILE = 512
ATT_TILE = 128
V7X_VMEM_LIMIT_BYTES = 56 * 1024 * 1024


def _params(*semantics):
    return pltpu.CompilerParams(dimension_semantics=semantics,
                                vmem_limit_bytes=V7X_VMEM_LIMIT_BYTES)


def _rms(x, g):
    ms = jnp.mean(x * x, axis=-1, keepdims=True)
    return x * lax.rsqrt(ms + RMS_EPS) * g


def _nt_dot(a, b, **kw):
    return lax.dot_general(a, b, (((1,), (1,)), ((), ())), preferred_element_type=F32, **kw)


def _inproj_kernel(x_ref, g_ref, w_ref, hc_ref, hq_ref):
    xn = _rms(x_ref[...], g_ref[...]).astype(BF16)
    n_conv = hc_ref.shape[1]
    chunk = 2 * LANES
    for c in range(0, w_ref.shape[1], chunk):
        h = jnp.dot(xn, w_ref[:, c:c + chunk], preferred_element_type=F32)
        if c + chunk <= n_conv:
            hc_ref[:, c:c + chunk] = h
        elif c >= n_conv:
            hq_ref[:, c - n_conv:c - n_conv + chunk] = h.astype(BF16)
        else:
            k = n_conv - c
            hc_ref[:, c:n_conv] = h[:, :k]
            hq_ref[:, 0:chunk - k] = h[:, k:].astype(BF16)


def _inproj(x2, g, w, n_conv):
    n, d = x2.shape
    n_all = w.shape[1]
    return pl.pallas_call(
        _inproj_kernel,
        grid=(n // ROW_TILE,),
        in_specs=[pl.BlockSpec((ROW_TILE, d), lambda i: (i, 0)),
                  pl.BlockSpec((1, d), lambda i: (0, 0)),
                  pl.BlockSpec((d, n_all), lambda i: (0, 0))],
        out_specs=[pl.BlockSpec((ROW_TILE, n_conv), lambda i: (i, 0)),
                   pl.BlockSpec((ROW_TILE, n_all - n_conv), lambda i: (i, 0))],
        out_shape=[jax.ShapeDtypeStruct((n, n_conv), F32),
                   jax.ShapeDtypeStruct((n, n_all - n_conv), BF16)],
        compiler_params=_params("arbitrary"),
        name="inproj",
    )(x2, g, w)


def _conv_kernel(xc_ref, bg_ref, cg_ref, w_ref, o_ref):
    u = cg_ref[0] * xc_ref[0]
    s = u.shape[0]
    row = lax.broadcasted_iota(jnp.int32, u.shape, 0)
    prev = jnp.where(row == 0, 0.0, pltpu.roll(u, 1, 0))
    nxt = jnp.where(row == s - 1, 0.0, pltpu.roll(u, s - 1, 0))
    w = w_ref[...]
    conv = w[0:1] * prev + w[1:2] * u + w[2:3] * nxt
    o_ref[0] = (bg_ref[0] * conv).astype(BF16)


def _conv(hc3, conv_w, conv_ch):
    b, s, _ = hc3.shape
    nblk = conv_ch // LANES
    blk = (1, s, LANES)
    return pl.pallas_call(
        _conv_kernel,
        grid=(b, nblk),
        in_specs=[pl.BlockSpec(blk, lambda i, j: (i, 0, j)),
                  pl.BlockSpec(blk, lambda i, j: (i, 0, nblk + j)),
                  pl.BlockSpec(blk, lambda i, j: (i, 0, 2 * nblk + j)),
                  pl.BlockSpec((conv_w.shape[0], LANES), lambda i, j: (0, j))],
        out_specs=pl.BlockSpec(blk, lambda i, j: (i, 0, j)),
        out_shape=jax.ShapeDtypeStruct((b, s, conv_ch), BF16),
        compiler_params=_params("arbitrary", "arbitrary"),
        name="shortconv",
    )(hc3, hc3, hc3, conv_w)


DIL_TILE_UNROLL = 16


def _dil_cases(seq, dil):
    tps = (seq // dil) // ATT_TILE
    return tps, (("single",) if tps == 1 else ("first", "interior", "last"))


def _dil_kernel(slopes_ref, q_ref, k_ref, v_ref, o_ref,
                natq, natk, natv, perq, perk, perv, bias,
                num1, mx1, sum1, num2, mx2, sum2, num3, mx3, sum3, onat, *operands):
    seq = q_ref.shape[1]
    pad = HALF_WINDOW
    tq = ATT_TILE
    step = DILATED_PATTERNS[1][1]
    seg2 = seq // step
    seg3 = seg2 // step
    log_seg2 = seg2.bit_length() - 1
    log_step = step.bit_length() - 1
    pair = pl.program_id(0)
    lane = lax.broadcasted_iota(jnp.int32, (1, LANES), 1)
    head0 = lane < HEAD_DIM
    m0 = head0.astype(F32)
    m1 = 1.0 - m0

    @pl.when(pl.program_id(1) == 0)
    def _():
        for pi in range(len(DILATED_PATTERNS)):
            _, _, kp, vp = operands[4 * pi:4 * pi + 4]
            kp[0:pad, :] = jnp.zeros((pad, LANES), BF16)
            kp[pad + seq:pad + seq + pad, :] = jnp.zeros((pad, LANES), BF16)
            vp[0:pad, :] = jnp.zeros((pad, 2 * LANES), BF16)
            vp[pad + seq:pad + seq + pad, :] = jnp.zeros((pad, 2 * LANES), BF16)
            vp[pad:pad + seq, LANES:2 * LANES] = jnp.ones((seq, LANES), BF16)
        row = lax.broadcasted_iota(jnp.int32, (2 * tq, 2 * tq), 0)
        col = lax.broadcasted_iota(jnp.int32, (2 * tq, 2 * tq), 1)
        kc = col - pad
        arel = jnp.abs(kc - (row & (tq - 1)))
        band = arel <= HALF_WINDOW
        slope_rows = jnp.where(row < tq, slopes_ref[2 * pair], slopes_ref[2 * pair + 1])
        idx = 0
        for _, dil in DILATED_PATTERNS:
            dist_bias = -slope_rows * (arel * dil).astype(F32)
            for case in _dil_cases(seq, dil)[1]:
                valid = band
                if case in ("first", "single"):
                    valid = valid & (kc >= 0)
                if case in ("last", "single"):
                    valid = valid & (kc < tq)
                bias[idx] = jnp.where(valid, dist_bias, NEG_INF)
                idx += 1

    natq[...] = q_ref[0].astype(F32) * (HEAD_DIM ** -0.5)
    natk[...] = k_ref[0].astype(F32)
    natv[...] = v_ref[0].astype(F32)

    def set_operands(pi, rows, qv, kv, vv):
        qp0, qp1, kp, vp = operands[4 * pi:4 * pi + 4]
        lo, hi = rows
        qp0[lo:hi, :] = (qv * m0).astype(BF16)
        qp1[lo:hi, :] = (qv * m1).astype(BF16)
        kp[pad + lo:pad + hi, :] = kv.astype(BF16)
        vp[pad + lo:pad + hi, 0:LANES] = vv.astype(BF16)

    def run_tiles(pi, case0, tps, dests):
        num_ref, mx_ref, sum_ref = dests
        qp0, qp1, kp, vp = operands[4 * pi:4 * pi + 4]

        def tile(m, carry):
            r0 = pl.multiple_of(m * tq, tq)
            qc = jnp.concatenate([qp0[pl.ds(r0, tq), :], qp1[pl.ds(r0, tq), :]], axis=0)
            kt = kp[pl.ds(r0, 2 * tq), :]
            vt = vp[pl.ds(r0, 2 * tq), :]
            if tps == 1:
                case = case0
                dst = pl.ds((m >> log_step) * seg2 + (m & (step - 1)), tq, stride=step)
            else:
                pos = m & (tps - 1)
                case = case0 + jnp.where(pos == 0, 0, jnp.where(pos == tps - 1, 2, 1))
                dst = pl.ds(r0, tq)
            s = _nt_dot(qc, kt) + bias[case]
            mx = jnp.max(s, axis=-1, keepdims=True)
            pe = jnp.exp(s - mx)
            pv = jnp.dot(pe.astype(BF16), vt, preferred_element_type=F32)
            num_ref[dst, :] = jnp.where(head0, pv[0:tq, 0:LANES], pv[tq:2 * tq, 0:LANES])
            sum_ref[dst, :] = jnp.where(head0, pv[0:tq, LANES:2 * LANES],
                                        pv[tq:2 * tq, LANES:2 * LANES])
            mx_ref[dst, :] = jnp.where(head0, mx[0:tq], mx[tq:2 * tq])
            return carry

        lax.fori_loop(0, seq // tq, tile, 0, unroll=DIL_TILE_UNROLL)

    case0 = 0
    set_operands(0, (0, seq), natq[...], natk[...], natv[...])
    tps1, cases1 = _dil_cases(seq, DILATED_PATTERNS[0][1])
    run_tiles(0, case0, tps1, (num1, mx1, sum1))
    case0 += len(cases1)

    for b in range(step):
        rows = (b * seg2, (b + 1) * seg2)
        qv = natq[pl.ds(b, seg2, stride=step), :]
        kv = natk[pl.ds(b, seg2, stride=step), :]
        vv = natv[pl.ds(b, seg2, stride=step), :]
        perq[rows[0]:rows[1], :] = qv
        perk[rows[0]:rows[1], :] = kv
        perv[rows[0]:rows[1], :] = vv
        set_operands(1, rows, qv, kv, vv)
    tps2, cases2 = _dil_cases(seq, DILATED_PATTERNS[1][1])
    run_tiles(1, case0, tps2, (num2, mx2, sum2))
    case0 += len(cases2)

    for sgm in range(step * step):
        b, a = divmod(sgm, step)
        src = pl.ds(b * seg2 + a, seg3, stride=step)
        set_operands(2, (sgm * seg3, (sgm + 1) * seg3), perq[src, :], perk[src, :], perv[src, :])
    tps3, _ = _dil_cases(seq, DILATED_PATTERNS[2][1])
    assert tps3 == 1 and seg3 == tq
    run_tiles(2, case0, tps3, (num3, mx3, sum3))

    def merge(c, carry):
        p0 = pl.multiple_of(c * tq, tq)
        per_rows = pl.ds(p0, tq)
        nat_rows = pl.ds((p0 & (seg2 - 1)) * step + (p0 >> log_seg2), tq, stride=step)
        a1, a2, a3 = mx1[nat_rows, :], mx2[per_rows, :], mx3[per_rows, :]
        top = jnp.maximum(jnp.maximum(a1, a2), a3)
        e1, e2, e3 = jnp.exp(a1 - top), jnp.exp(a2 - top), jnp.exp(a3 - top)
        den = e1 * sum1[nat_rows, :] + e2 * sum2[per_rows, :] + e3 * sum3[per_rows, :]
        num = e1 * num1[nat_rows, :] + e2 * num2[per_rows, :] + e3 * num3[per_rows, :]
        onat[nat_rows, :] = num / den
        return carry

    lax.fori_loop(0, seq // tq, merge, 0, unroll=2)
    o_ref[0] = onat[...].astype(BF16)


def _dilated(hq3, slopes, dil_width):
    b, s, _ = hq3.shape
    npair = dil_width // LANES
    blk = (1, s, LANES)
    dils = [d for _, d in DILATED_PATTERNS]
    assert dils[0] == 1 and dils[2] == dils[1] * dils[1]
    n_cases = sum(len(_dil_cases(s, d)[1]) for d in dils)
    pad = HALF_WINDOW
    scratch = [pltpu.VMEM((s, LANES), F32)] * 6
    scratch += [pltpu.VMEM((n_cases, 2 * ATT_TILE, 2 * ATT_TILE), F32)]
    scratch += [pltpu.VMEM((s, LANES), F32)] * 10
    scratch += [pltpu.VMEM((s, LANES), BF16), pltpu.VMEM((s, LANES), BF16),
                pltpu.VMEM((s + 2 * pad, LANES), BF16),
                pltpu.VMEM((s + 2 * pad, 2 * LANES), BF16)] * len(dils)
    return pl.pallas_call(
        _dil_kernel,
        grid_spec=pltpu.PrefetchScalarGridSpec(
            num_scalar_prefetch=1,
            grid=(npair, b),
            in_specs=[pl.BlockSpec(blk, lambda j, i, sl: (i, 0, j)),
                      pl.BlockSpec(blk, lambda j, i, sl: (i, 0, npair + j)),
                      pl.BlockSpec(blk, lambda j, i, sl: (i, 0, 2 * npair + j))],
            out_specs=pl.BlockSpec(blk, lambda j, i, sl: (i, 0, j)),
            scratch_shapes=scratch),
        out_shape=jax.ShapeDtypeStruct((b, s, dil_width), BF16),
        compiler_params=_params("arbitrary", "arbitrary"),
        name="dilated_attn",
    )(slopes, hq3, hq3, hq3)


def _memkv_kernel(m_ref, g_ref, w_ref, o_ref):
    mn = _rms(m_ref[...], g_ref[...]).astype(BF16)
    o_ref[...] = jnp.dot(mn, w_ref[...], preferred_element_type=F32).astype(BF16)


def _memkv(mem2, g, w):
    n, d = mem2.shape
    nw = w.shape[1]
    return pl.pallas_call(
        _memkv_kernel,
        grid=(n // ROW_TILE,),
        in_specs=[pl.BlockSpec((ROW_TILE, d), lambda i: (i, 0)),
                  pl.BlockSpec((1, d), lambda i: (0, 0)),
                  pl.BlockSpec((d, nw), lambda i: (0, 0))],
        out_specs=pl.BlockSpec((ROW_TILE, nw), lambda i: (i, 0)),
        out_shape=jax.ShapeDtypeStruct((n, nw), BF16),
        compiler_params=_params("arbitrary"),
        name="mem_kv",
    )(mem2, g, w)


def _memattn_kernel(q_ref, km_ref, vm_ref, o_ref, qp0, qp1, vext):
    seq = q_ref.shape[1]
    tq = 2 * ATT_TILE
    lane = lax.broadcasted_iota(jnp.int32, (1, LANES), 1)
    head0 = lane < HEAD_DIM
    m0 = head0.astype(F32)
    q = q_ref[0].astype(F32) * (HEAD_DIM ** -0.5)
    qp0[...] = (q * m0).astype(BF16)
    qp1[...] = (q * (1.0 - m0)).astype(BF16)
    vext[:, 0:LANES] = vm_ref[0]
    vext[:, LANES:2 * LANES] = jnp.ones((vext.shape[0], LANES), BF16)
    km = km_ref[0]

    def tile(m, carry):
        rows = pl.ds(pl.multiple_of(m * tq, tq), tq)
        qc = jnp.concatenate([qp0[rows, :], qp1[rows, :]], axis=0)
        s = _nt_dot(qc, km)
        pe = jnp.exp(s - jnp.max(s, axis=-1, keepdims=True))
        pv = jnp.dot(pe.astype(BF16), vext[...], preferred_element_type=F32)
        o = pv[:, 0:LANES] / pv[:, LANES:2 * LANES]
        o_ref[0, rows, :] = jnp.where(head0, o[0:tq], o[tq:2 * tq]).astype(BF16)
        return carry

    lax.fori_loop(0, seq // tq, tile, 0, unroll=4)


def _memattn(hq3, kvm3, layer, qm_block0, mem_width):
    b, s, _ = hq3.shape
    m = kvm3.shape[1]
    npair = mem_width // LANES
    kv_blocks = 2 * npair
    return pl.pallas_call(
        _memattn_kernel,
        grid=(b, npair),
        in_specs=[pl.BlockSpec((1, s, LANES), lambda i, j: (i, 0, qm_block0 + j)),
                  pl.BlockSpec((1, m, LANES), lambda i, j: (i, 0, layer * kv_blocks + j)),
                  pl.BlockSpec((1, m, LANES), lambda i, j: (i, 0, layer * kv_blocks + npair + j))],
        out_specs=pl.BlockSpec((1, s, LANES), lambda i, j: (i, 0, j)),
        out_shape=jax.ShapeDtypeStruct((b, s, mem_width), BF16),
        scratch_shapes=[pltpu.VMEM((s, LANES), BF16), pltpu.VMEM((s, LANES), BF16),
                        pltpu.VMEM((m, 2 * LANES), BF16)],
        compiler_params=_params("arbitrary", "arbitrary"),
        name="mem_attn",
    )(hq3, kvm3, kvm3)


OUT_SUB_ROWS = 256
TOKEN_TILE_ROWS = 8


def _outproj_kernel(x_ref, c_ref, d_ref, m_ref, w_ref, g_ref, wr_ref,
                    x1_ref, xn_ref, aff_ref, *, n_exp):
    valid = lax.broadcasted_iota(jnp.int32, (1, LANES), 1) < n_exp
    for r0 in range(0, x_ref.shape[0], OUT_SUB_ROWS):
        rs = slice(r0, r0 + OUT_SUB_ROWS)
        cat = jnp.concatenate([c_ref[rs, :], d_ref[rs, :], m_ref[rs, :]], axis=1)
        x1 = x_ref[rs, :] + jnp.dot(cat, w_ref[...], preferred_element_type=F32)
        x1_ref[rs, :] = x1
        xn = _rms(x1, g_ref[...])
        for j in range(xn.shape[1] // LANES):
            xn_ref[pl.ds(r0 * TOKEN_TILE_ROWS + j, OUT_SUB_ROWS, stride=TOKEN_TILE_ROWS), :] = (
                xn[:, j * LANES:(j + 1) * LANES])
        xh = xn.astype(BF16)
        xl = (xn - xh.astype(F32)).astype(BF16)
        both = jnp.dot(xh, wr_ref[...], preferred_element_type=F32)
        logits = (both[:, 0:LANES] + both[:, LANES:2 * LANES]
                  + jnp.dot(xl, wr_ref[:, 0:LANES], preferred_element_type=F32))
        logits = jnp.where(valid, logits, NEG_INF)
        ex = jnp.exp(logits - jnp.max(logits, axis=1, keepdims=True))
        aff_ref[rs, :] = ex / jnp.sum(ex, axis=1, keepdims=True)


def _outproj(x2, conv_o, dil_o, mem_o, w, g, wr_pieces, n_exp):
    n, d = x2.shape
    assert d == TOKEN_TILE_ROWS * LANES
    row = lambda i: (i, 0)
    fixed = lambda i: (0, 0)
    return pl.pallas_call(
        functools.partial(_outproj_kernel, n_exp=n_exp),
        grid=(n // ROW_TILE,),
        in_specs=[pl.BlockSpec((ROW_TILE, d), row),
                  pl.BlockSpec((ROW_TILE, conv_o.shape[1]), row),
                  pl.BlockSpec((ROW_TILE, dil_o.shape[1]), row),
                  pl.BlockSpec((ROW_TILE, mem_o.shape[1]), row),
                  pl.BlockSpec(w.shape, fixed),
                  pl.BlockSpec((1, d), fixed),
                  pl.BlockSpec((d, 2 * LANES), fixed)],
        out_specs=[pl.BlockSpec((ROW_TILE, d), row),
                   pl.BlockSpec((ROW_TILE * TOKEN_TILE_ROWS, LANES), row),
                   pl.BlockSpec((ROW_TILE, LANES), row)],
        out_shape=[jax.ShapeDtypeStruct((n, d), F32),
                   jax.ShapeDtypeStruct((n * TOKEN_TILE_ROWS, LANES), F32),
                   jax.ShapeDtypeStruct((n, LANES), F32)],
        compiler_params=_params("arbitrary"),
        name="outproj_router",
    )(x2, conv_o, dil_o, mem_o, w, g, wr_pieces)


TOK_DIGIT_BITS = 6
PICK_GATE_LANE0 = 2


def _excl_cumsum_lanes(mask_bf16):
    rows, s = mask_bf16.shape
    chunk = 2 * LANES
    acc = jnp.zeros((rows, s), F32)
    for c in range(0, s, chunk):
        src = lax.broadcasted_iota(jnp.int32, (chunk, s), 0) + c
        dst = lax.broadcasted_iota(jnp.int32, (chunk, s), 1)
        tri = jnp.where(src < dst, 1.0, 0.0).astype(BF16)
        acc = acc + jnp.dot(mask_bf16[:, c:c + chunk], tri, preferred_element_type=F32)
    return acc


def _topk_kernel(aff_ref, afft_ref, slot_ref, pick_ref, idx_ref, rhs, *, cap, n_batch, n_exp):
    aff = aff_ref[...]
    rows, seq = aff.shape
    bits = jnp.zeros((rows, 1), jnp.int32)
    for bit in range(30, -1, -1):
        cand = bits | (1 << bit)
        cnt = jnp.sum(jnp.where(aff >= pltpu.bitcast(cand, F32), 1.0, 0.0), axis=1, keepdims=True)
        bits = jnp.where(cnt >= cap, cand, bits)
    thr = pltpu.bitcast(bits, F32)
    gt = aff > thr
    eq = aff == thr
    need = cap - jnp.sum(jnp.where(gt, 1.0, 0.0), axis=1, keepdims=True)
    rank_eq = _excl_cumsum_lanes(jnp.where(eq, 1.0, 0.0).astype(BF16))
    sel = jnp.where(gt, 1.0, jnp.where(eq & (rank_eq < need), 1.0, 0.0))
    rank = _excl_cumsum_lanes(sel.astype(BF16))
    slot_ref[...] = jnp.where(sel > 0.5, rank, -1.0).astype(jnp.int32)

    lane = lax.broadcasted_iota(jnp.int32, (1, LANES), 1)
    tok = lax.broadcasted_iota(jnp.int32, (seq, LANES), 0)
    tok_part = jnp.where(lane == 0, tok >> TOK_DIGIT_BITS,
                         jnp.where(lane == 1, tok & ((1 << TOK_DIGIT_BITS) - 1), 0)).astype(F32)
    src = lax.broadcasted_iota(jnp.int32, (LANES, LANES), 0)
    dst = lax.broadcasted_iota(jnp.int32, (LANES, LANES), 1)

    def place(piece):
        return jnp.where((src < n_exp) & (dst == src + PICK_GATE_LANE0 + piece * n_exp),
                         1.0, 0.0).astype(BF16)

    for bi in range(n_batch):
        a = afft_ref[bi * seq:(bi + 1) * seq, :]
        a_hi = a.astype(BF16)
        rem = a - a_hi.astype(F32)
        a_mid = rem.astype(BF16)
        a_lo = (rem - a_mid.astype(F32)).astype(BF16)
        rhs[bi] = (tok_part
                   + jnp.dot(a_hi, place(0), preferred_element_type=F32)
                   + jnp.dot(a_mid, place(1), preferred_element_type=F32)
                   + jnp.dot(a_lo, place(2), preferred_element_type=F32)).astype(BF16)

    sidx = lax.broadcasted_iota(jnp.int32, (cap, seq), 0)

    def invert(r, carry):
        bi = lax.rem(r, n_batch)
        onehot = jnp.where(slot_ref[pl.ds(r, 1), :] == sidx, 1.0, 0.0).astype(BF16)
        pick = jnp.dot(onehot, rhs[bi], preferred_element_type=F32)
        pick_ref[pl.ds(pl.multiple_of(r * cap, cap), cap), :] = pick
        digits = jnp.transpose(pick)
        token = (digits[0:1] * (1 << TOK_DIGIT_BITS) + digits[1:2]).astype(jnp.int32)
        idx_ref[pl.ds(r, 1), :] = (token + bi * seq) * TOKEN_TILE_ROWS
        return carry

    lax.fori_loop(0, rows, invert, 0, unroll=2)


def _topk(aff_rows, aff_tok, cap, n_batch, n_exp):
    rows, seq = aff_rows.shape
    return pl.pallas_call(
        functools.partial(_topk_kernel, cap=cap, n_batch=n_batch, n_exp=n_exp),
        out_shape=[jax.ShapeDtypeStruct(aff_rows.shape, jnp.int32),
                   jax.ShapeDtypeStruct((rows * cap, LANES), F32),
                   jax.ShapeDtypeStruct((rows, cap), jnp.int32)],
        scratch_shapes=[pltpu.VMEM((n_batch, seq, LANES), BF16)],
        compiler_params=pltpu.CompilerParams(vmem_limit_bytes=V7X_VMEM_LIMIT_BYTES),
        name="expert_topk",
    )(aff_rows, aff_tok)


FF_CHUNK = 512
EXPERT_ROWS = 256


def _expert_kernel(idx_ref, xn_hbm, wg_ref, wu_ref, wd_ref, pick_ref, y_ref,
                   xbuf, sem, acc, wgb, wub, wdb, *, cap, n_f):
    e = pl.program_id(0)
    f = pl.program_id(1)
    n_exp = pl.num_programs(0)
    rows = acc.shape[0]
    slot = lax.rem(e, 2)
    nxt = 1 - slot
    e_nxt = jnp.minimum(e + 1, n_exp - 1)

    tile = TOKEN_TILE_ROWS

    def row_copy(src_tile_row, buf, dst_row):
        return pltpu.make_async_copy(
            xn_hbm.at[pl.ds(pl.multiple_of(src_tile_row, tile), tile), :],
            xbuf.at[pl.ds(pl.multiple_of((buf * rows + dst_row) * tile, tile), tile), :], sem.at[buf])

    def wait_rows(buf):
        pltpu.make_async_copy(xn_hbm.at[pl.ds(0, rows * tile), :],
                              xbuf.at[pl.ds(pl.multiple_of(buf * rows * tile, tile), rows * tile), :],
                              sem.at[buf]).wait()

    @pl.when((e == 0) & (f == 0))
    def _():
        def first(j, carry):
            row_copy(idx_ref[j], 0, j).start()
            return carry
        lax.fori_loop(0, rows, first, 0, unroll=8)

    @pl.when(f == 0)
    def _():
        wait_rows(slot)
        acc[...] = jnp.zeros(acc.shape, F32)

    wgb[...] = wg_ref[0, 0].astype(BF16)
    wub[...] = wu_ref[0, 0].astype(BF16)
    wdb[...] = wd_ref[0, 0].astype(BF16)

    per_step = rows // n_f
    per_chunk = per_step // (rows // EXPERT_ROWS)
    base = pl.multiple_of(f * per_step, per_step)
    idx_base = e_nxt * rows + base
    for ci, r0 in enumerate(range(0, rows, EXPERT_ROWS)):
        for k in range(ci * per_chunk, (ci + 1) * per_chunk):
            row_copy(idx_ref[idx_base + k], nxt, base + k).start()
        rsl = slice(r0, r0 + EXPERT_ROWS)
        xm = jnp.concatenate(
            [xbuf[pl.ds((slot * rows + r0) * tile + j, EXPERT_ROWS, stride=tile), :]
             for j in range(tile)], axis=1).astype(BF16)
        gate = jnp.dot(xm, wgb[...], preferred_element_type=F32)
        up = jnp.dot(xm, wub[...], preferred_element_type=F32)
        hid = (gate * (1.0 / (1.0 + jnp.exp(-gate))) * up).astype(BF16)
        acc[rsl, :] += jnp.dot(hid, wdb[...], preferred_element_type=F32)

    @pl.when(f == n_f - 1)
    def _():
        lane = lax.broadcasted_iota(jnp.int32, (1, LANES), 1) - (PICK_GATE_LANE0 + e)
        mine = (lane == 0) | (lane == n_exp) | (lane == 2 * n_exp)
        for r0 in range(0, rows, EXPERT_ROWS):
            rsl = slice(r0, r0 + EXPERT_ROWS)
            gate_col = jnp.sum(jnp.where(mine, pick_ref[rsl, :], 0.0), axis=1, keepdims=True)
            y_ref[0, rsl, :] = (acc[rsl, :] * gate_col).astype(BF16)

    @pl.when((e == n_exp - 1) & (f == n_f - 1))
    def _():
        wait_rows(nxt)


def _experts(idx2, xn_tiles, w_gate, w_up, w_down, layer, picked, cap):
    n_exp, rows = idx2.shape
    d = w_gate.shape[2]
    d_ff = w_gate.shape[3]
    n_f = d_ff // FF_CHUNK
    assert rows % (n_f * (rows // EXPERT_ROWS)) == 0
    return pl.pallas_call(
        functools.partial(_expert_kernel, cap=cap, n_f=n_f),
        grid_spec=pltpu.PrefetchScalarGridSpec(
            num_scalar_prefetch=1,
            grid=(n_exp, n_f),
            in_specs=[pl.BlockSpec(memory_space=pl.ANY),
                      pl.BlockSpec((1, 1, d, FF_CHUNK), lambda e, f, ix: (layer, e, 0, f)),
                      pl.BlockSpec((1, 1, d, FF_CHUNK), lambda e, f, ix: (layer, e, 0, f)),
                      pl.BlockSpec((1, 1, FF_CHUNK, d), lambda e, f, ix: (layer, e, f, 0)),
                      pl.BlockSpec((rows, LANES), lambda e, f, ix: (e, 0))],
            out_specs=pl.BlockSpec((1, rows, d), lambda e, f, ix: (e, 0, 0)),
            scratch_shapes=[pltpu.VMEM((2 * rows * TOKEN_TILE_ROWS, LANES), F32),
                            pltpu.SemaphoreType.DMA((2,)),
                            pltpu.VMEM((rows, d), F32),
                            pltpu.VMEM((d, FF_CHUNK), BF16),
                            pltpu.VMEM((d, FF_CHUNK), BF16),
                            pltpu.VMEM((FF_CHUNK, d), BF16)]),
        out_shape=jax.ShapeDtypeStruct((n_exp, rows, d), BF16),
        compiler_params=_params("arbitrary", "arbitrary"),
        name="expert_ffn",
    )(idx2.reshape(-1), xn_tiles, w_gate, w_up, w_down, picked)


COMBINE_GROUP = 4
COMBINE_WINDOW = 128
BF16_ROWS = 16


def _combine_kernel(win_ref, fits_ref, x_ref, slott_ref, y_ref, g_ref, o_ref, *, cap, final):
    bi = pl.program_id(0)
    ti = pl.program_id(1)
    tile_id = bi * pl.num_programs(1) + ti
    slott = slott_ref[0]
    t, n_exp = slott.shape
    d = x_ref.shape[2]

    def finish(acc):
        if final:
            acc = _rms(acc, g_ref[...])
        o_ref[0] = acc

    @pl.when(fits_ref[tile_id] != 0)
    def _():
        lane = lax.broadcasted_iota(jnp.int32, (t, COMBINE_WINDOW), 1)
        acc = x_ref[0]
        for e0 in range(0, n_exp, COMBINE_GROUP):
            parts, ys = [], []
            for e in range(e0, e0 + COMBINE_GROUP):
                start = pl.multiple_of(win_ref[tile_id * n_exp + e], BF16_ROWS)
                parts.append(jnp.where(slott[:, e:e + 1] - start == lane, 1.0, 0.0).astype(BF16))
                ys.append(y_ref[e, 0, pl.ds(start, COMBINE_WINDOW), :])
            acc = acc + jnp.dot(jnp.concatenate(parts, axis=1), jnp.concatenate(ys, axis=0),
                                preferred_element_type=F32)
        finish(acc)

    @pl.when(fits_ref[tile_id] == 0)
    def _():
        sidx = lax.broadcasted_iota(jnp.int32, (t, cap), 1)
        acc = x_ref[0]
        for e0 in range(0, n_exp, COMBINE_GROUP):
            parts = [jnp.where(slott[:, e:e + 1] == sidx, 1.0, 0.0).astype(BF16)
                     for e in range(e0, e0 + COMBINE_GROUP)]
            yg = y_ref[e0:e0 + COMBINE_GROUP, 0].reshape(COMBINE_GROUP * cap, d)
            acc = acc + jnp.dot(jnp.concatenate(parts, axis=1), yg, preferred_element_type=F32)
        finish(acc)


def _combine_windows(slot_rows, n_exp, b, s, cap):
    nt = s // ROW_TILE
    cnt = jnp.sum((slot_rows >= 0).reshape(n_exp, b, nt, ROW_TILE), axis=-1, dtype=jnp.int32)
    end = jnp.cumsum(cnt, axis=-1)
    start = jnp.minimum((end - cnt) // BF16_ROWS * BF16_ROWS, cap - COMBINE_WINDOW)
    fits = jnp.all(end - start <= COMBINE_WINDOW, axis=0)
    return jnp.transpose(start, (1, 2, 0)).reshape(-1), fits.reshape(-1).astype(jnp.int32)


def _combine(x3, slott, y4, g, win, fits, cap, final):
    b, s, d = x3.shape
    n_exp = y4.shape[0]
    assert cap >= COMBINE_WINDOW and cap % BF16_ROWS == 0
    return pl.pallas_call(
        functools.partial(_combine_kernel, cap=cap, final=final),
        grid_spec=pltpu.PrefetchScalarGridSpec(
            num_scalar_prefetch=2,
            grid=(b, s // ROW_TILE),
            in_specs=[pl.BlockSpec((1, ROW_TILE, d), lambda i, t, w, f: (i, t, 0)),
                      pl.BlockSpec((1, ROW_TILE, n_exp), lambda i, t, w, f: (i, t, 0)),
                      pl.BlockSpec((n_exp, 1, cap, d), lambda i, t, w, f: (0, i, 0, 0)),
                      pl.BlockSpec((1, d), lambda i, t, w, f: (0, 0))],
            out_specs=pl.BlockSpec((1, ROW_TILE, d), lambda i, t, w, f: (i, t, 0))),
        out_shape=jax.ShapeDtypeStruct((b, s, d), F32),
        compiler_params=_params("arbitrary", "arbitrary"),
        name="combine",
    )(win, fits, x3, slott, y4, g)


def kernel(x, mem, mem_norm, norm_mix, w_in, conv_w, w_mem_kv, w_out, norm_ffn,
           w_router, w_gate, w_up, w_down, norm_final):
    b, s, d = x.shape
    depth = w_in.shape[0]
    n_exp = w_router.shape[2]
    conv_ch = conv_w.shape[2]
    mem_width = w_mem_kv.shape[2] // 2
    dil_width = (w_in.shape[2] - 3 * conv_ch - mem_width) // 3
    n_dil_heads = dil_width // HEAD_DIM
    cap = EC_CAPACITY * s // n_exp
    assert n_exp == N_EXPERTS and mem_width == N_MEM_HEADS * HEAD_DIM
    assert all(w // (2 * dl) == HALF_WINDOW for w, dl in DILATED_PATTERNS)

    slopes = jnp.asarray([2.0 ** (-8.0 * (h + 1) / n_dil_heads) for h in range(n_dil_heads)], F32)
    kv_all = jnp.transpose(w_mem_kv, (1, 0, 2)).reshape(d, depth * 2 * mem_width).astype(BF16)
    kvm3 = _memkv(mem.reshape(-1, d), mem_norm.reshape(1, d), kv_all).reshape(b, mem.shape[1], -1)

    x2 = x.reshape(b * s, d)
    out = None
    for l in range(depth):
        hc, hq = _inproj(x2, norm_mix[l].reshape(1, d), w_in[l].astype(BF16), 3 * conv_ch)
        hq3 = hq.reshape(b, s, -1)
        conv_o = _conv(hc.reshape(b, s, -1), conv_w[l], conv_ch)
        dil_o = _dilated(hq3, slopes, dil_width)
        mem_o = _memattn(hq3, kvm3, l, 3 * dil_width // LANES, mem_width)
        wr = jnp.pad(w_router[l], ((0, 0), (0, LANES - n_exp)))
        wr_hi = wr.astype(BF16)
        wr_pieces = jnp.concatenate([wr_hi, (wr - wr_hi.astype(F32)).astype(BF16)], axis=1)
        x1, xn, aff_tok = _outproj(x2, conv_o.reshape(b * s, -1), dil_o.reshape(b * s, -1),
                                   mem_o.reshape(b * s, -1), w_out[l].astype(BF16),
                                   norm_ffn[l].reshape(1, d), wr_pieces, n_exp)
        aff_rows = jnp.transpose(aff_tok[:, :n_exp]).reshape(n_exp * b, s)
        slot_rows, picked, idx_rows = _topk(aff_rows, aff_tok, cap, b, n_exp)
        y = _experts(idx_rows.reshape(n_exp, b * cap), xn, w_gate, w_up, w_down, l, picked, cap)
        slott = jnp.transpose(slot_rows.reshape(n_exp, b, s), (1, 2, 0))
        win, fits = _combine_windows(slot_rows, n_exp, b, s, cap)
        final = l == depth - 1
        x3 = _combine(x1.reshape(b, s, d), slott, y.reshape(n_exp, b, cap, d),
                      norm_final.reshape(1, d), win, fits, cap, final)
        x2 = x3.reshape(b * s, d)
        out = x3
    return out
```

```python
import functools

import jax
import jax.numpy as jnp
from jax import lax
from jax.experimental import pallas as pl
from jax.experimental.pallas import tpu as pltpu

F32 = jnp.float32
BF16 = jnp.bfloat16

HEAD_DIM = 64
N_MEM_HEADS = 4
DILATED_PATTERNS = ((128, 1), (512, 4), (2048, 16))
HALF_WINDOW = 64
N_EXPERTS = 16
EC_CAPACITY = 2
RMS_EPS = 1e-6
NEG_INF = -1e30

LANES = 128
ROW_TILE = 512
ATT_TILE = 128
V7X_VMEM_LIMIT_BYTES = 56 * 1024 * 1024


def _params(*semantics):
    return pltpu.CompilerParams(dimension_semantics=semantics,
                                vmem_limit_bytes=V7X_VMEM_LIMIT_BYTES)


def _rms(x, g):
    ms = jnp.mean(x * x, axis=-1, keepdims=True)
    return x * lax.rsqrt(ms + RMS_EPS) * g


def _nt_dot(a, b, **kw):
    return lax.dot_general(a, b, (((1,), (1,)), ((), ())), preferred_element_type=F32, **kw)


HALO_ROWS = 8


def _inproj_kernel(x_ref, xprev_ref, xnext_ref, g_ref, w_ref, cw_ref, conv_ref, hq_ref, hc,
                   *, conv_ch, tiles_per_seq):
    t = x_ref.shape[0]
    n_conv = 3 * conv_ch
    g = g_ref[...]
    halo = jnp.concatenate([xprev_ref[...], xnext_ref[...]], axis=0)
    xa = jnp.concatenate([_rms(x_ref[...], g), _rms(halo, g)], axis=0).astype(BF16)
    chunk = 2 * LANES
    for c in range(0, w_ref.shape[1], chunk):
        if c >= n_conv:
            hq_ref[:, c - n_conv:c - n_conv + chunk] = jnp.dot(
                xa[0:t], w_ref[:, c:c + chunk], preferred_element_type=F32).astype(BF16)
            continue
        h = jnp.dot(xa, w_ref[:, c:c + chunk], preferred_element_type=F32)
        if c + chunk <= n_conv:
            hc[:, c:c + chunk] = h
        else:
            k = n_conv - c
            hc[:, c:n_conv] = h[:, :k]
            hq_ref[:, 0:chunk - k] = h[0:t, k:].astype(BF16)

    u_all = hc[:, 2 * conv_ch:3 * conv_ch] * hc[:, 0:conv_ch]
    u = u_all[0:t]
    pos = lax.rem(pl.program_id(0), tiles_per_seq)
    u_before = jnp.where(pos == 0, 0.0, u_all[t + HALO_ROWS - 1:t + HALO_ROWS])
    u_after = jnp.where(pos == tiles_per_seq - 1, 0.0, u_all[t + HALO_ROWS:t + HALO_ROWS + 1])
    row = lax.broadcasted_iota(jnp.int32, u.shape, 0)
    prev = jnp.where(row == 0, u_before, pltpu.roll(u, 1, 0))
    nxt = jnp.where(row == t - 1, u_after, pltpu.roll(u, t - 1, 0))
    w = cw_ref[...]
    conv = w[0:1] * prev + w[1:2] * u + w[2:3] * nxt
    conv_ref[...] = (hc[0:t, conv_ch:2 * conv_ch] * conv).astype(BF16)


def _inproj(x2, g, w, conv_w, conv_ch, seq):
    n, d = x2.shape
    n_all = w.shape[1]
    n_conv = 3 * conv_ch
    per_tile = ROW_TILE // HALO_ROWS
    last = n // HALO_ROWS - 1
    assert seq % ROW_TILE == 0
    return pl.pallas_call(
        functools.partial(_inproj_kernel, conv_ch=conv_ch, tiles_per_seq=seq // ROW_TILE),
        grid=(n // ROW_TILE,),
        in_specs=[pl.BlockSpec((ROW_TILE, d), lambda i: (i, 0)),
                  pl.BlockSpec((HALO_ROWS, d), lambda i: (jnp.maximum(i * per_tile - 1, 0), 0)),
                  pl.BlockSpec((HALO_ROWS, d), lambda i: (jnp.minimum((i + 1) * per_tile, last), 0)),
                  pl.BlockSpec((1, d), lambda i: (0, 0)),
                  pl.BlockSpec((d, n_all), lambda i: (0, 0)),
                  pl.BlockSpec(conv_w.shape, lambda i: (0, 0))],
        out_specs=[pl.BlockSpec((ROW_TILE, conv_ch), lambda i: (i, 0)),
                   pl.BlockSpec((ROW_TILE, n_all - n_conv), lambda i: (i, 0))],
        out_shape=[jax.ShapeDtypeStruct((n, conv_ch), BF16),
                   jax.ShapeDtypeStruct((n, n_all - n_conv), BF16)],
        scratch_shapes=[pltpu.VMEM((ROW_TILE + 2 * HALO_ROWS, n_conv), F32)],
        compiler_params=_params("arbitrary"),
        name="inproj_conv",
    )(x2, x2, x2, g, w, conv_w)


DIL_TILE_UNROLL = 16


def _dil_cases(seq, dil):
    tps = (seq // dil) // ATT_TILE
    return tps, (("single",) if tps == 1 else ("first", "interior", "last"))


def _dil_kernel(slopes_ref, q_ref, k_ref, v_ref, o_ref,
                natq, natk, natv, perq, perk, perv, bias,
                num1, mx1, sum1, num2, mx2, sum2, num3, mx3, sum3, onat, *operands):
    seq = q_ref.shape[1]
    pad = HALF_WINDOW
    tq = ATT_TILE
    step = DILATED_PATTERNS[1][1]
    seg2 = seq // step
    seg3 = seg2 // step
    log_seg2 = seg2.bit_length() - 1
    log_step = step.bit_length() - 1
    pair = pl.program_id(0)
    lane = lax.broadcasted_iota(jnp.int32, (1, LANES), 1)
    head0 = lane < HEAD_DIM
    m0 = head0.astype(F32)
    m1 = 1.0 - m0

    @pl.when(pl.program_id(1) == 0)
    def _():
        for pi in range(len(DILATED_PATTERNS)):
            _, _, kp, vp = operands[4 * pi:4 * pi + 4]
            kp[0:pad, :] = jnp.zeros((pad, LANES), BF16)
            kp[pad + seq:pad + seq + pad, :] = jnp.zeros((pad, LANES), BF16)
            vp[0:pad, :] = jnp.zeros((pad, 2 * LANES), BF16)
            vp[pad + seq:pad + seq + pad, :] = jnp.zeros((pad, 2 * LANES), BF16)
            vp[pad:pad + seq, LANES:2 * LANES] = jnp.ones((seq, LANES), BF16)
        row = lax.broadcasted_iota(jnp.int32, (2 * tq, 2 * tq), 0)
        col = lax.broadcasted_iota(jnp.int32, (2 * tq, 2 * tq), 1)
        kc = col - pad
        arel = jnp.abs(kc - (row & (tq - 1)))
        band = arel <= HALF_WINDOW
        slope_rows = jnp.where(row < tq, slopes_ref[2 * pair], slopes_ref[2 * pair + 1])
        idx = 0
        for _, dil in DILATED_PATTERNS:
            dist_bias = -slope_rows * (arel * dil).astype(F32)
            for case in _dil_cases(seq, dil)[1]:
                valid = band
                if case in ("first", "single"):
                    valid = valid & (kc >= 0)
                if case in ("last", "single"):
                    valid = valid & (kc < tq)
                bias[idx] = jnp.where(valid, dist_bias, NEG_INF)
                idx += 1

    natq[...] = q_ref[0].astype(F32) * (HEAD_DIM ** -0.5)
    natk[...] = k_ref[0].astype(F32)
    natv[...] = v_ref[0].astype(F32)

    def set_operands(pi, rows, qv, kv, vv):
        qp0, qp1, kp, vp = operands[4 * pi:4 * pi + 4]
        lo, hi = rows
        qp0[lo:hi, :] = (qv * m0).astype(BF16)
        qp1[lo:hi, :] = (qv * m1).astype(BF16)
        kp[pad + lo:pad + hi, :] = kv.astype(BF16)
        vp[pad + lo:pad + hi, 0:LANES] = vv.astype(BF16)

    def run_tiles(pi, case0, tps, dests):
        num_ref, mx_ref, sum_ref = dests
        qp0, qp1, kp, vp = operands[4 * pi:4 * pi + 4]

        def tile(m, carry):
            r0 = pl.multiple_of(m * tq, tq)
            qc = jnp.concatenate([qp0[pl.ds(r0, tq), :], qp1[pl.ds(r0, tq), :]], axis=0)
            kt = kp[pl.ds(r0, 2 * tq), :]
            vt = vp[pl.ds(r0, 2 * tq), :]
            if tps == 1:
                case = case0
                dst = pl.ds((m >> log_step) * seg2 + (m & (step - 1)), tq, stride=step)
            else:
                pos = m & (tps - 1)
                case = case0 + jnp.where(pos == 0, 0, jnp.where(pos == tps - 1, 2, 1))
                dst = pl.ds(r0, tq)
            s = _nt_dot(qc, kt) + bias[case]
            mx = jnp.max(s, axis=-1, keepdims=True)
            pe = jnp.exp(s - mx)
            pv = jnp.dot(pe.astype(BF16), vt, preferred_element_type=F32)
            num_ref[dst, :] = jnp.where(head0, pv[0:tq, 0:LANES], pv[tq:2 * tq, 0:LANES])
            sum_ref[dst, :] = jnp.where(head0, pv[0:tq, LANES:2 * LANES],
                                        pv[tq:2 * tq, LANES:2 * LANES])
            mx_ref[dst, :] = jnp.where(head0, mx[0:tq], mx[tq:2 * tq])
            return carry

        lax.fori_loop(0, seq // tq, tile, 0, unroll=DIL_TILE_UNROLL)

    case0 = 0
    set_operands(0, (0, seq), natq[...], natk[...], natv[...])
    tps1, cases1 = _dil_cases(seq, DILATED_PATTERNS[0][1])
    run_tiles(0, case0, tps1, (num1, mx1, sum1))
    case0 += len(cases1)

    for b in range(step):
        rows = (b * seg2, (b + 1) * seg2)
        qv = natq[pl.ds(b, seg2, stride=step), :]
        kv = natk[pl.ds(b, seg2, stride=step), :]
        vv = natv[pl.ds(b, seg2, stride=step), :]
        perq[rows[0]:rows[1], :] = qv
        perk[rows[0]:rows[1], :] = kv
        perv[rows[0]:rows[1], :] = vv
        set_operands(1, rows, qv, kv, vv)
    tps2, cases2 = _dil_cases(seq, DILATED_PATTERNS[1][1])
    run_tiles(1, case0, tps2, (num2, mx2, sum2))
    case0 += len(cases2)

    for sgm in range(step * step):
        b, a = divmod(sgm, step)
        src = pl.ds(b * seg2 + a, seg3, stride=step)
        set_operands(2, (sgm * seg3, (sgm + 1) * seg3), perq[src, :], perk[src, :], perv[src, :])
    tps3, _ = _dil_cases(seq, DILATED_PATTERNS[2][1])
    assert tps3 == 1 and seg3 == tq
    run_tiles(2, case0, tps3, (num3, mx3, sum3))

    def merge(c, carry):
        p0 = pl.multiple_of(c * tq, tq)
        per_rows = pl.ds(p0, tq)
        nat_rows = pl.ds((p0 & (seg2 - 1)) * step + (p0 >> log_seg2), tq, stride=step)
        a1, a2, a3 = mx1[nat_rows, :], mx2[per_rows, :], mx3[per_rows, :]
        top = jnp.maximum(jnp.maximum(a1, a2), a3)
        e1, e2, e3 = jnp.exp(a1 - top), jnp.exp(a2 - top), jnp.exp(a3 - top)
        den = e1 * sum1[nat_rows, :] + e2 * sum2[per_rows, :] + e3 * sum3[per_rows, :]
        num = e1 * num1[nat_rows, :] + e2 * num2[per_rows, :] + e3 * num3[per_rows, :]
        onat[nat_rows, :] = num / den
        return carry

    lax.fori_loop(0, seq // tq, merge, 0, unroll=2)
    o_ref[0] = onat[...].astype(BF16)


def _dilated(hq3, slopes, dil_width):
    b, s, _ = hq3.shape
    npair = dil_width // LANES
    blk = (1, s, LANES)
    dils = [d for _, d in DILATED_PATTERNS]
    assert dils[0] == 1 and dils[2] == dils[1] * dils[1]
    n_cases = sum(len(_dil_cases(s, d)[1]) for d in dils)
    pad = HALF_WINDOW
    scratch = [pltpu.VMEM((s, LANES), F32)] * 6
    scratch += [pltpu.VMEM((n_cases, 2 * ATT_TILE, 2 * ATT_TILE), F32)]
    scratch += [pltpu.VMEM((s, LANES), F32)] * 10
    scratch += [pltpu.VMEM((s, LANES), BF16), pltpu.VMEM((s, LANES), BF16),
                pltpu.VMEM((s + 2 * pad, LANES), BF16),
                pltpu.VMEM((s + 2 * pad, 2 * LANES), BF16)] * len(dils)
    return pl.pallas_call(
        _dil_kernel,
        grid_spec=pltpu.PrefetchScalarGridSpec(
            num_scalar_prefetch=1,
            grid=(npair, b),
            in_specs=[pl.BlockSpec(blk, lambda j, i, sl: (i, 0, j)),
                      pl.BlockSpec(blk, lambda j, i, sl: (i, 0, npair + j)),
                      pl.BlockSpec(blk, lambda j, i, sl: (i, 0, 2 * npair + j))],
            out_specs=pl.BlockSpec(blk, lambda j, i, sl: (i, 0, j)),
            scratch_shapes=scratch),
        out_shape=jax.ShapeDtypeStruct((b, s, dil_width), BF16),
        compiler_params=_params("arbitrary", "arbitrary"),
        name="dilated_attn",
    )(slopes, hq3, hq3, hq3)


def _memkv_kernel(m_ref, g_ref, w_ref, o_ref):
    mn = _rms(m_ref[...], g_ref[...]).astype(BF16)
    o_ref[...] = jnp.dot(mn, w_ref[...], preferred_element_type=F32).astype(BF16)


def _memkv(mem2, g, w):
    n, d = mem2.shape
    nw = w.shape[1]
    return pl.pallas_call(
        _memkv_kernel,
        grid=(n // ROW_TILE,),
        in_specs=[pl.BlockSpec((ROW_TILE, d), lambda i: (i, 0)),
                  pl.BlockSpec((1, d), lambda i: (0, 0)),
                  pl.BlockSpec((d, nw), lambda i: (0, 0))],
        out_specs=pl.BlockSpec((ROW_TILE, nw), lambda i: (i, 0)),
        out_shape=jax.ShapeDtypeStruct((n, nw), BF16),
        compiler_params=_params("arbitrary"),
        name="mem_kv",
    )(mem2, g, w)


def _memattn_kernel(q_ref, km_ref, vm_ref, o_ref, qp0, qp1, vext):
    seq = q_ref.shape[1]
    tq = 2 * ATT_TILE
    lane = lax.broadcasted_iota(jnp.int32, (1, LANES), 1)
    head0 = lane < HEAD_DIM
    m0 = head0.astype(F32)
    q = q_ref[0].astype(F32) * (HEAD_DIM ** -0.5)
    qp0[...] = (q * m0).astype(BF16)
    qp1[...] = (q * (1.0 - m0)).astype(BF16)
    vext[:, 0:LANES] = vm_ref[0]
    vext[:, LANES:2 * LANES] = jnp.ones((vext.shape[0], LANES), BF16)
    km = km_ref[0]

    def tile(m, carry):
        rows = pl.ds(pl.multiple_of(m * tq, tq), tq)
        qc = jnp.concatenate([qp0[rows, :], qp1[rows, :]], axis=0)
        s = _nt_dot(qc, km)
        pe = jnp.exp(s - jnp.max(s, axis=-1, keepdims=True))
        pv = jnp.dot(pe.astype(BF16), vext[...], preferred_element_type=F32)
        o = pv[:, 0:LANES] / pv[:, LANES:2 * LANES]
        o_ref[0, rows, :] = jnp.where(head0, o[0:tq], o[tq:2 * tq]).astype(BF16)
        return carry

    lax.fori_loop(0, seq // tq, tile, 0, unroll=4)


def _memattn(hq3, kvm3, layer, qm_block0, mem_width):
    b, s, _ = hq3.shape
    m = kvm3.shape[1]
    npair = mem_width // LANES
    kv_blocks = 2 * npair
    return pl.pallas_call(
        _memattn_kernel,
        grid=(b, npair),
        in_specs=[pl.BlockSpec((1, s, LANES), lambda i, j: (i, 0, qm_block0 + j)),
                  pl.BlockSpec((1, m, LANES), lambda i, j: (i, 0, layer * kv_blocks + j)),
                  pl.BlockSpec((1, m, LANES), lambda i, j: (i, 0, layer * kv_blocks + npair + j))],
        out_specs=pl.BlockSpec((1, s, LANES), lambda i, j: (i, 0, j)),
        out_shape=jax.ShapeDtypeStruct((b, s, mem_width), BF16),
        scratch_shapes=[pltpu.VMEM((s, LANES), BF16), pltpu.VMEM((s, LANES), BF16),
                        pltpu.VMEM((m, 2 * LANES), BF16)],
        compiler_params=_params("arbitrary", "arbitrary"),
        name="mem_attn",
    )(hq3, kvm3, kvm3)


OUT_SUB_ROWS = 256
TOKEN_TILE_ROWS = 8


def _outproj_kernel(x_ref, c_ref, d_ref, m_ref, w_ref, g_ref, wr_ref,
                    x1_ref, xn_ref, aff_ref, *, n_exp):
    valid = lax.broadcasted_iota(jnp.int32, (1, LANES), 1) < n_exp
    for r0 in range(0, x_ref.shape[0], OUT_SUB_ROWS):
        rs = slice(r0, r0 + OUT_SUB_ROWS)
        cat = jnp.concatenate([c_ref[rs, :], d_ref[rs, :], m_ref[rs, :]], axis=1)
        x1 = x_ref[rs, :] + jnp.dot(cat, w_ref[...], preferred_element_type=F32)
        x1_ref[rs, :] = x1
        xn = _rms(x1, g_ref[...])
        for j in range(xn.shape[1] // LANES):
            xn_ref[pl.ds(r0 * TOKEN_TILE_ROWS + j, OUT_SUB_ROWS, stride=TOKEN_TILE_ROWS), :] = (
                xn[:, j * LANES:(j + 1) * LANES])
        xh = xn.astype(BF16)
        xl = (xn - xh.astype(F32)).astype(BF16)
        both = jnp.dot(xh, wr_ref[...], preferred_element_type=F32)
        logits = (both[:, 0:LANES] + both[:, LANES:2 * LANES]
                  + jnp.dot(xl, wr_ref[:, 0:LANES], preferred_element_type=F32))
        logits = jnp.where(valid, logits, NEG_INF)
        ex = jnp.exp(logits - jnp.max(logits, axis=1, keepdims=True))
        aff_ref[rs, :] = ex / jnp.sum(ex, axis=1, keepdims=True)


def _outproj(x2, conv_o, dil_o, mem_o, w, g, wr_pieces, n_exp):
    n, d = x2.shape
    assert d == TOKEN_TILE_ROWS * LANES
    row = lambda i: (i, 0)
    fixed = lambda i: (0, 0)
    return pl.pallas_call(
        functools.partial(_outproj_kernel, n_exp=n_exp),
        grid=(n // ROW_TILE,),
        in_specs=[pl.BlockSpec((ROW_TILE, d), row),
                  pl.BlockSpec((ROW_TILE, conv_o.shape[1]), row),
                  pl.BlockSpec((ROW_TILE, dil_o.shape[1]), row),
                  pl.BlockSpec((ROW_TILE, mem_o.shape[1]), row),
                  pl.BlockSpec(w.shape, fixed),
                  pl.BlockSpec((1, d), fixed),
                  pl.BlockSpec((d, 2 * LANES), fixed)],
        out_specs=[pl.BlockSpec((ROW_TILE, d), row),
                   pl.BlockSpec((ROW_TILE * TOKEN_TILE_ROWS, LANES), row),
                   pl.BlockSpec((ROW_TILE, LANES), row)],
        out_shape=[jax.ShapeDtypeStruct((n, d), F32),
                   jax.ShapeDtypeStruct((n * TOKEN_TILE_ROWS, LANES), F32),
                   jax.ShapeDtypeStruct((n, LANES), F32)],
        compiler_params=_params("arbitrary"),
        name="outproj_router",
    )(x2, conv_o, dil_o, mem_o, w, g, wr_pieces)


TOK_DIGIT_BITS = 6
PICK_GATE_LANE0 = 2


def _excl_cumsum_lanes(mask_bf16):
    rows, s = mask_bf16.shape
    chunk = 2 * LANES
    acc = jnp.zeros((rows, s), F32)
    for c in range(0, s, chunk):
        src = lax.broadcasted_iota(jnp.int32, (chunk, s), 0) + c
        dst = lax.broadcasted_iota(jnp.int32, (chunk, s), 1)
        tri = jnp.where(src < dst, 1.0, 0.0).astype(BF16)
        acc = acc + jnp.dot(mask_bf16[:, c:c + chunk], tri, preferred_element_type=F32)
    return acc


def _topk_kernel(aff_ref, afft_ref, slot_ref, pick_ref, idx_ref, rhs, *, cap, n_batch, n_exp):
    aff = aff_ref[...]
    rows, seq = aff.shape
    bits = jnp.zeros((rows, 1), jnp.int32)
    for bit in range(30, -1, -1):
        cand = bits | (1 << bit)
        cnt = jnp.sum(jnp.where(aff >= pltpu.bitcast(cand, F32), 1.0, 0.0), axis=1, keepdims=True)
        bits = jnp.where(cnt >= cap, cand, bits)
    thr = pltpu.bitcast(bits, F32)
    gt = aff > thr
    eq = aff == thr
    need = cap - jnp.sum(jnp.where(gt, 1.0, 0.0), axis=1, keepdims=True)
    rank_eq = _excl_cumsum_lanes(jnp.where(eq, 1.0, 0.0).astype(BF16))
    sel = jnp.where(gt, 1.0, jnp.where(eq & (rank_eq < need), 1.0, 0.0))
    rank = _excl_cumsum_lanes(sel.astype(BF16))
    slot_ref[...] = jnp.where(sel > 0.5, rank, -1.0).astype(jnp.int32)

    lane = lax.broadcasted_iota(jnp.int32, (1, LANES), 1)
    tok = lax.broadcasted_iota(jnp.int32, (seq, LANES), 0)
    tok_part = jnp.where(lane == 0, tok >> TOK_DIGIT_BITS,
                         jnp.where(lane == 1, tok & ((1 << TOK_DIGIT_BITS) - 1), 0)).astype(F32)
    src = lax.broadcasted_iota(jnp.int32, (LANES, LANES), 0)
    dst = lax.broadcasted_iota(jnp.int32, (LANES, LANES), 1)

    def place(piece):
        return jnp.where((src < n_exp) & (dst == src + PICK_GATE_LANE0 + piece * n_exp),
                         1.0, 0.0).astype(BF16)

    for bi in range(n_batch):
        a = afft_ref[bi * seq:(bi + 1) * seq, :]
        a_hi = a.astype(BF16)
        rem = a - a_hi.astype(F32)
        a_mid = rem.astype(BF16)
        a_lo = (rem - a_mid.astype(F32)).astype(BF16)
        rhs[bi] = (tok_part
                   + jnp.dot(a_hi, place(0), preferred_element_type=F32)
                   + jnp.dot(a_mid, place(1), preferred_element_type=F32)
                   + jnp.dot(a_lo, place(2), preferred_element_type=F32)).astype(BF16)

    sidx = lax.broadcasted_iota(jnp.int32, (cap, seq), 0)

    def invert(r, carry):
        bi = lax.rem(r, n_batch)
        onehot = jnp.where(slot_ref[pl.ds(r, 1), :] == sidx, 1.0, 0.0).astype(BF16)
        pick = jnp.dot(onehot, rhs[bi], preferred_element_type=F32)
        pick_ref[pl.ds(pl.multiple_of(r * cap, cap), cap), :] = pick
        digits = jnp.transpose(pick)
        token = (digits[0:1] * (1 << TOK_DIGIT_BITS) + digits[1:2]).astype(jnp.int32)
        idx_ref[pl.ds(r, 1), :] = (token + bi * seq) * TOKEN_TILE_ROWS
        return carry

    lax.fori_loop(0, rows, invert, 0, unroll=2)


def _topk(aff_rows, aff_tok, cap, n_batch, n_exp):
    rows, seq = aff_rows.shape
    return pl.pallas_call(
        functools.partial(_topk_kernel, cap=cap, n_batch=n_batch, n_exp=n_exp),
        out_shape=[jax.ShapeDtypeStruct(aff_rows.shape, jnp.int32),
                   jax.ShapeDtypeStruct((rows * cap, LANES), F32),
                   jax.ShapeDtypeStruct((rows, cap), jnp.int32)],
        scratch_shapes=[pltpu.VMEM((n_batch, seq, LANES), BF16)],
        compiler_params=pltpu.CompilerParams(vmem_limit_bytes=V7X_VMEM_LIMIT_BYTES),
        name="expert_topk",
    )(aff_rows, aff_tok)


FF_CHUNK = 512
EXPERT_ROWS = 512


def _expert_kernel(idx_ref, xn_hbm, wg_ref, wu_ref, wd_ref, pick_ref, y_ref,
                   xbuf, sem, acc, wgb, wub, wdb, *, cap, n_f):
    e = pl.program_id(0)
    f = pl.program_id(1)
    n_exp = pl.num_programs(0)
    rows = acc.shape[0]
    slot = lax.rem(e, 2)
    nxt = 1 - slot
    e_nxt = jnp.minimum(e + 1, n_exp - 1)

    tile = TOKEN_TILE_ROWS

    def row_copy(src_tile_row, buf, dst_row):
        return pltpu.make_async_copy(
            xn_hbm.at[pl.ds(pl.multiple_of(src_tile_row, tile), tile), :],
            xbuf.at[pl.ds(pl.multiple_of((buf * rows + dst_row) * tile, tile), tile), :], sem.at[buf])

    def wait_rows(buf):
        pltpu.make_async_copy(xn_hbm.at[pl.ds(0, rows * tile), :],
                              xbuf.at[pl.ds(pl.multiple_of(buf * rows * tile, tile), rows * tile), :],
                              sem.at[buf]).wait()

    @pl.when((e == 0) & (f == 0))
    def _():
        def first(j, carry):
            row_copy(idx_ref[j], 0, j).start()
            return carry
        lax.fori_loop(0, rows, first, 0, unroll=8)

    @pl.when(f == 0)
    def _():
        wait_rows(slot)
        acc[...] = jnp.zeros(acc.shape, F32)

    wgb[...] = wg_ref[0, 0].astype(BF16)
    wub[...] = wu_ref[0, 0].astype(BF16)
    wdb[...] = wd_ref[0, 0].astype(BF16)

    per_step = rows // n_f
    per_chunk = per_step // (rows // EXPERT_ROWS)
    base = pl.multiple_of(f * per_step, per_step)
    idx_base = e_nxt * rows + base
    for ci, r0 in enumerate(range(0, rows, EXPERT_ROWS)):
        for k in range(ci * per_chunk, (ci + 1) * per_chunk):
            row_copy(idx_ref[idx_base + k], nxt, base + k).start()
        rsl = slice(r0, r0 + EXPERT_ROWS)
        xm = jnp.concatenate(
            [xbuf[pl.ds((slot * rows + r0) * tile + j, EXPERT_ROWS, stride=tile), :]
             for j in range(tile)], axis=1).astype(BF16)
        gate = jnp.dot(xm, wgb[...], preferred_element_type=F32)
        up = jnp.dot(xm, wub[...], preferred_element_type=F32)
        hid = (gate * (1.0 / (1.0 + jnp.exp(-gate))) * up).astype(BF16)
        acc[rsl, :] += jnp.dot(hid, wdb[...], preferred_element_type=F32)

    @pl.when(f == n_f - 1)
    def _():
        lane = lax.broadcasted_iota(jnp.int32, (1, LANES), 1) - (PICK_GATE_LANE0 + e)
        mine = (lane == 0) | (lane == n_exp) | (lane == 2 * n_exp)
        for r0 in range(0, rows, EXPERT_ROWS):
            rsl = slice(r0, r0 + EXPERT_ROWS)
            gate_col = jnp.sum(jnp.where(mine, pick_ref[rsl, :], 0.0), axis=1, keepdims=True)
            y_ref[0, rsl, :] = (acc[rsl, :] * gate_col).astype(BF16)

    @pl.when((e == n_exp - 1) & (f == n_f - 1))
    def _():
        wait_rows(nxt)


def _experts(idx2, xn_tiles, w_gate, w_up, w_down, layer, picked, cap):
    n_exp, rows = idx2.shape
    d = w_gate.shape[2]
    d_ff = w_gate.shape[3]
    n_f = d_ff // FF_CHUNK
    assert rows % (n_f * (rows // EXPERT_ROWS)) == 0
    return pl.pallas_call(
        functools.partial(_expert_kernel, cap=cap, n_f=n_f),
        grid_spec=pltpu.PrefetchScalarGridSpec(
            num_scalar_prefetch=1,
            grid=(n_exp, n_f),
            in_specs=[pl.BlockSpec(memory_space=pl.ANY),
                      pl.BlockSpec((1, 1, d, FF_CHUNK), lambda e, f, ix: (layer, e, 0, f)),
                      pl.BlockSpec((1, 1, d, FF_CHUNK), lambda e, f, ix: (layer, e, 0, f)),
                      pl.BlockSpec((1, 1, FF_CHUNK, d), lambda e, f, ix: (layer, e, f, 0)),
                      pl.BlockSpec((rows, LANES), lambda e, f, ix: (e, 0))],
            out_specs=pl.BlockSpec((1, rows, d), lambda e, f, ix: (e, 0, 0)),
            scratch_shapes=[pltpu.VMEM((2 * rows * TOKEN_TILE_ROWS, LANES), F32),
                            pltpu.SemaphoreType.DMA((2,)),
                            pltpu.VMEM((rows, d), F32),
                            pltpu.VMEM((d, FF_CHUNK), BF16),
                            pltpu.VMEM((d, FF_CHUNK), BF16),
                            pltpu.VMEM((FF_CHUNK, d), BF16)]),
        out_shape=jax.ShapeDtypeStruct((n_exp, rows, d), BF16),
        compiler_params=_params("arbitrary", "arbitrary"),
        name="expert_ffn",
    )(idx2.reshape(-1), xn_tiles, w_gate, w_up, w_down, picked)


COMBINE_GROUP = 4
COMBINE_WINDOW = 128
BF16_ROWS = 16


def _combine_kernel(win_ref, fits_ref, x_ref, slott_ref, y_ref, g_ref, o_ref, *, cap, final):
    bi = pl.program_id(0)
    ti = pl.program_id(1)
    tile_id = bi * pl.num_programs(1) + ti
    slott = slott_ref[0]
    t, n_exp = slott.shape
    d = x_ref.shape[2]

    def finish(acc):
        if final:
            acc = _rms(acc, g_ref[...])
        o_ref[0] = acc

    @pl.when(fits_ref[tile_id] != 0)
    def _():
        lane = lax.broadcasted_iota(jnp.int32, (t, COMBINE_WINDOW), 1)
        acc = x_ref[0]
        for e0 in range(0, n_exp, COMBINE_GROUP):
            parts, ys = [], []
            for e in range(e0, e0 + COMBINE_GROUP):
                start = pl.multiple_of(win_ref[tile_id * n_exp + e], BF16_ROWS)
                parts.append(jnp.where(slott[:, e:e + 1] - start == lane, 1.0, 0.0).astype(BF16))
                ys.append(y_ref[e, 0, pl.ds(start, COMBINE_WINDOW), :])
            acc = acc + jnp.dot(jnp.concatenate(parts, axis=1), jnp.concatenate(ys, axis=0),
                                preferred_element_type=F32)
        finish(acc)

    @pl.when(fits_ref[tile_id] == 0)
    def _():
        sidx = lax.broadcasted_iota(jnp.int32, (t, cap), 1)
        acc = x_ref[0]
        for e0 in range(0, n_exp, COMBINE_GROUP):
            parts = [jnp.where(slott[:, e:e + 1] == sidx, 1.0, 0.0).astype(BF16)
                     for e in range(e0, e0 + COMBINE_GROUP)]
            yg = y_ref[e0:e0 + COMBINE_GROUP, 0].reshape(COMBINE_GROUP * cap, d)
            acc = acc + jnp.dot(jnp.concatenate(parts, axis=1), yg, preferred_element_type=F32)
        finish(acc)


def _combine_windows(slot_rows, n_exp, b, s, cap):
    nt = s // ROW_TILE
    cnt = jnp.sum((slot_rows >= 0).reshape(n_exp, b, nt, ROW_TILE), axis=-1, dtype=jnp.int32)
    end = jnp.cumsum(cnt, axis=-1)
    start = jnp.minimum((end - cnt) // BF16_ROWS * BF16_ROWS, cap - COMBINE_WINDOW)
    fits = jnp.all(end - start <= COMBINE_WINDOW, axis=0)
    return jnp.transpose(start, (1, 2, 0)).reshape(-1), fits.reshape(-1).astype(jnp.int32)


def _combine(x3, slott, y4, g, win, fits, cap, final):
    b, s, d = x3.shape
    n_exp = y4.shape[0]
    assert cap >= COMBINE_WINDOW and cap % BF16_ROWS == 0
    return pl.pallas_call(
        functools.partial(_combine_kernel, cap=cap, final=final),
        grid_spec=pltpu.PrefetchScalarGridSpec(
            num_scalar_prefetch=2,
            grid=(b, s // ROW_TILE),
            in_specs=[pl.BlockSpec((1, ROW_TILE, d), lambda i, t, w, f: (i, t, 0)),
                      pl.BlockSpec((1, ROW_TILE, n_exp), lambda i, t, w, f: (i, t, 0)),
                      pl.BlockSpec((n_exp, 1, cap, d), lambda i, t, w, f: (0, i, 0, 0)),
                      pl.BlockSpec((1, d), lambda i, t, w, f: (0, 0))],
            out_specs=pl.BlockSpec((1, ROW_TILE, d), lambda i, t, w, f: (i, t, 0))),
        out_shape=jax.ShapeDtypeStruct((b, s, d), F32),
        compiler_params=_params("arbitrary", "arbitrary"),
        name="combine",
    )(win, fits, x3, slott, y4, g)


def kernel(x, mem, mem_norm, norm_mix, w_in, conv_w, w_mem_kv, w_out, norm_ffn,
           w_router, w_gate, w_up, w_down, norm_final):
    b, s, d = x.shape
    depth = w_in.shape[0]
    n_exp = w_router.shape[2]
    conv_ch = conv_w.shape[2]
    mem_width = w_mem_kv.shape[2] // 2
    dil_width = (w_in.shape[2] - 3 * conv_ch - mem_width) // 3
    n_dil_heads = dil_width // HEAD_DIM
    cap = EC_CAPACITY * s // n_exp
    assert n_exp == N_EXPERTS and mem_width == N_MEM_HEADS * HEAD_DIM
    assert all(w // (2 * dl) == HALF_WINDOW for w, dl in DILATED_PATTERNS)

    slopes = jnp.asarray([2.0 ** (-8.0 * (h + 1) / n_dil_heads) for h in range(n_dil_heads)], F32)
    kv_all = jnp.transpose(w_mem_kv, (1, 0, 2)).reshape(d, depth * 2 * mem_width).astype(BF16)
    kvm3 = _memkv(mem.reshape(-1, d), mem_norm.reshape(1, d), kv_all).reshape(b, mem.shape[1], -1)

    x2 = x.reshape(b * s, d)
    out = None
    for l in range(depth):
        conv_o, hq = _inproj(x2, norm_mix[l].reshape(1, d), w_in[l].astype(BF16), conv_w[l],
                             conv_ch, s)
        hq3 = hq.reshape(b, s, -1)
        dil_o = _dilated(hq3, slopes, dil_width)
        mem_o = _memattn(hq3, kvm3, l, 3 * dil_width // LANES, mem_width)
        wr = jnp.pad(w_router[l], ((0, 0), (0, LANES - n_exp)))
        wr_hi = wr.astype(BF16)
        wr_pieces = jnp.concatenate([wr_hi, (wr - wr_hi.astype(F32)).astype(BF16)], axis=1)
        x1, xn, aff_tok = _outproj(x2, conv_o, dil_o.reshape(b * s, -1),
                                   mem_o.reshape(b * s, -1), w_out[l].astype(BF16),
                                   norm_ffn[l].reshape(1, d), wr_pieces, n_exp)
        aff_rows = jnp.transpose(aff_tok[:, :n_exp]).reshape(n_exp * b, s)
        slot_rows, picked, idx_rows = _topk(aff_rows, aff_tok, cap, b, n_exp)
        y = _experts(idx_rows.reshape(n_exp, b * cap), xn, w_gate, w_up, w_down, l, picked, cap)
        slott = jnp.transpose(slot_rows.reshape(n_exp, b, s), (1, 2, 0))
        win, fits = _combine_windows(slot_rows, n_exp, b, s, cap)
        final = l == depth - 1
        x3 = _combine(x1.reshape(b, s, d), slott, y.reshape(n_exp, b, cap, d),
                      norm_final.reshape(1, d), win, fits, cap, final)
        x2 = x3.reshape(b * s, d)
        out = x3
    return out
```

```python
import functools

import jax
import jax.numpy as jnp
from jax import lax
from jax.experimental import pallas as pl
from jax.experimental.pallas import tpu as pltpu

F32 = jnp.float32
BF16 = jnp.bfloat16

HEAD_DIM = 64
N_MEM_HEADS = 4
DILATED_PATTERNS = ((128, 1), (512, 4), (2048, 16))
HALF_WINDOW = 64
N_EXPERTS = 16
EC_CAPACITY = 2
RMS_EPS = 1e-6
NEG_INF = -1e30

LANES = 128
ROW_TILE = 512
ATT_TILE = 128
V7X_VMEM_LIMIT_BYTES = 56 * 1024 * 1024


def _params(*semantics):
    return pltpu.CompilerParams(dimension_semantics=semantics,
                                vmem_limit_bytes=V7X_VMEM_LIMIT_BYTES)


def _rms(x, g):
    ms = jnp.mean(x * x, axis=-1, keepdims=True)
    return x * lax.rsqrt(ms + RMS_EPS) * g


def _nt_dot(a, b, **kw):
    return lax.dot_general(a, b, (((1,), (1,)), ((), ())), preferred_element_type=F32, **kw)


HALO_ROWS = 8


def _inproj_kernel(x_ref, xprev_ref, xnext_ref, g_ref, wf_ref, cw_ref, conv_ref, hq_ref, hc, w_ref,
                   *, conv_ch, tiles_per_seq):
    @pl.when(pl.program_id(0) == 0)
    def _():
        w_ref[...] = wf_ref[...].astype(BF16)

    t = x_ref.shape[0]
    n_conv = 3 * conv_ch
    g = g_ref[...]
    halo = jnp.concatenate([xprev_ref[...], xnext_ref[...]], axis=0)
    xa = jnp.concatenate([_rms(x_ref[...], g), _rms(halo, g)], axis=0).astype(BF16)
    chunk = 2 * LANES
    for c in range(0, w_ref.shape[1], chunk):
        if c >= n_conv:
            hq_ref[:, c - n_conv:c - n_conv + chunk] = jnp.dot(
                xa[0:t], w_ref[:, c:c + chunk], preferred_element_type=F32).astype(BF16)
            continue
        h = jnp.dot(xa, w_ref[:, c:c + chunk], preferred_element_type=F32)
        if c + chunk <= n_conv:
            hc[:, c:c + chunk] = h
        else:
            k = n_conv - c
            hc[:, c:n_conv] = h[:, :k]
            hq_ref[:, 0:chunk - k] = h[0:t, k:].astype(BF16)

    u_all = hc[:, 2 * conv_ch:3 * conv_ch] * hc[:, 0:conv_ch]
    u = u_all[0:t]
    pos = lax.rem(pl.program_id(0), tiles_per_seq)
    u_before = jnp.where(pos == 0, 0.0, u_all[t + HALO_ROWS - 1:t + HALO_ROWS])
    u_after = jnp.where(pos == tiles_per_seq - 1, 0.0, u_all[t + HALO_ROWS:t + HALO_ROWS + 1])
    row = lax.broadcasted_iota(jnp.int32, u.shape, 0)
    prev = jnp.where(row == 0, u_before, pltpu.roll(u, 1, 0))
    nxt = jnp.where(row == t - 1, u_after, pltpu.roll(u, t - 1, 0))
    w = cw_ref[...]
    conv = w[0:1] * prev + w[1:2] * u + w[2:3] * nxt
    conv_ref[...] = (hc[0:t, conv_ch:2 * conv_ch] * conv).astype(BF16)


def _inproj(x2, g, w, conv_w, conv_ch, seq):
    n, d = x2.shape
    n_all = w.shape[1]
    n_conv = 3 * conv_ch
    per_tile = ROW_TILE // HALO_ROWS
    last = n // HALO_ROWS - 1
    assert seq % ROW_TILE == 0
    return pl.pallas_call(
        functools.partial(_inproj_kernel, conv_ch=conv_ch, tiles_per_seq=seq // ROW_TILE),
        grid=(n // ROW_TILE,),
        in_specs=[pl.BlockSpec((ROW_TILE, d), lambda i: (i, 0)),
                  pl.BlockSpec((HALO_ROWS, d), lambda i: (jnp.maximum(i * per_tile - 1, 0), 0)),
                  pl.BlockSpec((HALO_ROWS, d), lambda i: (jnp.minimum((i + 1) * per_tile, last), 0)),
                  pl.BlockSpec((1, d), lambda i: (0, 0)),
                  pl.BlockSpec((d, n_all), lambda i: (0, 0)),
                  pl.BlockSpec(conv_w.shape, lambda i: (0, 0))],
        out_specs=[pl.BlockSpec((ROW_TILE, conv_ch), lambda i: (i, 0)),
                   pl.BlockSpec((ROW_TILE, n_all - n_conv), lambda i: (i, 0))],
        out_shape=[jax.ShapeDtypeStruct((n, conv_ch), BF16),
                   jax.ShapeDtypeStruct((n, n_all - n_conv), BF16)],
        scratch_shapes=[pltpu.VMEM((ROW_TILE + 2 * HALO_ROWS, n_conv), F32),
                        pltpu.VMEM((d, n_all), BF16)],
        compiler_params=_params("arbitrary"),
        name="inproj_conv",
    )(x2, x2, x2, g, w, conv_w)


DIL_TILE_UNROLL = 16


def _dil_cases(seq, dil):
    tps = (seq // dil) // ATT_TILE
    return tps, (("single",) if tps == 1 else ("first", "interior", "last"))


def _dil_kernel(slopes_ref, q_ref, k_ref, v_ref, o_ref,
                natq, natk, natv, perq, perk, perv, bias,
                num1, mx1, sum1, num2, mx2, sum2, num3, mx3, sum3, onat, *operands):
    seq = q_ref.shape[1]
    pad = HALF_WINDOW
    tq = ATT_TILE
    step = DILATED_PATTERNS[1][1]
    seg2 = seq // step
    seg3 = seg2 // step
    log_seg2 = seg2.bit_length() - 1
    log_step = step.bit_length() - 1
    pair = pl.program_id(0)
    lane = lax.broadcasted_iota(jnp.int32, (1, LANES), 1)
    head0 = lane < HEAD_DIM
    m0 = head0.astype(F32)
    m1 = 1.0 - m0

    @pl.when(pl.program_id(1) == 0)
    def _():
        for pi in range(len(DILATED_PATTERNS)):
            _, _, kp, vp = operands[4 * pi:4 * pi + 4]
            kp[0:pad, :] = jnp.zeros((pad, LANES), BF16)
            kp[pad + seq:pad + seq + pad, :] = jnp.zeros((pad, LANES), BF16)
            vp[0:pad, :] = jnp.zeros((pad, 2 * LANES), BF16)
            vp[pad + seq:pad + seq + pad, :] = jnp.zeros((pad, 2 * LANES), BF16)
            vp[pad:pad + seq, LANES:2 * LANES] = jnp.ones((seq, LANES), BF16)
        row = lax.broadcasted_iota(jnp.int32, (2 * tq, 2 * tq), 0)
        col = lax.broadcasted_iota(jnp.int32, (2 * tq, 2 * tq), 1)
        kc = col - pad
        arel = jnp.abs(kc - (row & (tq - 1)))
        band = arel <= HALF_WINDOW
        slope_rows = jnp.where(row < tq, slopes_ref[2 * pair], slopes_ref[2 * pair + 1])
        idx = 0
        for _, dil in DILATED_PATTERNS:
            dist_bias = -slope_rows * (arel * dil).astype(F32)
            for case in _dil_cases(seq, dil)[1]:
                valid = band
                if case in ("first", "single"):
                    valid = valid & (kc >= 0)
                if case in ("last", "single"):
                    valid = valid & (kc < tq)
                bias[idx] = jnp.where(valid, dist_bias, NEG_INF)
                idx += 1

    natq[...] = q_ref[0].astype(F32) * (HEAD_DIM ** -0.5)
    natk[...] = k_ref[0].astype(F32)
    natv[...] = v_ref[0].astype(F32)

    def set_operands(pi, rows, qv, kv, vv):
        qp0, qp1, kp, vp = operands[4 * pi:4 * pi + 4]
        lo, hi = rows
        qp0[lo:hi, :] = (qv * m0).astype(BF16)
        qp1[lo:hi, :] = (qv * m1).astype(BF16)
        kp[pad + lo:pad + hi, :] = kv.astype(BF16)
        vp[pad + lo:pad + hi, 0:LANES] = vv.astype(BF16)

    def run_tiles(pi, case0, tps, dests):
        num_ref, mx_ref, sum_ref = dests
        qp0, qp1, kp, vp = operands[4 * pi:4 * pi + 4]

        def tile(m, carry):
            r0 = pl.multiple_of(m * tq, tq)
            qc = jnp.concatenate([qp0[pl.ds(r0, tq), :], qp1[pl.ds(r0, tq), :]], axis=0)
            kt = kp[pl.ds(r0, 2 * tq), :]
            vt = vp[pl.ds(r0, 2 * tq), :]
            if tps == 1:
                case = case0
                dst = pl.ds((m >> log_step) * seg2 + (m & (step - 1)), tq, stride=step)
            else:
                pos = m & (tps - 1)
                case = case0 + jnp.where(pos == 0, 0, jnp.where(pos == tps - 1, 2, 1))
                dst = pl.ds(r0, tq)
            s = _nt_dot(qc, kt) + bias[case]
            mx = jnp.max(s, axis=-1, keepdims=True)
            pe = jnp.exp(s - mx)
            pv = jnp.dot(pe.astype(BF16), vt, preferred_element_type=F32)
            num_ref[dst, :] = jnp.where(head0, pv[0:tq, 0:LANES], pv[tq:2 * tq, 0:LANES])
            sum_ref[dst, :] = jnp.where(head0, pv[0:tq, LANES:2 * LANES],
                                        pv[tq:2 * tq, LANES:2 * LANES])
            mx_ref[dst, :] = jnp.where(head0, mx[0:tq], mx[tq:2 * tq])
            return carry

        lax.fori_loop(0, seq // tq, tile, 0, unroll=DIL_TILE_UNROLL)

    case0 = 0
    set_operands(0, (0, seq), natq[...], natk[...], natv[...])
    tps1, cases1 = _dil_cases(seq, DILATED_PATTERNS[0][1])
    run_tiles(0, case0, tps1, (num1, mx1, sum1))
    case0 += len(cases1)

    for b in range(step):
        rows = (b * seg2, (b + 1) * seg2)
        qv = natq[pl.ds(b, seg2, stride=step), :]
        kv = natk[pl.ds(b, seg2, stride=step), :]
        vv = natv[pl.ds(b, seg2, stride=step), :]
        perq[rows[0]:rows[1], :] = qv
        perk[rows[0]:rows[1], :] = kv
        perv[rows[0]:rows[1], :] = vv
        set_operands(1, rows, qv, kv, vv)
    tps2, cases2 = _dil_cases(seq, DILATED_PATTERNS[1][1])
    run_tiles(1, case0, tps2, (num2, mx2, sum2))
    case0 += len(cases2)

    for sgm in range(step * step):
        b, a = divmod(sgm, step)
        src = pl.ds(b * seg2 + a, seg3, stride=step)
        set_operands(2, (sgm * seg3, (sgm + 1) * seg3), perq[src, :], perk[src, :], perv[src, :])
    tps3, _ = _dil_cases(seq, DILATED_PATTERNS[2][1])
    assert tps3 == 1 and seg3 == tq
    run_tiles(2, case0, tps3, (num3, mx3, sum3))

    def merge(c, carry):
        p0 = pl.multiple_of(c * tq, tq)
        per_rows = pl.ds(p0, tq)
        nat_rows = pl.ds((p0 & (seg2 - 1)) * step + (p0 >> log_seg2), tq, stride=step)
        a1, a2, a3 = mx1[nat_rows, :], mx2[per_rows, :], mx3[per_rows, :]
        top = jnp.maximum(jnp.maximum(a1, a2), a3)
        e1, e2, e3 = jnp.exp(a1 - top), jnp.exp(a2 - top), jnp.exp(a3 - top)
        den = e1 * sum1[nat_rows, :] + e2 * sum2[per_rows, :] + e3 * sum3[per_rows, :]
        num = e1 * num1[nat_rows, :] + e2 * num2[per_rows, :] + e3 * num3[per_rows, :]
        onat[nat_rows, :] = num / den
        return carry

    lax.fori_loop(0, seq // tq, merge, 0, unroll=2)
    o_ref[0] = onat[...].astype(BF16)


def _dilated(hq3, slopes, dil_width):
    b, s, _ = hq3.shape
    npair = dil_width // LANES
    blk = (1, s, LANES)
    dils = [d for _, d in DILATED_PATTERNS]
    assert dils[0] == 1 and dils[2] == dils[1] * dils[1]
    n_cases = sum(len(_dil_cases(s, d)[1]) for d in dils)
    pad = HALF_WINDOW
    scratch = [pltpu.VMEM((s, LANES), F32)] * 6
    scratch += [pltpu.VMEM((n_cases, 2 * ATT_TILE, 2 * ATT_TILE), F32)]
    scratch += [pltpu.VMEM((s, LANES), F32)] * 10
    scratch += [pltpu.VMEM((s, LANES), BF16), pltpu.VMEM((s, LANES), BF16),
                pltpu.VMEM((s + 2 * pad, LANES), BF16),
                pltpu.VMEM((s + 2 * pad, 2 * LANES), BF16)] * len(dils)
    return pl.pallas_call(
        _dil_kernel,
        grid_spec=pltpu.PrefetchScalarGridSpec(
            num_scalar_prefetch=1,
            grid=(npair, b),
            in_specs=[pl.BlockSpec(blk, lambda j, i, sl: (i, 0, j)),
                      pl.BlockSpec(blk, lambda j, i, sl: (i, 0, npair + j)),
                      pl.BlockSpec(blk, lambda j, i, sl: (i, 0, 2 * npair + j))],
            out_specs=pl.BlockSpec(blk, lambda j, i, sl: (i, 0, j)),
            scratch_shapes=scratch),
        out_shape=jax.ShapeDtypeStruct((b, s, dil_width), BF16),
        compiler_params=_params("arbitrary", "arbitrary"),
        name="dilated_attn",
    )(slopes, hq3, hq3, hq3)


def _memkv_kernel(m_ref, g_ref, w_ref, o_ref):
    mn = _rms(m_ref[...], g_ref[...]).astype(BF16)
    o_ref[...] = jnp.dot(mn, w_ref[...], preferred_element_type=F32).astype(BF16)


def _memkv(mem2, g, w):
    n, d = mem2.shape
    nw = w.shape[1]
    return pl.pallas_call(
        _memkv_kernel,
        grid=(n // ROW_TILE,),
        in_specs=[pl.BlockSpec((ROW_TILE, d), lambda i: (i, 0)),
                  pl.BlockSpec((1, d), lambda i: (0, 0)),
                  pl.BlockSpec((d, nw), lambda i: (0, 0))],
        out_specs=pl.BlockSpec((ROW_TILE, nw), lambda i: (i, 0)),
        out_shape=jax.ShapeDtypeStruct((n, nw), BF16),
        compiler_params=_params("arbitrary"),
        name="mem_kv",
    )(mem2, g, w)


def _memattn_kernel(q_ref, km_ref, vm_ref, o_ref, qp0, qp1, vext):
    seq = q_ref.shape[1]
    tq = 2 * ATT_TILE
    lane = lax.broadcasted_iota(jnp.int32, (1, LANES), 1)
    head0 = lane < HEAD_DIM
    m0 = head0.astype(F32)
    q = q_ref[0].astype(F32) * (HEAD_DIM ** -0.5)
    qp0[...] = (q * m0).astype(BF16)
    qp1[...] = (q * (1.0 - m0)).astype(BF16)
    vext[:, 0:LANES] = vm_ref[0]
    vext[:, LANES:2 * LANES] = jnp.ones((vext.shape[0], LANES), BF16)
    km = km_ref[0]

    def tile(m, carry):
        rows = pl.ds(pl.multiple_of(m * tq, tq), tq)
        qc = jnp.concatenate([qp0[rows, :], qp1[rows, :]], axis=0)
        s = _nt_dot(qc, km)
        pe = jnp.exp(s - jnp.max(s, axis=-1, keepdims=True))
        pv = jnp.dot(pe.astype(BF16), vext[...], preferred_element_type=F32)
        o = pv[:, 0:LANES] / pv[:, LANES:2 * LANES]
        o_ref[0, rows, :] = jnp.where(head0, o[0:tq], o[tq:2 * tq]).astype(BF16)
        return carry

    lax.fori_loop(0, seq // tq, tile, 0, unroll=True)


def _memattn(hq3, kvm3, layer, qm_block0, mem_width):
    b, s, _ = hq3.shape
    m = kvm3.shape[1]
    npair = mem_width // LANES
    kv_blocks = 2 * npair
    return pl.pallas_call(
        _memattn_kernel,
        grid=(b, npair),
        in_specs=[pl.BlockSpec((1, s, LANES), lambda i, j: (i, 0, qm_block0 + j)),
                  pl.BlockSpec((1, m, LANES), lambda i, j: (i, 0, layer * kv_blocks + j)),
                  pl.BlockSpec((1, m, LANES), lambda i, j: (i, 0, layer * kv_blocks + npair + j))],
        out_specs=pl.BlockSpec((1, s, LANES), lambda i, j: (i, 0, j)),
        out_shape=jax.ShapeDtypeStruct((b, s, mem_width), BF16),
        scratch_shapes=[pltpu.VMEM((s, LANES), BF16), pltpu.VMEM((s, LANES), BF16),
                        pltpu.VMEM((m, 2 * LANES), BF16)],
        compiler_params=_params("arbitrary", "arbitrary"),
        name="mem_attn",
    )(hq3, kvm3, kvm3)


OUT_SUB_ROWS = 256
TOKEN_TILE_ROWS = 8


def _outproj_kernel(x_ref, c_ref, d_ref, m_ref, wf_ref, g_ref, wr_ref,
                    x1_ref, xn_ref, aff_ref, w_ref, *, n_exp):
    @pl.when(pl.program_id(0) == 0)
    def _():
        w_ref[...] = wf_ref[...].astype(BF16)

    valid = lax.broadcasted_iota(jnp.int32, (1, LANES), 1) < n_exp
    for r0 in range(0, x_ref.shape[0], OUT_SUB_ROWS):
        rs = slice(r0, r0 + OUT_SUB_ROWS)
        cat = jnp.concatenate([c_ref[rs, :], d_ref[rs, :], m_ref[rs, :]], axis=1)
        x1 = x_ref[rs, :] + jnp.dot(cat, w_ref[...], preferred_element_type=F32)
        x1_ref[rs, :] = x1
        xn = _rms(x1, g_ref[...])
        for j in range(xn.shape[1] // LANES):
            xn_ref[pl.ds(r0 * TOKEN_TILE_ROWS + j, OUT_SUB_ROWS, stride=TOKEN_TILE_ROWS), :] = (
                xn[:, j * LANES:(j + 1) * LANES])
        xh = xn.astype(BF16)
        xl = (xn - xh.astype(F32)).astype(BF16)
        both = jnp.dot(xh, wr_ref[...], preferred_element_type=F32)
        logits = (both[:, 0:LANES] + both[:, LANES:2 * LANES]
                  + jnp.dot(xl, wr_ref[:, 0:LANES], preferred_element_type=F32))
        logits = jnp.where(valid, logits, NEG_INF)
        ex = jnp.exp(logits - jnp.max(logits, axis=1, keepdims=True))
        aff_ref[rs, :] = ex / jnp.sum(ex, axis=1, keepdims=True)


def _outproj(x2, conv_o, dil_o, mem_o, w, g, wr_pieces, n_exp):
    n, d = x2.shape
    assert d == TOKEN_TILE_ROWS * LANES
    row = lambda i: (i, 0)
    fixed = lambda i: (0, 0)
    return pl.pallas_call(
        functools.partial(_outproj_kernel, n_exp=n_exp),
        grid=(n // ROW_TILE,),
        in_specs=[pl.BlockSpec((ROW_TILE, d), row),
                  pl.BlockSpec((ROW_TILE, conv_o.shape[1]), row),
                  pl.BlockSpec((ROW_TILE, dil_o.shape[1]), row),
                  pl.BlockSpec((ROW_TILE, mem_o.shape[1]), row),
                  pl.BlockSpec(w.shape, fixed),
                  pl.BlockSpec((1, d), fixed),
                  pl.BlockSpec((d, 2 * LANES), fixed)],
        out_specs=[pl.BlockSpec((ROW_TILE, d), row),
                   pl.BlockSpec((ROW_TILE * TOKEN_TILE_ROWS, LANES), row),
                   pl.BlockSpec((ROW_TILE, LANES), row)],
        out_shape=[jax.ShapeDtypeStruct((n, d), F32),
                   jax.ShapeDtypeStruct((n * TOKEN_TILE_ROWS, LANES), F32),
                   jax.ShapeDtypeStruct((n, LANES), F32)],
        scratch_shapes=[pltpu.VMEM(w.shape, BF16)],
        compiler_params=_params("arbitrary"),
        name="outproj_router",
    )(x2, conv_o, dil_o, mem_o, w, g, wr_pieces)


TOK_DIGIT_BITS = 6
PICK_GATE_LANE0 = 2


def _fill_strict_upper(tri):
    s = tri.shape[0]
    chunk = 2 * LANES
    for c in range(0, s, chunk):
        src = lax.broadcasted_iota(jnp.int32, (chunk, s), 0) + c
        dst = lax.broadcasted_iota(jnp.int32, (chunk, s), 1)
        tri[c:c + chunk, :] = jnp.where(src < dst, 1.0, 0.0).astype(BF16)


def _topk_kernel(aff_ref, afft_ref, slot_ref, pick_ref, idx_ref, rhs, tri, *, cap, n_batch, n_exp):
    aff = aff_ref[...]
    rows, seq = aff.shape
    _fill_strict_upper(tri)
    bits = jnp.zeros((rows, 1), jnp.int32)
    for bit in range(30, -1, -1):
        cand = bits | (1 << bit)
        cnt = jnp.sum(jnp.where(aff >= pltpu.bitcast(cand, F32), 1.0, 0.0), axis=1, keepdims=True)
        bits = jnp.where(cnt >= cap, cand, bits)
    thr = pltpu.bitcast(bits, F32)
    gt = aff > thr
    eq = aff == thr
    need = cap - jnp.sum(jnp.where(gt, 1.0, 0.0), axis=1, keepdims=True)
    rank_eq = jnp.dot(jnp.where(eq, 1.0, 0.0).astype(BF16), tri[...], preferred_element_type=F32)
    sel = jnp.where(gt, 1.0, jnp.where(eq & (rank_eq < need), 1.0, 0.0))
    rank = jnp.dot(sel.astype(BF16), tri[...], preferred_element_type=F32)
    slot_ref[...] = jnp.where(sel > 0.5, rank, -1.0).astype(jnp.int32)

    half = seq // 2
    lane = lax.broadcasted_iota(jnp.int32, (1, LANES), 1)
    tok = lax.broadcasted_iota(jnp.int32, (seq, LANES), 0)
    tok_part = jnp.where(lane == 0, tok >> TOK_DIGIT_BITS,
                         jnp.where(lane == 1, tok & ((1 << TOK_DIGIT_BITS) - 1),
                                   jnp.where(lane == LANES - 1, 1, 0))).astype(F32)
    src = lax.broadcasted_iota(jnp.int32, (LANES, LANES), 0)
    dst = lax.broadcasted_iota(jnp.int32, (LANES, LANES), 1)

    def place(piece):
        return jnp.where((src < n_exp) & (dst == src + PICK_GATE_LANE0 + piece * n_exp),
                         1.0, 0.0).astype(BF16)

    for bi in range(n_batch):
        a = afft_ref[bi * seq:(bi + 1) * seq, :]
        a_hi = a.astype(BF16)
        a_lo = (a - a_hi.astype(F32)).astype(BF16)
        full = (tok_part
                + jnp.dot(a_hi, place(0), preferred_element_type=F32)
                + jnp.dot(a_lo, place(1), preferred_element_type=F32)).astype(BF16)
        rhs[bi, :, 0:LANES] = full[0:half]
        rhs[bi, :, LANES:2 * LANES] = full[half:seq]

    sidx = lax.broadcasted_iota(jnp.int32, (cap, half), 0)

    def invert(r, carry):
        bi = lax.rem(r, n_batch)
        srow = slot_ref[pl.ds(r, 1), :]
        onehot = jnp.where(srow[:, 0:half] == sidx, 1.0,
                           jnp.where(srow[:, half:seq] == sidx, 2.0, 0.0)).astype(BF16)
        both = jnp.dot(onehot, rhs[bi], preferred_element_type=F32)
        lower = both[:, LANES - 1:LANES] < 1.5
        pick = jnp.where(lower, both[:, 0:LANES], both[:, LANES:2 * LANES] * 0.5)
        pick_ref[pl.ds(pl.multiple_of(r * cap, cap), cap), :] = pick
        digits = jnp.transpose(pick)
        token = (digits[0:1] * (1 << TOK_DIGIT_BITS) + digits[1:2]).astype(jnp.int32)
        idx_ref[pl.ds(r, 1), :] = (token + bi * seq) * TOKEN_TILE_ROWS
        return carry

    lax.fori_loop(0, rows, invert, 0, unroll=2)


def _topk(aff_rows, aff_tok, cap, n_batch, n_exp):
    rows, seq = aff_rows.shape
    return pl.pallas_call(
        functools.partial(_topk_kernel, cap=cap, n_batch=n_batch, n_exp=n_exp),
        out_shape=[jax.ShapeDtypeStruct(aff_rows.shape, jnp.int32),
                   jax.ShapeDtypeStruct((rows * cap, LANES), F32),
                   jax.ShapeDtypeStruct((rows, cap), jnp.int32)],
        scratch_shapes=[pltpu.VMEM((n_batch, seq // 2, 2 * LANES), BF16),
                        pltpu.VMEM((seq, seq), BF16)],
        compiler_params=pltpu.CompilerParams(vmem_limit_bytes=V7X_VMEM_LIMIT_BYTES),
        name="expert_topk",
    )(aff_rows, aff_tok)


FF_CHUNK = 512
EXPERT_ROWS = 512


def _expert_kernel(idx_ref, xn_hbm, wg_ref, wu_ref, wd_ref, pick_ref, y_ref,
                   xbuf, sem, acc, wgb, wub, wdb, *, cap, n_f):
    e = pl.program_id(0)
    f = pl.program_id(1)
    n_exp = pl.num_programs(0)
    rows = acc.shape[0]
    slot = lax.rem(e, 2)
    nxt = 1 - slot
    e_nxt = jnp.minimum(e + 1, n_exp - 1)

    tile = TOKEN_TILE_ROWS

    def row_copy(src_tile_row, buf, dst_row):
        return pltpu.make_async_copy(
            xn_hbm.at[pl.ds(pl.multiple_of(src_tile_row, tile), tile), :],
            xbuf.at[pl.ds(pl.multiple_of((buf * rows + dst_row) * tile, tile), tile), :], sem.at[buf])

    def wait_rows(buf):
        pltpu.make_async_copy(xn_hbm.at[pl.ds(0, rows * tile), :],
                              xbuf.at[pl.ds(pl.multiple_of(buf * rows * tile, tile), rows * tile), :],
                              sem.at[buf]).wait()

    @pl.when((e == 0) & (f == 0))
    def _():
        def first(j, carry):
            row_copy(idx_ref[j], 0, j).start()
            return carry
        lax.fori_loop(0, rows, first, 0, unroll=8)

    @pl.when(f == 0)
    def _():
        wait_rows(slot)
        acc[...] = jnp.zeros(acc.shape, F32)

    wgb[...] = wg_ref[0, 0].astype(BF16)
    wub[...] = wu_ref[0, 0].astype(BF16)
    wdb[...] = wd_ref[0, 0].astype(BF16)

    per_step = rows // n_f
    per_chunk = per_step // (rows // EXPERT_ROWS)
    base = pl.multiple_of(f * per_step, per_step)
    idx_base = e_nxt * rows + base
    for ci, r0 in enumerate(range(0, rows, EXPERT_ROWS)):
        for k in range(ci * per_chunk, (ci + 1) * per_chunk):
            row_copy(idx_ref[idx_base + k], nxt, base + k).start()
        rsl = slice(r0, r0 + EXPERT_ROWS)
        xm = jnp.concatenate(
            [xbuf[pl.ds((slot * rows + r0) * tile + j, EXPERT_ROWS, stride=tile), :]
             for j in range(tile)], axis=1).astype(BF16)
        gate = jnp.dot(xm, wgb[...], preferred_element_type=F32)
        up = jnp.dot(xm, wub[...], preferred_element_type=F32)
        hid = (gate * (1.0 / (1.0 + jnp.exp(-gate))) * up).astype(BF16)
        acc[rsl, :] += jnp.dot(hid, wdb[...], preferred_element_type=F32)

    @pl.when(f == n_f - 1)
    def _():
        lane = lax.broadcasted_iota(jnp.int32, (1, LANES), 1) - (PICK_GATE_LANE0 + e)
        mine = (lane == 0) | (lane == n_exp)
        for r0 in range(0, rows, EXPERT_ROWS):
            rsl = slice(r0, r0 + EXPERT_ROWS)
            gate_col = jnp.sum(jnp.where(mine, pick_ref[rsl, :], 0.0), axis=1, keepdims=True)
            y_ref[0, rsl, :] = (acc[rsl, :] * gate_col).astype(BF16)

    @pl.when((e == n_exp - 1) & (f == n_f - 1))
    def _():
        wait_rows(nxt)


def _experts(idx2, xn_tiles, w_gate, w_up, w_down, layer, picked, cap):
    n_exp, rows = idx2.shape
    d = w_gate.shape[2]
    d_ff = w_gate.shape[3]
    n_f = d_ff // FF_CHUNK
    assert rows % (n_f * (rows // EXPERT_ROWS)) == 0
    return pl.pallas_call(
        functools.partial(_expert_kernel, cap=cap, n_f=n_f),
        grid_spec=pltpu.PrefetchScalarGridSpec(
            num_scalar_prefetch=1,
            grid=(n_exp, n_f),
            in_specs=[pl.BlockSpec(memory_space=pl.ANY),
                      pl.BlockSpec((1, 1, d, FF_CHUNK), lambda e, f, ix: (layer, e, 0, f)),
                      pl.BlockSpec((1, 1, d, FF_CHUNK), lambda e, f, ix: (layer, e, 0, f)),
                      pl.BlockSpec((1, 1, FF_CHUNK, d), lambda e, f, ix: (layer, e, f, 0)),
                      pl.BlockSpec((rows, LANES), lambda e, f, ix: (e, 0))],
            out_specs=pl.BlockSpec((1, rows, d), lambda e, f, ix: (e, 0, 0)),
            scratch_shapes=[pltpu.VMEM((2 * rows * TOKEN_TILE_ROWS, LANES), F32),
                            pltpu.SemaphoreType.DMA((2,)),
                            pltpu.VMEM((rows, d), F32),
                            pltpu.VMEM((d, FF_CHUNK), BF16),
                            pltpu.VMEM((d, FF_CHUNK), BF16),
                            pltpu.VMEM((FF_CHUNK, d), BF16)]),
        out_shape=jax.ShapeDtypeStruct((n_exp, rows, d), BF16),
        compiler_params=_params("arbitrary", "arbitrary"),
        name="expert_ffn",
    )(idx2.reshape(-1), xn_tiles, w_gate, w_up, w_down, picked)


COMBINE_GROUP = 4
COMBINE_SUB = 256
COMBINE_WINDOW = 64
BF16_ROWS = 16


def _combine_kernel(win_ref, fits_ref, x_ref, slott_ref, y_ref, g_ref, o_ref, *, cap, final):
    tile_id = pl.program_id(0) * pl.num_programs(1) + pl.program_id(1)
    t, n_exp = slott_ref.shape[1:]
    d = x_ref.shape[2]
    n_sub = t // COMBINE_SUB

    def finish(rows, acc):
        if final:
            acc = _rms(acc, g_ref[...])
        o_ref[0, rows, :] = acc

    @pl.when(fits_ref[tile_id] != 0)
    def _():
        lane = lax.broadcasted_iota(jnp.int32, (COMBINE_SUB, COMBINE_WINDOW), 1)
        for si in range(n_sub):
            rows = slice(si * COMBINE_SUB, (si + 1) * COMBINE_SUB)
            slott = slott_ref[0, rows, :]
            acc = x_ref[0, rows, :]
            for e0 in range(0, n_exp, COMBINE_GROUP):
                parts, ys = [], []
                for e in range(e0, e0 + COMBINE_GROUP):
                    start = pl.multiple_of(win_ref[(tile_id * n_sub + si) * n_exp + e], BF16_ROWS)
                    parts.append(
                        jnp.where(slott[:, e:e + 1] - start == lane, 1.0, 0.0).astype(BF16))
                    ys.append(y_ref[e, 0, pl.ds(start, COMBINE_WINDOW), :])
                acc = acc + jnp.dot(jnp.concatenate(parts, axis=1), jnp.concatenate(ys, axis=0),
                                    preferred_element_type=F32)
            finish(rows, acc)

    @pl.when(fits_ref[tile_id] == 0)
    def _():
        slott = slott_ref[0]
        sidx = lax.broadcasted_iota(jnp.int32, (t, cap), 1)
        acc = x_ref[0]
        for e0 in range(0, n_exp, COMBINE_GROUP):
            parts = [jnp.where(slott[:, e:e + 1] == sidx, 1.0, 0.0).astype(BF16)
                     for e in range(e0, e0 + COMBINE_GROUP)]
            yg = y_ref[e0:e0 + COMBINE_GROUP, 0].reshape(COMBINE_GROUP * cap, d)
            acc = acc + jnp.dot(jnp.concatenate(parts, axis=1), yg, preferred_element_type=F32)
        finish(slice(0, t), acc)


def _combine_windows(slot_rows, n_exp, b, s, cap):
    nsub = s // COMBINE_SUB
    cnt = jnp.sum((slot_rows >= 0).reshape(n_exp, b, nsub, COMBINE_SUB), axis=-1, dtype=jnp.int32)
    end = jnp.cumsum(cnt, axis=-1)
    start = jnp.minimum((end - cnt) // BF16_ROWS * BF16_ROWS, cap - COMBINE_WINDOW)
    fits = jnp.all(end - start <= COMBINE_WINDOW, axis=0)
    fits = jnp.all(fits.reshape(b, s // ROW_TILE, ROW_TILE // COMBINE_SUB), axis=-1)
    return jnp.transpose(start, (1, 2, 0)).reshape(-1), fits.reshape(-1).astype(jnp.int32)


def _combine(x3, slott, y4, g, win, fits, cap, final):
    b, s, d = x3.shape
    n_exp = y4.shape[0]
    assert cap >= COMBINE_WINDOW and cap % BF16_ROWS == 0 and ROW_TILE % COMBINE_SUB == 0
    return pl.pallas_call(
        functools.partial(_combine_kernel, cap=cap, final=final),
        grid_spec=pltpu.PrefetchScalarGridSpec(
            num_scalar_prefetch=2,
            grid=(b, s // ROW_TILE),
            in_specs=[pl.BlockSpec((1, ROW_TILE, d), lambda i, t, w, f: (i, t, 0)),
                      pl.BlockSpec((1, ROW_TILE, n_exp), lambda i, t, w, f: (i, t, 0)),
                      pl.BlockSpec((n_exp, 1, cap, d), lambda i, t, w, f: (0, i, 0, 0)),
                      pl.BlockSpec((1, d), lambda i, t, w, f: (0, 0))],
            out_specs=pl.BlockSpec((1, ROW_TILE, d), lambda i, t, w, f: (i, t, 0))),
        out_shape=jax.ShapeDtypeStruct((b, s, d), F32),
        compiler_params=_params("arbitrary", "arbitrary"),
        name="combine",
    )(win, fits, x3, slott, y4, g)


def kernel(x, mem, mem_norm, norm_mix, w_in, conv_w, w_mem_kv, w_out, norm_ffn,
           w_router, w_gate, w_up, w_down, norm_final):
    b, s, d = x.shape
    depth = w_in.shape[0]
    n_exp = w_router.shape[2]
    conv_ch = conv_w.shape[2]
    mem_width = w_mem_kv.shape[2] // 2
    dil_width = (w_in.shape[2] - 3 * conv_ch - mem_width) // 3
    n_dil_heads = dil_width // HEAD_DIM
    cap = EC_CAPACITY * s // n_exp
    assert n_exp == N_EXPERTS and mem_width == N_MEM_HEADS * HEAD_DIM
    assert all(w // (2 * dl) == HALF_WINDOW for w, dl in DILATED_PATTERNS)

    slopes = jnp.asarray([2.0 ** (-8.0 * (h + 1) / n_dil_heads) for h in range(n_dil_heads)], F32)
    kv_all = jnp.transpose(w_mem_kv, (1, 0, 2)).reshape(d, depth * 2 * mem_width).astype(BF16)
    kvm3 = _memkv(mem.reshape(-1, d), mem_norm.reshape(1, d), kv_all).reshape(b, mem.shape[1], -1)

    x2 = x.reshape(b * s, d)
    out = None
    for l in range(depth):
        conv_o, hq = _inproj(x2, norm_mix[l].reshape(1, d), w_in[l], conv_w[l],
                             conv_ch, s)
        hq3 = hq.reshape(b, s, -1)
        dil_o = _dilated(hq3, slopes, dil_width)
        mem_o = _memattn(hq3, kvm3, l, 3 * dil_width // LANES, mem_width)
        wr = jnp.pad(w_router[l], ((0, 0), (0, LANES - n_exp)))
        wr_hi = wr.astype(BF16)
        wr_pieces = jnp.concatenate([wr_hi, (wr - wr_hi.astype(F32)).astype(BF16)], axis=1)
        x1, xn, aff_tok = _outproj(x2, conv_o, dil_o.reshape(b * s, -1),
                                   mem_o.reshape(b * s, -1), w_out[l],
                                   norm_ffn[l].reshape(1, d), wr_pieces, n_exp)
        aff_rows = jnp.transpose(aff_tok[:, :n_exp]).reshape(n_exp * b, s)
        slot_rows, picked, idx_rows = _topk(aff_rows, aff_tok, cap, b, n_exp)
        y = _experts(idx_rows.reshape(n_exp, b * cap), xn, w_gate, w_up, w_down, l, picked, cap)
        slott = jnp.transpose(slot_rows.reshape(n_exp, b, s), (1, 2, 0))
        win, fits = _combine_windows(slot_rows, n_exp, b, s, cap)
        final = l == depth - 1
        x3 = _combine(x1.reshape(b, s, d), slott, y.reshape(n_exp, b, cap, d),
                      norm_final.reshape(1, d), win, fits, cap, final)
        x2 = x3.reshape(b * s, d)
        out = x3
    return out
```

```python
import functools

import jax
import jax.numpy as jnp
from jax import lax
from jax.experimental import pallas as pl
from jax.experimental.pallas import tpu as pltpu

F32 = jnp.float32
BF16 = jnp.bfloat16

HEAD_DIM = 64
N_MEM_HEADS = 4
DILATED_PATTERNS = ((128, 1), (512, 4), (2048, 16))
HALF_WINDOW = 64
N_EXPERTS = 16
EC_CAPACITY = 2
RMS_EPS = 1e-6
NEG_INF = -1e30

LANES = 128
ROW_TILE = 512
PROJ_ROW_TILE = 1024
ATT_TILE = 128
V7X_VMEM_LIMIT_BYTES = 56 * 1024 * 1024


def _params(*semantics):
    return pltpu.CompilerParams(dimension_semantics=semantics,
                                vmem_limit_bytes=V7X_VMEM_LIMIT_BYTES)


def _rms(x, g):
    ms = jnp.mean(x * x, axis=-1, keepdims=True)
    return x * lax.rsqrt(ms + RMS_EPS) * g


def _nt_dot(a, b, **kw):
    return lax.dot_general(a, b, (((1,), (1,)), ((), ())), preferred_element_type=F32, **kw)


HALO_ROWS = 8


def _inproj_kernel(x_ref, xprev_ref, xnext_ref, g_ref, wf_ref, cw_ref, conv_ref, hq_ref, hc, w_ref,
                   *, conv_ch, tiles_per_seq):
    @pl.when(pl.program_id(0) == 0)
    def _():
        w_ref[...] = wf_ref[0].astype(BF16)

    t = x_ref.shape[0]
    n_conv = 3 * conv_ch
    g = g_ref[...]
    halo = jnp.concatenate([xprev_ref[...], xnext_ref[...]], axis=0)
    xa = jnp.concatenate([_rms(x_ref[...], g), _rms(halo, g)], axis=0).astype(BF16)
    chunk = 2 * LANES
    for c in range(0, w_ref.shape[1], chunk):
        if c >= n_conv:
            hq_ref[:, c - n_conv:c - n_conv + chunk] = jnp.dot(
                xa[0:t], w_ref[:, c:c + chunk], preferred_element_type=F32).astype(BF16)
            continue
        h = jnp.dot(xa, w_ref[:, c:c + chunk], preferred_element_type=F32)
        if c + chunk <= n_conv:
            hc[:, c:c + chunk] = h
        else:
            k = n_conv - c
            hc[:, c:n_conv] = h[:, :k]
            hq_ref[:, 0:chunk - k] = h[0:t, k:].astype(BF16)

    u_all = hc[:, 2 * conv_ch:3 * conv_ch] * hc[:, 0:conv_ch]
    u = u_all[0:t]
    pos = lax.rem(pl.program_id(0), tiles_per_seq)
    u_before = jnp.where(pos == 0, 0.0, u_all[t + HALO_ROWS - 1:t + HALO_ROWS])
    u_after = jnp.where(pos == tiles_per_seq - 1, 0.0, u_all[t + HALO_ROWS:t + HALO_ROWS + 1])
    row = lax.broadcasted_iota(jnp.int32, u.shape, 0)
    prev = jnp.where(row == 0, u_before, pltpu.roll(u, 1, 0))
    nxt = jnp.where(row == t - 1, u_after, pltpu.roll(u, t - 1, 0))
    w = cw_ref[...]
    conv = w[0:1] * prev + w[1:2] * u + w[2:3] * nxt
    conv_ref[...] = (hc[0:t, conv_ch:2 * conv_ch] * conv).astype(BF16)


def _inproj(x2, g, w_all, layer, conv_w, conv_ch, seq):
    n, d = x2.shape
    n_all = w_all.shape[2]
    n_conv = 3 * conv_ch
    per_tile = PROJ_ROW_TILE // HALO_ROWS
    last = n // HALO_ROWS - 1
    assert seq % PROJ_ROW_TILE == 0
    return pl.pallas_call(
        functools.partial(_inproj_kernel, conv_ch=conv_ch, tiles_per_seq=seq // PROJ_ROW_TILE),
        grid=(n // PROJ_ROW_TILE,),
        in_specs=[pl.BlockSpec((PROJ_ROW_TILE, d), lambda i: (i, 0)),
                  pl.BlockSpec((HALO_ROWS, d), lambda i: (jnp.maximum(i * per_tile - 1, 0), 0)),
                  pl.BlockSpec((HALO_ROWS, d), lambda i: (jnp.minimum((i + 1) * per_tile, last), 0)),
                  pl.BlockSpec((1, d), lambda i: (0, 0)),
                  pl.BlockSpec((1, d, n_all), lambda i: (layer, 0, 0)),
                  pl.BlockSpec(conv_w.shape, lambda i: (0, 0))],
        out_specs=[pl.BlockSpec((PROJ_ROW_TILE, conv_ch), lambda i: (i, 0)),
                   pl.BlockSpec((PROJ_ROW_TILE, n_all - n_conv), lambda i: (i, 0))],
        out_shape=[jax.ShapeDtypeStruct((n, conv_ch), BF16),
                   jax.ShapeDtypeStruct((n, n_all - n_conv), BF16)],
        scratch_shapes=[pltpu.VMEM((PROJ_ROW_TILE + 2 * HALO_ROWS, n_conv), F32),
                        pltpu.VMEM((d, n_all), BF16)],
        compiler_params=_params("arbitrary"),
        name="inproj_conv",
    )(x2, x2, x2, g, w_all, conv_w)


DIL_TILE_UNROLL = 16


def _dil_cases(seq, dil):
    tps = (seq // dil) // ATT_TILE
    return tps, (("single",) if tps == 1 else ("first", "interior", "last"))


def _dil_kernel(slopes_ref, q_ref, k_ref, v_ref, o_ref,
                natq, natk, natv, perq, perk, perv, bias,
                num1, mx1, sum1, num2, mx2, sum2, num3, mx3, sum3, onat, *operands):
    seq = q_ref.shape[1]
    pad = HALF_WINDOW
    tq = ATT_TILE
    step = DILATED_PATTERNS[1][1]
    seg2 = seq // step
    seg3 = seg2 // step
    log_seg2 = seg2.bit_length() - 1
    log_step = step.bit_length() - 1
    pair = pl.program_id(0)
    lane = lax.broadcasted_iota(jnp.int32, (1, LANES), 1)
    head0 = lane < HEAD_DIM
    m0 = head0.astype(F32)
    m1 = 1.0 - m0

    @pl.when(pl.program_id(1) == 0)
    def _():
        for pi in range(len(DILATED_PATTERNS)):
            _, _, kp, vp = operands[4 * pi:4 * pi + 4]
            kp[0:pad, :] = jnp.zeros((pad, LANES), BF16)
            kp[pad + seq:pad + seq + pad, :] = jnp.zeros((pad, LANES), BF16)
            vp[0:pad, :] = jnp.zeros((pad, 2 * LANES), BF16)
            vp[pad + seq:pad + seq + pad, :] = jnp.zeros((pad, 2 * LANES), BF16)
            vp[pad:pad + seq, LANES:2 * LANES] = jnp.ones((seq, LANES), BF16)
        row = lax.broadcasted_iota(jnp.int32, (2 * tq, 2 * tq), 0)
        col = lax.broadcasted_iota(jnp.int32, (2 * tq, 2 * tq), 1)
        kc = col - pad
        arel = jnp.abs(kc - (row & (tq - 1)))
        band = arel <= HALF_WINDOW
        slope_rows = jnp.where(row < tq, slopes_ref[2 * pair], slopes_ref[2 * pair + 1])
        idx = 0
        for _, dil in DILATED_PATTERNS:
            dist_bias = -slope_rows * (arel * dil).astype(F32)
            for case in _dil_cases(seq, dil)[1]:
                valid = band
                if case in ("first", "single"):
                    valid = valid & (kc >= 0)
                if case in ("last", "single"):
                    valid = valid & (kc < tq)
                bias[idx] = jnp.where(valid, dist_bias, NEG_INF)
                idx += 1

    natq[...] = q_ref[0].astype(F32) * (HEAD_DIM ** -0.5)
    natk[...] = k_ref[0].astype(F32)
    natv[...] = v_ref[0].astype(F32)

    def set_operands(pi, rows, qv, kv, vv):
        qp0, qp1, kp, vp = operands[4 * pi:4 * pi + 4]
        lo, hi = rows
        qp0[lo:hi, :] = (qv * m0).astype(BF16)
        qp1[lo:hi, :] = (qv * m1).astype(BF16)
        kp[pad + lo:pad + hi, :] = kv.astype(BF16)
        vp[pad + lo:pad + hi, 0:LANES] = vv.astype(BF16)

    def run_tiles(pi, case0, tps, dests):
        num_ref, mx_ref, sum_ref = dests
        qp0, qp1, kp, vp = operands[4 * pi:4 * pi + 4]

        def tile(m, carry):
            r0 = pl.multiple_of(m * tq, tq)
            qc = jnp.concatenate([qp0[pl.ds(r0, tq), :], qp1[pl.ds(r0, tq), :]], axis=0)
            kt = kp[pl.ds(r0, 2 * tq), :]
            vt = vp[pl.ds(r0, 2 * tq), :]
            if tps == 1:
                case = case0
                dst = pl.ds((m >> log_step) * seg2 + (m & (step - 1)), tq, stride=step)
            else:
                pos = m & (tps - 1)
                case = case0 + jnp.where(pos == 0, 0, jnp.where(pos == tps - 1, 2, 1))
                dst = pl.ds(r0, tq)
            s = _nt_dot(qc, kt) + bias[case]
            mx = jnp.max(s, axis=-1, keepdims=True)
            pe = jnp.exp(s - mx)
            pv = jnp.dot(pe.astype(BF16), vt, preferred_element_type=F32)
            num_ref[dst, :] = jnp.where(head0, pv[0:tq, 0:LANES], pv[tq:2 * tq, 0:LANES])
            sum_ref[dst, :] = jnp.where(head0, pv[0:tq, LANES:2 * LANES],
                                        pv[tq:2 * tq, LANES:2 * LANES])
            mx_ref[dst, :] = jnp.where(head0, mx[0:tq], mx[tq:2 * tq])
            return carry

        lax.fori_loop(0, seq // tq, tile, 0, unroll=DIL_TILE_UNROLL)

    case0 = 0
    set_operands(0, (0, seq), natq[...], natk[...], natv[...])
    tps1, cases1 = _dil_cases(seq, DILATED_PATTERNS[0][1])
    run_tiles(0, case0, tps1, (num1, mx1, sum1))
    case0 += len(cases1)

    for b in range(step):
        rows = (b * seg2, (b + 1) * seg2)
        qv = natq[pl.ds(b, seg2, stride=step), :]
        kv = natk[pl.ds(b, seg2, stride=step), :]
        vv = natv[pl.ds(b, seg2, stride=step), :]
        perq[rows[0]:rows[1], :] = qv
        perk[rows[0]:rows[1], :] = kv
        perv[rows[0]:rows[1], :] = vv
        set_operands(1, rows, qv, kv, vv)
    tps2, cases2 = _dil_cases(seq, DILATED_PATTERNS[1][1])
    run_tiles(1, case0, tps2, (num2, mx2, sum2))
    case0 += len(cases2)

    for sgm in range(step * step):
        b, a = divmod(sgm, step)
        src = pl.ds(b * seg2 + a, seg3, stride=step)
        set_operands(2, (sgm * seg3, (sgm + 1) * seg3), perq[src, :], perk[src, :], perv[src, :])
    tps3, _ = _dil_cases(seq, DILATED_PATTERNS[2][1])
    assert tps3 == 1 and seg3 == tq
    run_tiles(2, case0, tps3, (num3, mx3, sum3))

    def merge(c, carry):
        p0 = pl.multiple_of(c * tq, tq)
        per_rows = pl.ds(p0, tq)
        nat_rows = pl.ds((p0 & (seg2 - 1)) * step + (p0 >> log_seg2), tq, stride=step)
        a1, a2, a3 = mx1[nat_rows, :], mx2[per_rows, :], mx3[per_rows, :]
        top = jnp.maximum(jnp.maximum(a1, a2), a3)
        e1, e2, e3 = jnp.exp(a1 - top), jnp.exp(a2 - top), jnp.exp(a3 - top)
        den = e1 * sum1[nat_rows, :] + e2 * sum2[per_rows, :] + e3 * sum3[per_rows, :]
        num = e1 * num1[nat_rows, :] + e2 * num2[per_rows, :] + e3 * num3[per_rows, :]
        onat[nat_rows, :] = num / den
        return carry

    lax.fori_loop(0, seq // tq, merge, 0, unroll=2)
    o_ref[0] = onat[...].astype(BF16)


def _dilated(hq3, slopes, dil_width):
    b, s, _ = hq3.shape
    npair = dil_width // LANES
    blk = (1, s, LANES)
    dils = [d for _, d in DILATED_PATTERNS]
    assert dils[0] == 1 and dils[2] == dils[1] * dils[1]
    n_cases = sum(len(_dil_cases(s, d)[1]) for d in dils)
    pad = HALF_WINDOW
    scratch = [pltpu.VMEM((s, LANES), F32)] * 6
    scratch += [pltpu.VMEM((n_cases, 2 * ATT_TILE, 2 * ATT_TILE), F32)]
    scratch += [pltpu.VMEM((s, LANES), F32)] * 10
    scratch += [pltpu.VMEM((s, LANES), BF16), pltpu.VMEM((s, LANES), BF16),
                pltpu.VMEM((s + 2 * pad, LANES), BF16),
                pltpu.VMEM((s + 2 * pad, 2 * LANES), BF16)] * len(dils)
    return pl.pallas_call(
        _dil_kernel,
        grid_spec=pltpu.PrefetchScalarGridSpec(
            num_scalar_prefetch=1,
            grid=(npair, b),
            in_specs=[pl.BlockSpec(blk, lambda j, i, sl: (i, 0, j)),
                      pl.BlockSpec(blk, lambda j, i, sl: (i, 0, npair + j)),
                      pl.BlockSpec(blk, lambda j, i, sl: (i, 0, 2 * npair + j))],
            out_specs=pl.BlockSpec(blk, lambda j, i, sl: (i, 0, j)),
            scratch_shapes=scratch),
        out_shape=jax.ShapeDtypeStruct((b, s, dil_width), BF16),
        compiler_params=_params("arbitrary", "arbitrary"),
        name="dilated_attn",
    )(slopes, hq3, hq3, hq3)


def _memkv_kernel(m_ref, g_ref, w_ref, o_ref):
    mn = _rms(m_ref[...], g_ref[...]).astype(BF16)
    o_ref[...] = jnp.dot(mn, w_ref[...], preferred_element_type=F32).astype(BF16)


def _memkv(mem2, g, w):
    n, d = mem2.shape
    nw = w.shape[1]
    return pl.pallas_call(
        _memkv_kernel,
        grid=(n // ROW_TILE,),
        in_specs=[pl.BlockSpec((ROW_TILE, d), lambda i: (i, 0)),
                  pl.BlockSpec((1, d), lambda i: (0, 0)),
                  pl.BlockSpec((d, nw), lambda i: (0, 0))],
        out_specs=pl.BlockSpec((ROW_TILE, nw), lambda i: (i, 0)),
        out_shape=jax.ShapeDtypeStruct((n, nw), BF16),
        compiler_params=_params("arbitrary"),
        name="mem_kv",
    )(mem2, g, w)


def _memattn_kernel(q_ref, km_ref, vm_ref, o_ref, qp0, qp1, vext):
    seq = q_ref.shape[1]
    tq = 2 * ATT_TILE
    lane = lax.broadcasted_iota(jnp.int32, (1, LANES), 1)
    head0 = lane < HEAD_DIM
    m0 = head0.astype(F32)
    q = q_ref[0].astype(F32) * (HEAD_DIM ** -0.5)
    qp0[...] = (q * m0).astype(BF16)
    qp1[...] = (q * (1.0 - m0)).astype(BF16)
    vext[:, 0:LANES] = vm_ref[0]
    vext[:, LANES:2 * LANES] = jnp.ones((vext.shape[0], LANES), BF16)
    km = km_ref[0]

    def tile(m, carry):
        rows = pl.ds(pl.multiple_of(m * tq, tq), tq)
        qc = jnp.concatenate([qp0[rows, :], qp1[rows, :]], axis=0)
        s = _nt_dot(qc, km)
        pe = jnp.exp(s - jnp.max(s, axis=-1, keepdims=True))
        pv = jnp.dot(pe.astype(BF16), vext[...], preferred_element_type=F32)
        o = pv[:, 0:LANES] / pv[:, LANES:2 * LANES]
        o_ref[0, rows, :] = jnp.where(head0, o[0:tq], o[tq:2 * tq]).astype(BF16)
        return carry

    lax.fori_loop(0, seq // tq, tile, 0, unroll=True)


def _memattn(hq3, kvm3, layer, qm_block0, mem_width):
    b, s, _ = hq3.shape
    m = kvm3.shape[1]
    npair = mem_width // LANES
    kv_blocks = 2 * npair
    return pl.pallas_call(
        _memattn_kernel,
        grid=(b, npair),
        in_specs=[pl.BlockSpec((1, s, LANES), lambda i, j: (i, 0, qm_block0 + j)),
                  pl.BlockSpec((1, m, LANES), lambda i, j: (i, 0, layer * kv_blocks + j)),
                  pl.BlockSpec((1, m, LANES), lambda i, j: (i, 0, layer * kv_blocks + npair + j))],
        out_specs=pl.BlockSpec((1, s, LANES), lambda i, j: (i, 0, j)),
        out_shape=jax.ShapeDtypeStruct((b, s, mem_width), BF16),
        scratch_shapes=[pltpu.VMEM((s, LANES), BF16), pltpu.VMEM((s, LANES), BF16),
                        pltpu.VMEM((m, 2 * LANES), BF16)],
        compiler_params=_params("arbitrary", "arbitrary"),
        name="mem_attn",
    )(hq3, kvm3, kvm3)


OUT_SUB_ROWS = 256
TOKEN_TILE_ROWS = 8


def _outproj_kernel(x_ref, c_ref, d_ref, m_ref, wf_ref, g_ref, wr_ref,
                    x1_ref, xn_ref, aff_ref, w_ref, *, n_exp):
    @pl.when(pl.program_id(0) == 0)
    def _():
        w_ref[...] = wf_ref[0].astype(BF16)

    valid = lax.broadcasted_iota(jnp.int32, (1, LANES), 1) < n_exp
    for r0 in range(0, x_ref.shape[0], OUT_SUB_ROWS):
        rs = slice(r0, r0 + OUT_SUB_ROWS)
        cat = jnp.concatenate([c_ref[rs, :], d_ref[rs, :], m_ref[rs, :]], axis=1)
        x1 = x_ref[rs, :] + jnp.dot(cat, w_ref[...], preferred_element_type=F32)
        x1_ref[rs, :] = x1
        xn = _rms(x1, g_ref[...])
        for j in range(xn.shape[1] // LANES):
            xn_ref[pl.ds(r0 * TOKEN_TILE_ROWS + j, OUT_SUB_ROWS, stride=TOKEN_TILE_ROWS), :] = (
                xn[:, j * LANES:(j + 1) * LANES])
        xh = xn.astype(BF16)
        xl = (xn - xh.astype(F32)).astype(BF16)
        both = jnp.dot(xh, wr_ref[...], preferred_element_type=F32)
        logits = (both[:, 0:LANES] + both[:, LANES:2 * LANES]
                  + jnp.dot(xl, wr_ref[:, 0:LANES], preferred_element_type=F32))
        logits = jnp.where(valid, logits, NEG_INF)
        ex = jnp.exp(logits - jnp.max(logits, axis=1, keepdims=True))
        aff_ref[rs, :] = ex / jnp.sum(ex, axis=1, keepdims=True)


def _outproj(x2, conv_o, dil_o, mem_o, w_all, layer, g, wr_pieces, n_exp):
    n, d = x2.shape
    assert d == TOKEN_TILE_ROWS * LANES
    row = lambda i: (i, 0)
    fixed = lambda i: (0, 0)
    return pl.pallas_call(
        functools.partial(_outproj_kernel, n_exp=n_exp),
        grid=(n // PROJ_ROW_TILE,),
        in_specs=[pl.BlockSpec((PROJ_ROW_TILE, d), row),
                  pl.BlockSpec((PROJ_ROW_TILE, conv_o.shape[1]), row),
                  pl.BlockSpec((PROJ_ROW_TILE, dil_o.shape[1]), row),
                  pl.BlockSpec((PROJ_ROW_TILE, mem_o.shape[1]), row),
                  pl.BlockSpec((1,) + w_all.shape[1:], lambda i: (layer, 0, 0)),
                  pl.BlockSpec((1, d), fixed),
                  pl.BlockSpec((d, 2 * LANES), fixed)],
        out_specs=[pl.BlockSpec((PROJ_ROW_TILE, d), row),
                   pl.BlockSpec((PROJ_ROW_TILE * TOKEN_TILE_ROWS, LANES), row),
                   pl.BlockSpec((PROJ_ROW_TILE, LANES), row)],
        out_shape=[jax.ShapeDtypeStruct((n, d), F32),
                   jax.ShapeDtypeStruct((n * TOKEN_TILE_ROWS, LANES), F32),
                   jax.ShapeDtypeStruct((n, LANES), F32)],
        scratch_shapes=[pltpu.VMEM(w_all.shape[1:], BF16)],
        compiler_params=_params("arbitrary"),
        name="outproj_router",
    )(x2, conv_o, dil_o, mem_o, w_all, g, wr_pieces)


TOK_DIGIT_BITS = 6
PICK_GATE_LANE0 = 2


def _fill_strict_upper(tri):
    s = tri.shape[0]
    chunk = 2 * LANES
    for c in range(0, s, chunk):
        src = lax.broadcasted_iota(jnp.int32, (chunk, s), 0) + c
        dst = lax.broadcasted_iota(jnp.int32, (chunk, s), 1)
        tri[c:c + chunk, :] = jnp.where(src < dst, 1.0, 0.0).astype(BF16)


def _topk_kernel(aff_ref, afft_ref, slot_ref, pick_ref, idx_ref, rhs, tri, *, cap, n_batch, n_exp):
    aff = aff_ref[...]
    rows, seq = aff.shape
    _fill_strict_upper(tri)
    bits = jnp.zeros((rows, 1), jnp.int32)
    for bit in range(30, -1, -1):
        cand = bits | (1 << bit)
        cnt = jnp.sum(jnp.where(aff >= pltpu.bitcast(cand, F32), 1.0, 0.0), axis=1, keepdims=True)
        bits = jnp.where(cnt >= cap, cand, bits)
    thr = pltpu.bitcast(bits, F32)
    gt = aff > thr
    eq = aff == thr
    need = cap - jnp.sum(jnp.where(gt, 1.0, 0.0), axis=1, keepdims=True)
    rank_eq = jnp.dot(jnp.where(eq, 1.0, 0.0).astype(BF16), tri[...], preferred_element_type=F32)
    sel = jnp.where(gt, 1.0, jnp.where(eq & (rank_eq < need), 1.0, 0.0))
    rank = jnp.dot(sel.astype(BF16), tri[...], preferred_element_type=F32)
    slot_ref[...] = jnp.where(sel > 0.5, rank, -1.0).astype(jnp.int32)

    half = seq // 2
    lane = lax.broadcasted_iota(jnp.int32, (1, LANES), 1)
    tok = lax.broadcasted_iota(jnp.int32, (seq, LANES), 0)
    tok_part = jnp.where(lane == 0, tok >> TOK_DIGIT_BITS,
                         jnp.where(lane == 1, tok & ((1 << TOK_DIGIT_BITS) - 1),
                                   jnp.where(lane == LANES - 1, 1, 0))).astype(F32)
    src = lax.broadcasted_iota(jnp.int32, (LANES, LANES), 0)
    dst = lax.broadcasted_iota(jnp.int32, (LANES, LANES), 1)

    def place(piece):
        return jnp.where((src < n_exp) & (dst == src + PICK_GATE_LANE0 + piece * n_exp),
                         1.0, 0.0).astype(BF16)

    for bi in range(n_batch):
        a = afft_ref[bi * seq:(bi + 1) * seq, :]
        a_hi = a.astype(BF16)
        a_lo = (a - a_hi.astype(F32)).astype(BF16)
        full = (tok_part
                + jnp.dot(a_hi, place(0), preferred_element_type=F32)
                + jnp.dot(a_lo, place(1), preferred_element_type=F32)).astype(BF16)
        rhs[bi, :, 0:LANES] = full[0:half]
        rhs[bi, :, LANES:2 * LANES] = full[half:seq]

    sidx = lax.broadcasted_iota(jnp.int32, (cap, half), 0)

    def invert(r, carry):
        bi = lax.rem(r, n_batch)
        srow = slot_ref[pl.ds(r, 1), :]
        onehot = jnp.where(srow[:, 0:half] == sidx, 1.0,
                           jnp.where(srow[:, half:seq] == sidx, 2.0, 0.0)).astype(BF16)
        both = jnp.dot(onehot, rhs[bi], preferred_element_type=F32)
        lower = both[:, LANES - 1:LANES] < 1.5
        pick = jnp.where(lower, both[:, 0:LANES], both[:, LANES:2 * LANES] * 0.5)
        pick_ref[pl.ds(pl.multiple_of(r * cap, cap), cap), :] = pick
        digits = jnp.transpose(pick)
        token = (digits[0:1] * (1 << TOK_DIGIT_BITS) + digits[1:2]).astype(jnp.int32)
        idx_ref[pl.ds(r, 1), :] = (token + bi * seq) * TOKEN_TILE_ROWS
        return carry

    lax.fori_loop(0, rows, invert, 0, unroll=2)


def _topk(aff_rows, aff_tok, cap, n_batch, n_exp):
    rows, seq = aff_rows.shape
    return pl.pallas_call(
        functools.partial(_topk_kernel, cap=cap, n_batch=n_batch, n_exp=n_exp),
        out_shape=[jax.ShapeDtypeStruct(aff_rows.shape, jnp.int32),
                   jax.ShapeDtypeStruct((rows * cap, LANES), F32),
                   jax.ShapeDtypeStruct((rows, cap), jnp.int32)],
        scratch_shapes=[pltpu.VMEM((n_batch, seq // 2, 2 * LANES), BF16),
                        pltpu.VMEM((seq, seq), BF16)],
        compiler_params=pltpu.CompilerParams(vmem_limit_bytes=V7X_VMEM_LIMIT_BYTES),
        name="expert_topk",
    )(aff_rows, aff_tok)


FF_CHUNK = 512
EXPERT_ROWS = 512


def _expert_kernel(idx_ref, xn_hbm, wg_ref, wu_ref, wd_ref, pick_ref, y_ref,
                   xbuf, sem, acc, wgb, wub, wdb, *, cap, n_f):
    e = pl.program_id(0)
    f = pl.program_id(1)
    n_exp = pl.num_programs(0)
    rows = acc.shape[0]
    slot = lax.rem(e, 2)
    nxt = 1 - slot
    e_nxt = jnp.minimum(e + 1, n_exp - 1)

    tile = TOKEN_TILE_ROWS

    def row_copy(src_tile_row, buf, dst_row):
        return pltpu.make_async_copy(
            xn_hbm.at[pl.ds(pl.multiple_of(src_tile_row, tile), tile), :],
            xbuf.at[pl.ds(pl.multiple_of((buf * rows + dst_row) * tile, tile), tile), :], sem.at[buf])

    def wait_rows(buf):
        pltpu.make_async_copy(xn_hbm.at[pl.ds(0, rows * tile), :],
                              xbuf.at[pl.ds(pl.multiple_of(buf * rows * tile, tile), rows * tile), :],
                              sem.at[buf]).wait()

    @pl.when((e == 0) & (f == 0))
    def _():
        def first(j, carry):
            row_copy(idx_ref[j], 0, j).start()
            return carry
        lax.fori_loop(0, rows, first, 0, unroll=8)

    @pl.when(f == 0)
    def _():
        wait_rows(slot)
        acc[...] = jnp.zeros(acc.shape, F32)

    wgb[...] = wg_ref[0, 0].astype(BF16)
    wub[...] = wu_ref[0, 0].astype(BF16)
    wdb[...] = wd_ref[0, 0].astype(BF16)

    per_step = rows // n_f
    per_chunk = per_step // (rows // EXPERT_ROWS)
    base = pl.multiple_of(f * per_step, per_step)
    idx_base = e_nxt * rows + base
    for ci, r0 in enumerate(range(0, rows, EXPERT_ROWS)):
        for k in range(ci * per_chunk, (ci + 1) * per_chunk):
            row_copy(idx_ref[idx_base + k], nxt, base + k).start()
        rsl = slice(r0, r0 + EXPERT_ROWS)
        xm = jnp.concatenate(
            [xbuf[pl.ds((slot * rows + r0) * tile + j, EXPERT_ROWS, stride=tile), :]
             for j in range(tile)], axis=1).astype(BF16)
        gate = jnp.dot(xm, wgb[...], preferred_element_type=F32)
        up = jnp.dot(xm, wub[...], preferred_element_type=F32)
        hid = (gate * (1.0 / (1.0 + jnp.exp(-gate))) * up).astype(BF16)
        acc[rsl, :] += jnp.dot(hid, wdb[...], preferred_element_type=F32)

    @pl.when(f == n_f - 1)
    def _():
        lane = lax.broadcasted_iota(jnp.int32, (1, LANES), 1) - (PICK_GATE_LANE0 + e)
        mine = (lane == 0) | (lane == n_exp)
        for r0 in range(0, rows, EXPERT_ROWS):
            rsl = slice(r0, r0 + EXPERT_ROWS)
            gate_col = jnp.sum(jnp.where(mine, pick_ref[rsl, :], 0.0), axis=1, keepdims=True)
            y_ref[0, rsl, :] = (acc[rsl, :] * gate_col).astype(BF16)

    @pl.when((e == n_exp - 1) & (f == n_f - 1))
    def _():
        wait_rows(nxt)


def _experts(idx2, xn_tiles, w_gate, w_up, w_down, layer, picked, cap):
    n_exp, rows = idx2.shape
    d = w_gate.shape[2]
    d_ff = w_gate.shape[3]
    n_f = d_ff // FF_CHUNK
    assert rows % (n_f * (rows // EXPERT_ROWS)) == 0
    return pl.pallas_call(
        functools.partial(_expert_kernel, cap=cap, n_f=n_f),
        grid_spec=pltpu.PrefetchScalarGridSpec(
            num_scalar_prefetch=1,
            grid=(n_exp, n_f),
            in_specs=[pl.BlockSpec(memory_space=pl.ANY),
                      pl.BlockSpec((1, 1, d, FF_CHUNK), lambda e, f, ix: (layer, e, 0, f)),
                      pl.BlockSpec((1, 1, d, FF_CHUNK), lambda e, f, ix: (layer, e, 0, f)),
                      pl.BlockSpec((1, 1, FF_CHUNK, d), lambda e, f, ix: (layer, e, f, 0)),
                      pl.BlockSpec((rows, LANES), lambda e, f, ix: (e, 0))],
            out_specs=pl.BlockSpec((1, rows, d), lambda e, f, ix: (e, 0, 0)),
            scratch_shapes=[pltpu.VMEM((2 * rows * TOKEN_TILE_ROWS, LANES), F32),
                            pltpu.SemaphoreType.DMA((2,)),
                            pltpu.VMEM((rows, d), F32),
                            pltpu.VMEM((d, FF_CHUNK), BF16),
                            pltpu.VMEM((d, FF_CHUNK), BF16),
                            pltpu.VMEM((FF_CHUNK, d), BF16)]),
        out_shape=jax.ShapeDtypeStruct((n_exp, rows, d), BF16),
        compiler_params=_params("arbitrary", "arbitrary"),
        name="expert_ffn",
    )(idx2.reshape(-1), xn_tiles, w_gate, w_up, w_down, picked)


COMBINE_GROUP = 4
COMBINE_WINDOW = 128
BF16_ROWS = 16


def _combine_kernel(win_ref, fits_ref, x_ref, slott_ref, y_ref, g_ref, o_ref, *, cap, final):
    bi = pl.program_id(0)
    ti = pl.program_id(1)
    tile_id = bi * pl.num_programs(1) + ti
    slott = slott_ref[0]
    t, n_exp = slott.shape
    d = x_ref.shape[2]

    def finish(acc):
        if final:
            acc = _rms(acc, g_ref[...])
        o_ref[0] = acc

    @pl.when(fits_ref[tile_id] != 0)
    def _():
        lane = lax.broadcasted_iota(jnp.int32, (t, COMBINE_WINDOW), 1)
        acc = x_ref[0]
        for e0 in range(0, n_exp, COMBINE_GROUP):
            parts, ys = [], []
            for e in range(e0, e0 + COMBINE_GROUP):
                start = pl.multiple_of(win_ref[tile_id * n_exp + e], BF16_ROWS)
                parts.append(jnp.where(slott[:, e:e + 1] - start == lane, 1.0, 0.0).astype(BF16))
                ys.append(y_ref[e, 0, pl.ds(start, COMBINE_WINDOW), :])
            acc = acc + jnp.dot(jnp.concatenate(parts, axis=1), jnp.concatenate(ys, axis=0),
                                preferred_element_type=F32)
        finish(acc)

    @pl.when(fits_ref[tile_id] == 0)
    def _():
        sidx = lax.broadcasted_iota(jnp.int32, (t, cap), 1)
        acc = x_ref[0]
        for e0 in range(0, n_exp, COMBINE_GROUP):
            parts = [jnp.where(slott[:, e:e + 1] == sidx, 1.0, 0.0).astype(BF16)
                     for e in range(e0, e0 + COMBINE_GROUP)]
            yg = y_ref[e0:e0 + COMBINE_GROUP, 0].reshape(COMBINE_GROUP * cap, d)
            acc = acc + jnp.dot(jnp.concatenate(parts, axis=1), yg, preferred_element_type=F32)
        finish(acc)


def _combine_windows(slot_rows, n_exp, b, s, cap):
    nt = s // ROW_TILE
    cnt = jnp.sum((slot_rows >= 0).reshape(n_exp, b, nt, ROW_TILE), axis=-1, dtype=jnp.int32)
    end = jnp.cumsum(cnt, axis=-1)
    start = jnp.minimum((end - cnt) // BF16_ROWS * BF16_ROWS, cap - COMBINE_WINDOW)
    fits = jnp.all(end - start <= COMBINE_WINDOW, axis=0)
    return jnp.transpose(start, (1, 2, 0)).reshape(-1), fits.reshape(-1).astype(jnp.int32)


def _combine(x3, slott, y4, g, win, fits, cap, final):
    b, s, d = x3.shape
    n_exp = y4.shape[0]
    assert cap >= COMBINE_WINDOW and cap % BF16_ROWS == 0
    return pl.pallas_call(
        functools.partial(_combine_kernel, cap=cap, final=final),
        grid_spec=pltpu.PrefetchScalarGridSpec(
            num_scalar_prefetch=2,
            grid=(b, s // ROW_TILE),
            in_specs=[pl.BlockSpec((1, ROW_TILE, d), lambda i, t, w, f: (i, t, 0)),
                      pl.BlockSpec((1, ROW_TILE, n_exp), lambda i, t, w, f: (i, t, 0)),
                      pl.BlockSpec((n_exp, 1, cap, d), lambda i, t, w, f: (0, i, 0, 0)),
                      pl.BlockSpec((1, d), lambda i, t, w, f: (0, 0))],
            out_specs=pl.BlockSpec((1, ROW_TILE, d), lambda i, t, w, f: (i, t, 0))),
        out_shape=jax.ShapeDtypeStruct((b, s, d), F32),
        compiler_params=_params("arbitrary", "arbitrary"),
        name="combine",
    )(win, fits, x3, slott, y4, g)


def kernel(x, mem, mem_norm, norm_mix, w_in, conv_w, w_mem_kv, w_out, norm_ffn,
           w_router, w_gate, w_up, w_down, norm_final):
    b, s, d = x.shape
    depth = w_in.shape[0]
    n_exp = w_router.shape[2]
    conv_ch = conv_w.shape[2]
    mem_width = w_mem_kv.shape[2] // 2
    dil_width = (w_in.shape[2] - 3 * conv_ch - mem_width) // 3
    n_dil_heads = dil_width // HEAD_DIM
    cap = EC_CAPACITY * s // n_exp
    assert n_exp == N_EXPERTS and mem_width == N_MEM_HEADS * HEAD_DIM
    assert all(w // (2 * dl) == HALF_WINDOW for w, dl in DILATED_PATTERNS)

    slopes = jnp.asarray([2.0 ** (-8.0 * (h + 1) / n_dil_heads) for h in range(n_dil_heads)], F32)
    kv_all = jnp.transpose(w_mem_kv, (1, 0, 2)).reshape(d, depth * 2 * mem_width).astype(BF16)
    kvm3 = _memkv(mem.reshape(-1, d), mem_norm.reshape(1, d), kv_all).reshape(b, mem.shape[1], -1)

    x2 = x.reshape(b * s, d)
    out = None
    for l in range(depth):
        conv_o, hq = _inproj(x2, norm_mix[l].reshape(1, d), w_in, l, conv_w[l], conv_ch, s)
        hq3 = hq.reshape(b, s, -1)
        dil_o = _dilated(hq3, slopes, dil_width)
        mem_o = _memattn(hq3, kvm3, l, 3 * dil_width // LANES, mem_width)
        wr = jnp.pad(w_router[l], ((0, 0), (0, LANES - n_exp)))
        wr_hi = wr.astype(BF16)
        wr_pieces = jnp.concatenate([wr_hi, (wr - wr_hi.astype(F32)).astype(BF16)], axis=1)
        x1, xn, aff_tok = _outproj(x2, conv_o, dil_o.reshape(b * s, -1),
                                   mem_o.reshape(b * s, -1), w_out, l,
                                   norm_ffn[l].reshape(1, d), wr_pieces, n_exp)
        aff_rows = jnp.transpose(aff_tok[:, :n_exp]).reshape(n_exp * b, s)
        slot_rows, picked, idx_rows = _topk(aff_rows, aff_tok, cap, b, n_exp)
        y = _experts(idx_rows.reshape(n_exp, b * cap), xn, w_gate, w_up, w_down, l, picked, cap)
        slott = jnp.transpose(slot_rows.reshape(n_exp, b, s), (1, 2, 0))
        win, fits = _combine_windows(slot_rows, n_exp, b, s, cap)
        final = l == depth - 1
        x3 = _combine(x1.reshape(b, s, d), slott, y.reshape(n_exp, b, cap, d),
                      norm_final.reshape(1, d), win, fits, cap, final)
        x2 = x3.reshape(b * s, d)
        out = x3
    return out
```

```python
import functools

import jax
import jax.numpy as jnp
from jax import lax
from jax.experimental import pallas as pl
from jax.experimental.pallas import tpu as pltpu

F32 = jnp.float32
BF16 = jnp.bfloat16

HEAD_DIM = 64
N_MEM_HEADS = 4
DILATED_PATTERNS = ((128, 1), (512, 4), (2048, 16))
HALF_WINDOW = 64
N_EXPERTS = 16
EC_CAPACITY = 2
RMS_EPS = 1e-6
NEG_INF = -1e30

LANES = 128
ROW_TILE = 512
PROJ_ROW_TILE = 1024
ATT_TILE = 128
V7X_VMEM_LIMIT_BYTES = 56 * 1024 * 1024


def _params(*semantics):
    return pltpu.CompilerParams(dimension_semantics=semantics,
                                vmem_limit_bytes=V7X_VMEM_LIMIT_BYTES)


def _rms(x, g):
    ms = jnp.mean(x * x, axis=-1, keepdims=True)
    return x * lax.rsqrt(ms + RMS_EPS) * g


def _nt_dot(a, b, **kw):
    return lax.dot_general(a, b, (((1,), (1,)), ((), ())), preferred_element_type=F32, **kw)


HALO_ROWS = 8


def _inproj_kernel(x_ref, xprev_ref, xnext_ref, g_ref, wf_ref, cw_ref, conv_ref, hq_ref, hc, w_ref,
                   *, conv_ch, tiles_per_seq):
    @pl.when(pl.program_id(0) == 0)
    def _():
        w_ref[...] = wf_ref[0].astype(BF16)

    t = x_ref.shape[0]
    n_conv = 3 * conv_ch
    g = g_ref[...]
    halo = jnp.concatenate([xprev_ref[...], xnext_ref[...]], axis=0)
    xa = jnp.concatenate([_rms(x_ref[...], g), _rms(halo, g)], axis=0).astype(BF16)
    chunk = 2 * LANES
    for c in range(0, w_ref.shape[1], chunk):
        if c >= n_conv:
            hq_ref[:, c - n_conv:c - n_conv + chunk] = jnp.dot(
                xa[0:t], w_ref[:, c:c + chunk], preferred_element_type=F32).astype(BF16)
            continue
        h = jnp.dot(xa, w_ref[:, c:c + chunk], preferred_element_type=F32)
        if c + chunk <= n_conv:
            hc[:, c:c + chunk] = h
        else:
            k = n_conv - c
            hc[:, c:n_conv] = h[:, :k]
            hq_ref[:, 0:chunk - k] = h[0:t, k:].astype(BF16)

    u_all = hc[:, 2 * conv_ch:3 * conv_ch] * hc[:, 0:conv_ch]
    u = u_all[0:t]
    pos = lax.rem(pl.program_id(0), tiles_per_seq)
    u_before = jnp.where(pos == 0, 0.0, u_all[t + HALO_ROWS - 1:t + HALO_ROWS])
    u_after = jnp.where(pos == tiles_per_seq - 1, 0.0, u_all[t + HALO_ROWS:t + HALO_ROWS + 1])
    row = lax.broadcasted_iota(jnp.int32, u.shape, 0)
    prev = jnp.where(row == 0, u_before, pltpu.roll(u, 1, 0))
    nxt = jnp.where(row == t - 1, u_after, pltpu.roll(u, t - 1, 0))
    w = cw_ref[...]
    conv = w[0:1] * prev + w[1:2] * u + w[2:3] * nxt
    conv_ref[...] = (hc[0:t, conv_ch:2 * conv_ch] * conv).astype(BF16)


def _inproj(x2, g, w_all, layer, conv_w, conv_ch, seq):
    n, d = x2.shape
    n_all = w_all.shape[2]
    n_conv = 3 * conv_ch
    per_tile = PROJ_ROW_TILE // HALO_ROWS
    last = n // HALO_ROWS - 1
    assert seq % PROJ_ROW_TILE == 0
    return pl.pallas_call(
        functools.partial(_inproj_kernel, conv_ch=conv_ch, tiles_per_seq=seq // PROJ_ROW_TILE),
        grid=(n // PROJ_ROW_TILE,),
        in_specs=[pl.BlockSpec((PROJ_ROW_TILE, d), lambda i: (i, 0)),
                  pl.BlockSpec((HALO_ROWS, d), lambda i: (jnp.maximum(i * per_tile - 1, 0), 0)),
                  pl.BlockSpec((HALO_ROWS, d), lambda i: (jnp.minimum((i + 1) * per_tile, last), 0)),
                  pl.BlockSpec((1, d), lambda i: (0, 0)),
                  pl.BlockSpec((1, d, n_all), lambda i: (layer, 0, 0)),
                  pl.BlockSpec(conv_w.shape, lambda i: (0, 0))],
        out_specs=[pl.BlockSpec((PROJ_ROW_TILE, conv_ch), lambda i: (i, 0)),
                   pl.BlockSpec((PROJ_ROW_TILE, n_all - n_conv), lambda i: (i, 0))],
        out_shape=[jax.ShapeDtypeStruct((n, conv_ch), BF16),
                   jax.ShapeDtypeStruct((n, n_all - n_conv), BF16)],
        scratch_shapes=[pltpu.VMEM((PROJ_ROW_TILE + 2 * HALO_ROWS, n_conv), F32),
                        pltpu.VMEM((d, n_all), BF16)],
        compiler_params=_params("arbitrary"),
        name="inproj_conv",
    )(x2, x2, x2, g, w_all, conv_w)


DIL_TILE_UNROLL = 16


def _dil_cases(seq, dil):
    tps = (seq // dil) // ATT_TILE
    return tps, (("single",) if tps == 1 else ("first", "interior", "last"))


def _dil_kernel(slopes_ref, q_ref, k_ref, v_ref, o_ref,
                natq, natk, natv, perq, perk, perv, bias,
                num1, mx1, sum1, num2, mx2, sum2, num3, mx3, sum3, onat, *operands):
    seq = q_ref.shape[1]
    pad = HALF_WINDOW
    tq = ATT_TILE
    step = DILATED_PATTERNS[1][1]
    seg2 = seq // step
    seg3 = seg2 // step
    log_seg2 = seg2.bit_length() - 1
    log_step = step.bit_length() - 1
    pair = pl.program_id(0)
    lane = lax.broadcasted_iota(jnp.int32, (1, LANES), 1)
    head0 = lane < HEAD_DIM
    m0 = head0.astype(F32)
    m1 = 1.0 - m0

    @pl.when(pl.program_id(1) == 0)
    def _():
        for pi in range(len(DILATED_PATTERNS)):
            _, _, kp, vp = operands[4 * pi:4 * pi + 4]
            kp[0:pad, :] = jnp.zeros((pad, LANES), BF16)
            kp[pad + seq:pad + seq + pad, :] = jnp.zeros((pad, LANES), BF16)
            vp[0:pad, :] = jnp.zeros((pad, 2 * LANES), BF16)
            vp[pad + seq:pad + seq + pad, :] = jnp.zeros((pad, 2 * LANES), BF16)
            vp[pad:pad + seq, LANES:2 * LANES] = jnp.ones((seq, LANES), BF16)
        row = lax.broadcasted_iota(jnp.int32, (2 * tq, 2 * tq), 0)
        col = lax.broadcasted_iota(jnp.int32, (2 * tq, 2 * tq), 1)
        kc = col - pad
        arel = jnp.abs(kc - (row & (tq - 1)))
        band = arel <= HALF_WINDOW
        slope_rows = jnp.where(row < tq, slopes_ref[2 * pair], slopes_ref[2 * pair + 1])
        idx = 0
        for _, dil in DILATED_PATTERNS:
            dist_bias = -slope_rows * (arel * dil).astype(F32)
            for case in _dil_cases(seq, dil)[1]:
                valid = band
                if case in ("first", "single"):
                    valid = valid & (kc >= 0)
                if case in ("last", "single"):
                    valid = valid & (kc < tq)
                bias[idx] = jnp.where(valid, dist_bias, NEG_INF)
                idx += 1

    natq[...] = q_ref[0].astype(F32) * (HEAD_DIM ** -0.5)
    natk[...] = k_ref[0].astype(F32)
    natv[...] = v_ref[0].astype(F32)

    def set_operands(pi, rows, qv, kv, vv):
        qp0, qp1, kp, vp = operands[4 * pi:4 * pi + 4]
        lo, hi = rows
        qp0[lo:hi, :] = (qv * m0).astype(BF16)
        qp1[lo:hi, :] = (qv * m1).astype(BF16)
        kp[pad + lo:pad + hi, :] = kv.astype(BF16)
        vp[pad + lo:pad + hi, 0:LANES] = vv.astype(BF16)

    def run_tiles(pi, case0, tps, dests):
        num_ref, mx_ref, sum_ref = dests
        qp0, qp1, kp, vp = operands[4 * pi:4 * pi + 4]

        def tile(m, carry):
            r0 = pl.multiple_of(m * tq, tq)
            qc = jnp.concatenate([qp0[pl.ds(r0, tq), :], qp1[pl.ds(r0, tq), :]], axis=0)
            kt = kp[pl.ds(r0, 2 * tq), :]
            vt = vp[pl.ds(r0, 2 * tq), :]
            if tps == 1:
                case = case0
                dst = pl.ds((m >> log_step) * seg2 + (m & (step - 1)), tq, stride=step)
            else:
                pos = m & (tps - 1)
                case = case0 + jnp.where(pos == 0, 0, jnp.where(pos == tps - 1, 2, 1))
                dst = pl.ds(r0, tq)
            s = _nt_dot(qc, kt) + bias[case]
            mx = jnp.max(s, axis=-1, keepdims=True)
            pe = jnp.exp(s - mx)
            pv = jnp.dot(pe.astype(BF16), vt, preferred_element_type=F32)
            num_ref[dst, :] = jnp.where(head0, pv[0:tq, 0:LANES], pv[tq:2 * tq, 0:LANES])
            sum_ref[dst, :] = jnp.where(head0, pv[0:tq, LANES:2 * LANES],
                                        pv[tq:2 * tq, LANES:2 * LANES])
            mx_ref[dst, :] = jnp.where(head0, mx[0:tq], mx[tq:2 * tq])
            return carry

        lax.fori_loop(0, seq // tq, tile, 0, unroll=DIL_TILE_UNROLL)

    case0 = 0
    set_operands(0, (0, seq), natq[...], natk[...], natv[...])
    tps1, cases1 = _dil_cases(seq, DILATED_PATTERNS[0][1])
    run_tiles(0, case0, tps1, (num1, mx1, sum1))
    case0 += len(cases1)

    for b in range(step):
        rows = (b * seg2, (b + 1) * seg2)
        qv = natq[pl.ds(b, seg2, stride=step), :]
        kv = natk[pl.ds(b, seg2, stride=step), :]
        vv = natv[pl.ds(b, seg2, stride=step), :]
        perq[rows[0]:rows[1], :] = qv
        perk[rows[0]:rows[1], :] = kv
        perv[rows[0]:rows[1], :] = vv
        set_operands(1, rows, qv, kv, vv)
    tps2, cases2 = _dil_cases(seq, DILATED_PATTERNS[1][1])
    run_tiles(1, case0, tps2, (num2, mx2, sum2))
    case0 += len(cases2)

    for sgm in range(step * step):
        b, a = divmod(sgm, step)
        src = pl.ds(b * seg2 + a, seg3, stride=step)
        set_operands(2, (sgm * seg3, (sgm + 1) * seg3), perq[src, :], perk[src, :], perv[src, :])
    tps3, _ = _dil_cases(seq, DILATED_PATTERNS[2][1])
    assert tps3 == 1 and seg3 == tq
    run_tiles(2, case0, tps3, (num3, mx3, sum3))

    def merge(c, carry):
        p0 = pl.multiple_of(c * tq, tq)
        per_rows = pl.ds(p0, tq)
        nat_rows = pl.ds((p0 & (seg2 - 1)) * step + (p0 >> log_seg2), tq, stride=step)
        a1, a2, a3 = mx1[nat_rows, :], mx2[per_rows, :], mx3[per_rows, :]
        top = jnp.maximum(jnp.maximum(a1, a2), a3)
        e1, e2, e3 = jnp.exp(a1 - top), jnp.exp(a2 - top), jnp.exp(a3 - top)
        den = e1 * sum1[nat_rows, :] + e2 * sum2[per_rows, :] + e3 * sum3[per_rows, :]
        num = e1 * num1[nat_rows, :] + e2 * num2[per_rows, :] + e3 * num3[per_rows, :]
        onat[nat_rows, :] = num / den
        return carry

    lax.fori_loop(0, seq // tq, merge, 0, unroll=2)
    o_ref[0] = onat[...].astype(BF16)


def _dilated(hq3, slopes, dil_width):
    b, s, _ = hq3.shape
    npair = dil_width // LANES
    blk = (1, s, LANES)
    dils = [d for _, d in DILATED_PATTERNS]
    assert dils[0] == 1 and dils[2] == dils[1] * dils[1]
    n_cases = sum(len(_dil_cases(s, d)[1]) for d in dils)
    pad = HALF_WINDOW
    scratch = [pltpu.VMEM((s, LANES), F32)] * 6
    scratch += [pltpu.VMEM((n_cases, 2 * ATT_TILE, 2 * ATT_TILE), F32)]
    scratch += [pltpu.VMEM((s, LANES), F32)] * 10
    scratch += [pltpu.VMEM((s, LANES), BF16), pltpu.VMEM((s, LANES), BF16),
                pltpu.VMEM((s + 2 * pad, LANES), BF16),
                pltpu.VMEM((s + 2 * pad, 2 * LANES), BF16)] * len(dils)
    return pl.pallas_call(
        _dil_kernel,
        grid_spec=pltpu.PrefetchScalarGridSpec(
            num_scalar_prefetch=1,
            grid=(npair, b),
            in_specs=[pl.BlockSpec(blk, lambda j, i, sl: (i, 0, j)),
                      pl.BlockSpec(blk, lambda j, i, sl: (i, 0, npair + j)),
                      pl.BlockSpec(blk, lambda j, i, sl: (i, 0, 2 * npair + j))],
            out_specs=pl.BlockSpec(blk, lambda j, i, sl: (i, 0, j)),
            scratch_shapes=scratch),
        out_shape=jax.ShapeDtypeStruct((b, s, dil_width), BF16),
        compiler_params=_params("arbitrary", "arbitrary"),
        name="dilated_attn",
    )(slopes, hq3, hq3, hq3)


def _memkv_kernel(m_ref, g_ref, w_ref, o_ref):
    mn = _rms(m_ref[...], g_ref[...]).astype(BF16)
    o_ref[...] = jnp.dot(mn, w_ref[...], preferred_element_type=F32).astype(BF16)


def _memkv(mem2, g, w):
    n, d = mem2.shape
    nw = w.shape[1]
    return pl.pallas_call(
        _memkv_kernel,
        grid=(n // ROW_TILE,),
        in_specs=[pl.BlockSpec((ROW_TILE, d), lambda i: (i, 0)),
                  pl.BlockSpec((1, d), lambda i: (0, 0)),
                  pl.BlockSpec((d, nw), lambda i: (0, 0))],
        out_specs=pl.BlockSpec((ROW_TILE, nw), lambda i: (i, 0)),
        out_shape=jax.ShapeDtypeStruct((n, nw), BF16),
        compiler_params=_params("arbitrary"),
        name="mem_kv",
    )(mem2, g, w)


def _memattn_kernel(q_ref, km_ref, vm_ref, o_ref, qp0, qp1, vext):
    seq = q_ref.shape[1]
    tq = 2 * ATT_TILE
    lane = lax.broadcasted_iota(jnp.int32, (1, LANES), 1)
    head0 = lane < HEAD_DIM
    m0 = head0.astype(F32)
    q = q_ref[0].astype(F32) * (HEAD_DIM ** -0.5)
    qp0[...] = (q * m0).astype(BF16)
    qp1[...] = (q * (1.0 - m0)).astype(BF16)
    vext[:, 0:LANES] = vm_ref[0]
    vext[:, LANES:2 * LANES] = jnp.ones((vext.shape[0], LANES), BF16)
    km = km_ref[0]

    def tile(m, carry):
        rows = pl.ds(pl.multiple_of(m * tq, tq), tq)
        qc = jnp.concatenate([qp0[rows, :], qp1[rows, :]], axis=0)
        s = _nt_dot(qc, km)
        pe = jnp.exp(s - jnp.max(s, axis=-1, keepdims=True))
        pv = jnp.dot(pe.astype(BF16), vext[...], preferred_element_type=F32)
        o = pv[:, 0:LANES] / pv[:, LANES:2 * LANES]
        o_ref[0, rows, :] = jnp.where(head0, o[0:tq], o[tq:2 * tq]).astype(BF16)
        return carry

    lax.fori_loop(0, seq // tq, tile, 0, unroll=True)


def _memattn(hq3, kvm3, layer, qm_block0, mem_width):
    b, s, _ = hq3.shape
    m = kvm3.shape[1]
    npair = mem_width // LANES
    kv_blocks = 2 * npair
    return pl.pallas_call(
        _memattn_kernel,
        grid=(b, npair),
        in_specs=[pl.BlockSpec((1, s, LANES), lambda i, j: (i, 0, qm_block0 + j)),
                  pl.BlockSpec((1, m, LANES), lambda i, j: (i, 0, layer * kv_blocks + j)),
                  pl.BlockSpec((1, m, LANES), lambda i, j: (i, 0, layer * kv_blocks + npair + j))],
        out_specs=pl.BlockSpec((1, s, LANES), lambda i, j: (i, 0, j)),
        out_shape=jax.ShapeDtypeStruct((b, s, mem_width), BF16),
        scratch_shapes=[pltpu.VMEM((s, LANES), BF16), pltpu.VMEM((s, LANES), BF16),
                        pltpu.VMEM((m, 2 * LANES), BF16)],
        compiler_params=_params("arbitrary", "arbitrary"),
        name="mem_attn",
    )(hq3, kvm3, kvm3)


OUT_SUB_ROWS = 256
TOKEN_TILE_ROWS = 8


def _outproj_kernel(x_ref, c_ref, d_ref, m_ref, wf_ref, g_ref, wr_ref,
                    x1_ref, xn_ref, aff_ref, w_ref, *, n_exp):
    @pl.when(pl.program_id(0) == 0)
    def _():
        w_ref[...] = wf_ref[0].astype(BF16)

    valid = lax.broadcasted_iota(jnp.int32, (1, LANES), 1) < n_exp
    for r0 in range(0, x_ref.shape[0], OUT_SUB_ROWS):
        rs = slice(r0, r0 + OUT_SUB_ROWS)
        cat = jnp.concatenate([c_ref[rs, :], d_ref[rs, :], m_ref[rs, :]], axis=1)
        x1 = x_ref[rs, :] + jnp.dot(cat, w_ref[...], preferred_element_type=F32)
        x1_ref[rs, :] = x1
        xn = _rms(x1, g_ref[...])
        for j in range(xn.shape[1] // LANES):
            xn_ref[pl.ds(r0 * TOKEN_TILE_ROWS + j, OUT_SUB_ROWS, stride=TOKEN_TILE_ROWS), :] = (
                xn[:, j * LANES:(j + 1) * LANES])
        xh = xn.astype(BF16)
        xl = (xn - xh.astype(F32)).astype(BF16)
        both = jnp.dot(xh, wr_ref[...], preferred_element_type=F32)
        logits = (both[:, 0:LANES] + both[:, LANES:2 * LANES]
                  + jnp.dot(xl, wr_ref[:, 0:LANES], preferred_element_type=F32))
        logits = jnp.where(valid, logits, NEG_INF)
        ex = jnp.exp(logits - jnp.max(logits, axis=1, keepdims=True))
        aff_ref[rs, :] = ex / jnp.sum(ex, axis=1, keepdims=True)


def _outproj(x2, conv_o, dil_o, mem_o, w_all, layer, g, wr_pieces, n_exp):
    n, d = x2.shape
    assert d == TOKEN_TILE_ROWS * LANES
    row = lambda i: (i, 0)
    fixed = lambda i: (0, 0)
    return pl.pallas_call(
        functools.partial(_outproj_kernel, n_exp=n_exp),
        grid=(n // PROJ_ROW_TILE,),
        in_specs=[pl.BlockSpec((PROJ_ROW_TILE, d), row),
                  pl.BlockSpec((PROJ_ROW_TILE, conv_o.shape[1]), row),
                  pl.BlockSpec((PROJ_ROW_TILE, dil_o.shape[1]), row),
                  pl.BlockSpec((PROJ_ROW_TILE, mem_o.shape[1]), row),
                  pl.BlockSpec((1,) + w_all.shape[1:], lambda i: (layer, 0, 0)),
                  pl.BlockSpec((1, d), fixed),
                  pl.BlockSpec((d, 2 * LANES), fixed)],
        out_specs=[pl.BlockSpec((PROJ_ROW_TILE, d), row),
                   pl.BlockSpec((PROJ_ROW_TILE * TOKEN_TILE_ROWS, LANES), row),
                   pl.BlockSpec((PROJ_ROW_TILE, LANES), row)],
        out_shape=[jax.ShapeDtypeStruct((n, d), F32),
                   jax.ShapeDtypeStruct((n * TOKEN_TILE_ROWS, LANES), F32),
                   jax.ShapeDtypeStruct((n, LANES), F32)],
        scratch_shapes=[pltpu.VMEM(w_all.shape[1:], BF16)],
        compiler_params=_params("arbitrary"),
        name="outproj_router",
    )(x2, conv_o, dil_o, mem_o, w_all, g, wr_pieces)


TOK_DIGIT_BITS = 6
PICK_GATE_LANE0 = 2


def _fill_strict_upper(tri):
    s = tri.shape[0]
    chunk = 2 * LANES
    for c in range(0, s, chunk):
        src = lax.broadcasted_iota(jnp.int32, (chunk, s), 0) + c
        dst = lax.broadcasted_iota(jnp.int32, (chunk, s), 1)
        tri[c:c + chunk, :] = jnp.where(src < dst, 1.0, 0.0).astype(BF16)


def _topk_kernel(aff_ref, afft_ref, slot_ref, pick_ref, idx_ref, rhs, tri, *, cap, n_batch, n_exp):
    aff = aff_ref[...]
    rows, seq = aff.shape
    _fill_strict_upper(tri)
    bits = jnp.zeros((rows, 1), jnp.int32)
    for bit in range(30, -1, -1):
        cand = bits | (1 << bit)
        cnt = jnp.sum(jnp.where(aff >= pltpu.bitcast(cand, F32), 1.0, 0.0), axis=1, keepdims=True)
        bits = jnp.where(cnt >= cap, cand, bits)
    thr = pltpu.bitcast(bits, F32)
    gt = aff > thr
    eq = aff == thr
    need = cap - jnp.sum(jnp.where(gt, 1.0, 0.0), axis=1, keepdims=True)
    rank_eq = jnp.dot(jnp.where(eq, 1.0, 0.0).astype(BF16), tri[...], preferred_element_type=F32)
    sel = jnp.where(gt, 1.0, jnp.where(eq & (rank_eq < need), 1.0, 0.0))
    rank = jnp.dot(sel.astype(BF16), tri[...], preferred_element_type=F32)
    slot_ref[...] = jnp.where(sel > 0.5, rank, -1.0).astype(jnp.int32)

    half = seq // 2
    lane = lax.broadcasted_iota(jnp.int32, (1, LANES), 1)
    tok = lax.broadcasted_iota(jnp.int32, (seq, LANES), 0)
    tok_part = jnp.where(lane == 0, tok >> TOK_DIGIT_BITS,
                         jnp.where(lane == 1, tok & ((1 << TOK_DIGIT_BITS) - 1),
                                   jnp.where(lane == LANES - 1, 1, 0))).astype(F32)
    src = lax.broadcasted_iota(jnp.int32, (LANES, LANES), 0)
    dst = lax.broadcasted_iota(jnp.int32, (LANES, LANES), 1)

    def place(piece):
        return jnp.where((src < n_exp) & (dst == src + PICK_GATE_LANE0 + piece * n_exp),
                         1.0, 0.0).astype(BF16)

    for bi in range(n_batch):
        a = afft_ref[bi * seq:(bi + 1) * seq, :]
        a_hi = a.astype(BF16)
        a_lo = (a - a_hi.astype(F32)).astype(BF16)
        full = (tok_part
                + jnp.dot(a_hi, place(0), preferred_element_type=F32)
                + jnp.dot(a_lo, place(1), preferred_element_type=F32)).astype(BF16)
        rhs[bi, :, 0:LANES] = full[0:half]
        rhs[bi, :, LANES:2 * LANES] = full[half:seq]

    sidx = lax.broadcasted_iota(jnp.int32, (cap, half), 0)

    def invert(r, carry):
        bi = lax.rem(r, n_batch)
        srow = slot_ref[pl.ds(r, 1), :]
        onehot = jnp.where(srow[:, 0:half] == sidx, 1.0,
                           jnp.where(srow[:, half:seq] == sidx, 2.0, 0.0)).astype(BF16)
        both = jnp.dot(onehot, rhs[bi], preferred_element_type=F32)
        lower = both[:, LANES - 1:LANES] < 1.5
        pick = jnp.where(lower, both[:, 0:LANES], both[:, LANES:2 * LANES] * 0.5)
        pick_ref[pl.ds(pl.multiple_of(r * cap, cap), cap), :] = pick
        digits = jnp.transpose(pick)
        token = (digits[0:1] * (1 << TOK_DIGIT_BITS) + digits[1:2]).astype(jnp.int32)
        idx_ref[pl.ds(r, 1), :] = (token + bi * seq) * TOKEN_TILE_ROWS
        return carry

    lax.fori_loop(0, rows, invert, 0, unroll=8)


def _topk(aff_rows, aff_tok, cap, n_batch, n_exp):
    rows, seq = aff_rows.shape
    return pl.pallas_call(
        functools.partial(_topk_kernel, cap=cap, n_batch=n_batch, n_exp=n_exp),
        out_shape=[jax.ShapeDtypeStruct(aff_rows.shape, jnp.int32),
                   jax.ShapeDtypeStruct((rows * cap, LANES), F32),
                   jax.ShapeDtypeStruct((rows, cap), jnp.int32)],
        scratch_shapes=[pltpu.VMEM((n_batch, seq // 2, 2 * LANES), BF16),
                        pltpu.VMEM((seq, seq), BF16)],
        compiler_params=pltpu.CompilerParams(vmem_limit_bytes=V7X_VMEM_LIMIT_BYTES),
        name="expert_topk",
    )(aff_rows, aff_tok)


FF_CHUNK = 512
EXPERT_ROWS = 1024


def _expert_kernel(idx_ref, xn_hbm, wg_ref, wu_ref, wd_ref, pick_ref, y_ref,
                   xbuf, sem, acc, wgb, wub, wdb, *, cap, n_f):
    e = pl.program_id(0)
    f = pl.program_id(1)
    n_exp = pl.num_programs(0)
    rows = acc.shape[0]
    slot = lax.rem(e, 2)
    nxt = 1 - slot
    e_nxt = jnp.minimum(e + 1, n_exp - 1)

    tile = TOKEN_TILE_ROWS

    def row_copy(src_tile_row, buf, dst_row):
        return pltpu.make_async_copy(
            xn_hbm.at[pl.ds(pl.multiple_of(src_tile_row, tile), tile), :],
            xbuf.at[pl.ds(pl.multiple_of((buf * rows + dst_row) * tile, tile), tile), :], sem.at[buf])

    def wait_rows(buf):
        pltpu.make_async_copy(xn_hbm.at[pl.ds(0, rows * tile), :],
                              xbuf.at[pl.ds(pl.multiple_of(buf * rows * tile, tile), rows * tile), :],
                              sem.at[buf]).wait()

    @pl.when((e == 0) & (f == 0))
    def _():
        def first(j, carry):
            row_copy(idx_ref[j], 0, j).start()
            return carry
        lax.fori_loop(0, rows, first, 0, unroll=8)

    @pl.when(f == 0)
    def _():
        wait_rows(slot)
        acc[...] = jnp.zeros(acc.shape, F32)

    wgb[...] = wg_ref[0, 0].astype(BF16)
    wub[...] = wu_ref[0, 0].astype(BF16)
    wdb[...] = wd_ref[0, 0].astype(BF16)

    per_step = rows // n_f
    per_chunk = per_step // (rows // EXPERT_ROWS)
    base = pl.multiple_of(f * per_step, per_step)
    idx_base = e_nxt * rows + base
    for ci, r0 in enumerate(range(0, rows, EXPERT_ROWS)):
        for k in range(ci * per_chunk, (ci + 1) * per_chunk):
            row_copy(idx_ref[idx_base + k], nxt, base + k).start()
        rsl = slice(r0, r0 + EXPERT_ROWS)
        xm = jnp.concatenate(
            [xbuf[pl.ds((slot * rows + r0) * tile + j, EXPERT_ROWS, stride=tile), :]
             for j in range(tile)], axis=1).astype(BF16)
        gate = jnp.dot(xm, wgb[...], preferred_element_type=F32)
        up = jnp.dot(xm, wub[...], preferred_element_type=F32)
        hid = (gate * (1.0 / (1.0 + jnp.exp(-gate))) * up).astype(BF16)
        acc[rsl, :] += jnp.dot(hid, wdb[...], preferred_element_type=F32)

    @pl.when(f == n_f - 1)
    def _():
        lane = lax.broadcasted_iota(jnp.int32, (1, LANES), 1) - (PICK_GATE_LANE0 + e)
        mine = (lane == 0) | (lane == n_exp)
        for r0 in range(0, rows, EXPERT_ROWS):
            rsl = slice(r0, r0 + EXPERT_ROWS)
            gate_col = jnp.sum(jnp.where(mine, pick_ref[rsl, :], 0.0), axis=1, keepdims=True)
            y_ref[0, rsl, :] = (acc[rsl, :] * gate_col).astype(BF16)

    @pl.when((e == n_exp - 1) & (f == n_f - 1))
    def _():
        wait_rows(nxt)


def _experts(idx2, xn_tiles, w_gate, w_up, w_down, layer, picked, cap):
    n_exp, rows = idx2.shape
    d = w_gate.shape[2]
    d_ff = w_gate.shape[3]
    n_f = d_ff // FF_CHUNK
    assert rows % (n_f * (rows // EXPERT_ROWS)) == 0
    return pl.pallas_call(
        functools.partial(_expert_kernel, cap=cap, n_f=n_f),
        grid_spec=pltpu.PrefetchScalarGridSpec(
            num_scalar_prefetch=1,
            grid=(n_exp, n_f),
            in_specs=[pl.BlockSpec(memory_space=pl.ANY),
                      pl.BlockSpec((1, 1, d, FF_CHUNK), lambda e, f, ix: (layer, e, 0, f)),
                      pl.BlockSpec((1, 1, d, FF_CHUNK), lambda e, f, ix: (layer, e, 0, f)),
                      pl.BlockSpec((1, 1, FF_CHUNK, d), lambda e, f, ix: (layer, e, f, 0)),
                      pl.BlockSpec((rows, LANES), lambda e, f, ix: (e, 0))],
            out_specs=pl.BlockSpec((1, rows, d), lambda e, f, ix: (e, 0, 0)),
            scratch_shapes=[pltpu.VMEM((2 * rows * TOKEN_TILE_ROWS, LANES), F32),
                            pltpu.SemaphoreType.DMA((2,)),
                            pltpu.VMEM((rows, d), F32),
                            pltpu.VMEM((d, FF_CHUNK), BF16),
                            pltpu.VMEM((d, FF_CHUNK), BF16),
                            pltpu.VMEM((FF_CHUNK, d), BF16)]),
        out_shape=jax.ShapeDtypeStruct((n_exp, rows, d), BF16),
        compiler_params=_params("arbitrary", "arbitrary"),
        name="expert_ffn",
    )(idx2.reshape(-1), xn_tiles, w_gate, w_up, w_down, picked)


COMBINE_GROUP = 4
COMBINE_WINDOW = 128
BF16_ROWS = 16


def _combine_kernel(win_ref, fits_ref, x_ref, slott_ref, y_ref, g_ref, o_ref, *, cap, final):
    bi = pl.program_id(0)
    ti = pl.program_id(1)
    tile_id = bi * pl.num_programs(1) + ti
    slott = slott_ref[0]
    t, n_exp = slott.shape
    d = x_ref.shape[2]

    def finish(acc):
        if final:
            acc = _rms(acc, g_ref[...])
        o_ref[0] = acc

    @pl.when(fits_ref[tile_id] != 0)
    def _():
        lane = lax.broadcasted_iota(jnp.int32, (t, COMBINE_WINDOW), 1)
        acc = x_ref[0]
        for e0 in range(0, n_exp, COMBINE_GROUP):
            parts, ys = [], []
            for e in range(e0, e0 + COMBINE_GROUP):
                start = pl.multiple_of(win_ref[tile_id * n_exp + e], BF16_ROWS)
                parts.append(jnp.where(slott[:, e:e + 1] - start == lane, 1.0, 0.0).astype(BF16))
                ys.append(y_ref[e, 0, pl.ds(start, COMBINE_WINDOW), :])
            acc = acc + jnp.dot(jnp.concatenate(parts, axis=1), jnp.concatenate(ys, axis=0),
                                preferred_element_type=F32)
        finish(acc)

    @pl.when(fits_ref[tile_id] == 0)
    def _():
        sidx = lax.broadcasted_iota(jnp.int32, (t, cap), 1)
        acc = x_ref[0]
        for e0 in range(0, n_exp, COMBINE_GROUP):
            parts = [jnp.where(slott[:, e:e + 1] == sidx, 1.0, 0.0).astype(BF16)
                     for e in range(e0, e0 + COMBINE_GROUP)]
            yg = y_ref[e0:e0 + COMBINE_GROUP, 0].reshape(COMBINE_GROUP * cap, d)
            acc = acc + jnp.dot(jnp.concatenate(parts, axis=1), yg, preferred_element_type=F32)
        finish(acc)


def _combine_windows(slot_rows, n_exp, b, s, cap):
    nt = s // ROW_TILE
    cnt = jnp.sum((slot_rows >= 0).reshape(n_exp, b, nt, ROW_TILE), axis=-1, dtype=jnp.int32)
    end = jnp.cumsum(cnt, axis=-1)
    start = jnp.minimum((end - cnt) // BF16_ROWS * BF16_ROWS, cap - COMBINE_WINDOW)
    fits = jnp.all(end - start <= COMBINE_WINDOW, axis=0)
    return jnp.transpose(start, (1, 2, 0)).reshape(-1), fits.reshape(-1).astype(jnp.int32)


def _combine(x3, slott, y4, g, win, fits, cap, final):
    b, s, d = x3.shape
    n_exp = y4.shape[0]
    assert cap >= COMBINE_WINDOW and cap % BF16_ROWS == 0
    return pl.pallas_call(
        functools.partial(_combine_kernel, cap=cap, final=final),
        grid_spec=pltpu.PrefetchScalarGridSpec(
            num_scalar_prefetch=2,
            grid=(b, s // ROW_TILE),
            in_specs=[pl.BlockSpec((1, ROW_TILE, d), lambda i, t, w, f: (i, t, 0)),
                      pl.BlockSpec((1, ROW_TILE, n_exp), lambda i, t, w, f: (i, t, 0)),
                      pl.BlockSpec((n_exp, 1, cap, d), lambda i, t, w, f: (0, i, 0, 0)),
                      pl.BlockSpec((1, d), lambda i, t, w, f: (0, 0))],
            out_specs=pl.BlockSpec((1, ROW_TILE, d), lambda i, t, w, f: (i, t, 0))),
        out_shape=jax.ShapeDtypeStruct((b, s, d), F32),
        compiler_params=_params("arbitrary", "arbitrary"),
        name="combine",
    )(win, fits, x3, slott, y4, g)


def kernel(x, mem, mem_norm, norm_mix, w_in, conv_w, w_mem_kv, w_out, norm_ffn,
           w_router, w_gate, w_up, w_down, norm_final):
    b, s, d = x.shape
    depth = w_in.shape[0]
    n_exp = w_router.shape[2]
    conv_ch = conv_w.shape[2]
    mem_width = w_mem_kv.shape[2] // 2
    dil_width = (w_in.shape[2] - 3 * conv_ch - mem_width) // 3
    n_dil_heads = dil_width // HEAD_DIM
    cap = EC_CAPACITY * s // n_exp
    assert n_exp == N_EXPERTS and mem_width == N_MEM_HEADS * HEAD_DIM
    assert all(w // (2 * dl) == HALF_WINDOW for w, dl in DILATED_PATTERNS)

    slopes = jnp.asarray([2.0 ** (-8.0 * (h + 1) / n_dil_heads) for h in range(n_dil_heads)], F32)
    kv_all = jnp.transpose(w_mem_kv, (1, 0, 2)).reshape(d, depth * 2 * mem_width).astype(BF16)
    kvm3 = _memkv(mem.reshape(-1, d), mem_norm.reshape(1, d), kv_all).reshape(b, mem.shape[1], -1)

    x2 = x.reshape(b * s, d)
    out = None
    for l in range(depth):
        conv_o, hq = _inproj(x2, norm_mix[l].reshape(1, d), w_in, l, conv_w[l], conv_ch, s)
        hq3 = hq.reshape(b, s, -1)
        dil_o = _dilated(hq3, slopes, dil_width)
        mem_o = _memattn(hq3, kvm3, l, 3 * dil_width // LANES, mem_width)
        wr = jnp.pad(w_router[l], ((0, 0), (0, LANES - n_exp)))
        wr_hi = wr.astype(BF16)
        wr_pieces = jnp.concatenate([wr_hi, (wr - wr_hi.astype(F32)).astype(BF16)], axis=1)
        x1, xn, aff_tok = _outproj(x2, conv_o, dil_o.reshape(b * s, -1),
                                   mem_o.reshape(b * s, -1), w_out, l,
                                   norm_ffn[l].reshape(1, d), wr_pieces, n_exp)
        aff_rows = jnp.transpose(aff_tok[:, :n_exp]).reshape(n_exp * b, s)
        slot_rows, picked, idx_rows = _topk(aff_rows, aff_tok, cap, b, n_exp)
        y = _experts(idx_rows.reshape(n_exp, b * cap), xn, w_gate, w_up, w_down, l, picked, cap)
        slott = jnp.transpose(slot_rows.reshape(n_exp, b, s), (1, 2, 0))
        win, fits = _combine_windows(slot_rows, n_exp, b, s, cap)
        final = l == depth - 1
        x3 = _combine(x1.reshape(b, s, d), slott, y.reshape(n_exp, b, cap, d),
                      norm_final.reshape(1, d), win, fits, cap, final)
        x2 = x3.reshape(b * s, d)
        out = x3
    return out
```

```python
import functools

import jax
import jax.numpy as jnp
from jax import lax
from jax.experimental import pallas as pl
from jax.experimental.pallas import tpu as pltpu

F32 = jnp.float32
BF16 = jnp.bfloat16

HEAD_DIM = 64
N_MEM_HEADS = 4
DILATED_PATTERNS = ((128, 1), (512, 4), (2048, 16))
HALF_WINDOW = 64
N_EXPERTS = 16
EC_CAPACITY = 2
RMS_EPS = 1e-6
NEG_INF = -1e30

LANES = 128
ROW_TILE = 512
PROJ_ROW_TILE = 1024
ATT_TILE = 128
V7X_VMEM_LIMIT_BYTES = 56 * 1024 * 1024


def _params(*semantics):
    return pltpu.CompilerParams(dimension_semantics=semantics,
                                vmem_limit_bytes=V7X_VMEM_LIMIT_BYTES)


def _rms(x, g):
    ms = jnp.mean(x * x, axis=-1, keepdims=True)
    return x * lax.rsqrt(ms + RMS_EPS) * g


def _nt_dot(a, b, **kw):
    return lax.dot_general(a, b, (((1,), (1,)), ((), ())), preferred_element_type=F32, **kw)


HALO_ROWS = 8


def _inproj_kernel(x_ref, xprev_ref, xnext_ref, g_ref, wf_ref, cw_ref, conv_ref, hq_ref, hc, w_ref,
                   *, conv_ch, tiles_per_seq):
    @pl.when(pl.program_id(0) == 0)
    def _():
        w_ref[...] = wf_ref[0].astype(BF16)

    t = x_ref.shape[0]
    n_conv = 3 * conv_ch
    g = g_ref[...]
    halo = jnp.concatenate([xprev_ref[...], xnext_ref[...]], axis=0)
    xa = jnp.concatenate([_rms(x_ref[...], g), _rms(halo, g)], axis=0).astype(BF16)
    chunk = 2 * LANES
    for c in range(0, w_ref.shape[1], chunk):
        if c >= n_conv:
            hq_ref[:, c - n_conv:c - n_conv + chunk] = jnp.dot(
                xa[0:t], w_ref[:, c:c + chunk], preferred_element_type=F32).astype(BF16)
            continue
        h = jnp.dot(xa, w_ref[:, c:c + chunk], preferred_element_type=F32)
        if c + chunk <= n_conv:
            hc[:, c:c + chunk] = h
        else:
            k = n_conv - c
            hc[:, c:n_conv] = h[:, :k]
            hq_ref[:, 0:chunk - k] = h[0:t, k:].astype(BF16)

    u_all = hc[:, 2 * conv_ch:3 * conv_ch] * hc[:, 0:conv_ch]
    u = u_all[0:t]
    pos = lax.rem(pl.program_id(0), tiles_per_seq)
    u_before = jnp.where(pos == 0, 0.0, u_all[t + HALO_ROWS - 1:t + HALO_ROWS])
    u_after = jnp.where(pos == tiles_per_seq - 1, 0.0, u_all[t + HALO_ROWS:t + HALO_ROWS + 1])
    row = lax.broadcasted_iota(jnp.int32, u.shape, 0)
    prev = jnp.where(row == 0, u_before, pltpu.roll(u, 1, 0))
    nxt = jnp.where(row == t - 1, u_after, pltpu.roll(u, t - 1, 0))
    w = cw_ref[...]
    conv = w[0:1] * prev + w[1:2] * u + w[2:3] * nxt
    conv_ref[...] = (hc[0:t, conv_ch:2 * conv_ch] * conv).astype(BF16)


def _inproj(x2, g, w_all, layer, conv_w, conv_ch, seq):
    n, d = x2.shape
    n_all = w_all.shape[2]
    n_conv = 3 * conv_ch
    per_tile = PROJ_ROW_TILE // HALO_ROWS
    last = n // HALO_ROWS - 1
    assert seq % PROJ_ROW_TILE == 0
    return pl.pallas_call(
        functools.partial(_inproj_kernel, conv_ch=conv_ch, tiles_per_seq=seq // PROJ_ROW_TILE),
        grid=(n // PROJ_ROW_TILE,),
        in_specs=[pl.BlockSpec((PROJ_ROW_TILE, d), lambda i: (i, 0)),
                  pl.BlockSpec((HALO_ROWS, d), lambda i: (jnp.maximum(i * per_tile - 1, 0), 0)),
                  pl.BlockSpec((HALO_ROWS, d), lambda i: (jnp.minimum((i + 1) * per_tile, last), 0)),
                  pl.BlockSpec((1, d), lambda i: (0, 0)),
                  pl.BlockSpec((1, d, n_all), lambda i: (layer, 0, 0)),
                  pl.BlockSpec(conv_w.shape, lambda i: (0, 0))],
        out_specs=[pl.BlockSpec((PROJ_ROW_TILE, conv_ch), lambda i: (i, 0)),
                   pl.BlockSpec((PROJ_ROW_TILE, n_all - n_conv), lambda i: (i, 0))],
        out_shape=[jax.ShapeDtypeStruct((n, conv_ch), BF16),
                   jax.ShapeDtypeStruct((n, n_all - n_conv), BF16)],
        scratch_shapes=[pltpu.VMEM((PROJ_ROW_TILE + 2 * HALO_ROWS, n_conv), F32),
                        pltpu.VMEM((d, n_all), BF16)],
        compiler_params=_params("arbitrary"),
        name="inproj_conv",
    )(x2, x2, x2, g, w_all, conv_w)


DIL_TILE_UNROLL = 16


def _dil_cases(seq, dil):
    tps = (seq // dil) // ATT_TILE
    return tps, (("single",) if tps == 1 else ("first", "interior", "last"))


def _dil_kernel(slopes_ref, q_ref, k_ref, v_ref, o_ref,
                natq, natk, natv, perq, perk, perv, bias,
                num1, mx1, sum1, num2, mx2, sum2, num3, mx3, sum3, onat, *operands):
    seq = q_ref.shape[1]
    pad = HALF_WINDOW
    tq = ATT_TILE
    step = DILATED_PATTERNS[1][1]
    seg2 = seq // step
    seg3 = seg2 // step
    log_seg2 = seg2.bit_length() - 1
    log_step = step.bit_length() - 1
    pair = pl.program_id(0)
    lane = lax.broadcasted_iota(jnp.int32, (1, LANES), 1)
    head0 = lane < HEAD_DIM
    m0 = head0.astype(F32)
    m1 = 1.0 - m0

    @pl.when(pl.program_id(1) == 0)
    def _():
        for pi in range(len(DILATED_PATTERNS)):
            _, _, kp, vp = operands[4 * pi:4 * pi + 4]
            kp[0:pad, :] = jnp.zeros((pad, LANES), BF16)
            kp[pad + seq:pad + seq + pad, :] = jnp.zeros((pad, LANES), BF16)
            vp[0:pad, :] = jnp.zeros((pad, 2 * LANES), BF16)
            vp[pad + seq:pad + seq + pad, :] = jnp.zeros((pad, 2 * LANES), BF16)
            vp[pad:pad + seq, LANES:2 * LANES] = jnp.ones((seq, LANES), BF16)
        row = lax.broadcasted_iota(jnp.int32, (2 * tq, 2 * tq), 0)
        col = lax.broadcasted_iota(jnp.int32, (2 * tq, 2 * tq), 1)
        kc = col - pad
        arel = jnp.abs(kc - (row & (tq - 1)))
        band = arel <= HALF_WINDOW
        slope_rows = jnp.where(row < tq, slopes_ref[2 * pair], slopes_ref[2 * pair + 1])
        idx = 0
        for _, dil in DILATED_PATTERNS:
            dist_bias = -slope_rows * (arel * dil).astype(F32)
            for case in _dil_cases(seq, dil)[1]:
                valid = band
                if case in ("first", "single"):
                    valid = valid & (kc >= 0)
                if case in ("last", "single"):
                    valid = valid & (kc < tq)
                bias[idx] = jnp.where(valid, dist_bias, NEG_INF)
                idx += 1

    natq[...] = q_ref[0].astype(F32) * (HEAD_DIM ** -0.5)
    natk[...] = k_ref[0].astype(F32)
    natv[...] = v_ref[0].astype(F32)

    def set_operands(pi, rows, qv, kv, vv):
        qp0, qp1, kp, vp = operands[4 * pi:4 * pi + 4]
        lo, hi = rows
        qp0[lo:hi, :] = (qv * m0).astype(BF16)
        qp1[lo:hi, :] = (qv * m1).astype(BF16)
        kp[pad + lo:pad + hi, :] = kv.astype(BF16)
        vp[pad + lo:pad + hi, 0:LANES] = vv.astype(BF16)

    def run_tiles(pi, case0, tps, dests):
        num_ref, mx_ref, sum_ref = dests
        qp0, qp1, kp, vp = operands[4 * pi:4 * pi + 4]

        def tile(m, carry):
            r0 = pl.multiple_of(m * tq, tq)
            qc = jnp.concatenate([qp0[pl.ds(r0, tq), :], qp1[pl.ds(r0, tq), :]], axis=0)
            kt = kp[pl.ds(r0, 2 * tq), :]
            vt = vp[pl.ds(r0, 2 * tq), :]
            if tps == 1:
                case = case0
                dst = pl.ds((m >> log_step) * seg2 + (m & (step - 1)), tq, stride=step)
            else:
                pos = m & (tps - 1)
                case = case0 + jnp.where(pos == 0, 0, jnp.where(pos == tps - 1, 2, 1))
                dst = pl.ds(r0, tq)
            s = _nt_dot(qc, kt) + bias[case]
            mx = jnp.max(s, axis=-1, keepdims=True)
            pe = jnp.exp(s - mx)
            pv = jnp.dot(pe.astype(BF16), vt, preferred_element_type=F32)
            num_ref[dst, :] = jnp.where(head0, pv[0:tq, 0:LANES], pv[tq:2 * tq, 0:LANES])
            sum_ref[dst, :] = jnp.where(head0, pv[0:tq, LANES:2 * LANES],
                                        pv[tq:2 * tq, LANES:2 * LANES])
            mx_ref[dst, :] = jnp.where(head0, mx[0:tq], mx[tq:2 * tq])
            return carry

        lax.fori_loop(0, seq // tq, tile, 0, unroll=DIL_TILE_UNROLL)

    case0 = 0
    set_operands(0, (0, seq), natq[...], natk[...], natv[...])
    tps1, cases1 = _dil_cases(seq, DILATED_PATTERNS[0][1])
    run_tiles(0, case0, tps1, (num1, mx1, sum1))
    case0 += len(cases1)

    for b in range(step):
        rows = (b * seg2, (b + 1) * seg2)
        qv = natq[pl.ds(b, seg2, stride=step), :]
        kv = natk[pl.ds(b, seg2, stride=step), :]
        vv = natv[pl.ds(b, seg2, stride=step), :]
        perq[rows[0]:rows[1], :] = qv
        perk[rows[0]:rows[1], :] = kv
        perv[rows[0]:rows[1], :] = vv
        set_operands(1, rows, qv, kv, vv)
    tps2, cases2 = _dil_cases(seq, DILATED_PATTERNS[1][1])
    run_tiles(1, case0, tps2, (num2, mx2, sum2))
    case0 += len(cases2)

    for sgm in range(step * step):
        b, a = divmod(sgm, step)
        src = pl.ds(b * seg2 + a, seg3, stride=step)
        set_operands(2, (sgm * seg3, (sgm + 1) * seg3), perq[src, :], perk[src, :], perv[src, :])
    tps3, _ = _dil_cases(seq, DILATED_PATTERNS[2][1])
    assert tps3 == 1 and seg3 == tq
    run_tiles(2, case0, tps3, (num3, mx3, sum3))

    def merge(c, carry):
        p0 = pl.multiple_of(c * tq, tq)
        per_rows = pl.ds(p0, tq)
        nat_rows = pl.ds((p0 & (seg2 - 1)) * step + (p0 >> log_seg2), tq, stride=step)
        a1, a2, a3 = mx1[nat_rows, :], mx2[per_rows, :], mx3[per_rows, :]
        top = jnp.maximum(jnp.maximum(a1, a2), a3)
        e1, e2, e3 = jnp.exp(a1 - top), jnp.exp(a2 - top), jnp.exp(a3 - top)
        den = e1 * sum1[nat_rows, :] + e2 * sum2[per_rows, :] + e3 * sum3[per_rows, :]
        num = e1 * num1[nat_rows, :] + e2 * num2[per_rows, :] + e3 * num3[per_rows, :]
        onat[nat_rows, :] = num / den
        return carry

    lax.fori_loop(0, seq // tq, merge, 0, unroll=2)
    o_ref[0] = onat[...].astype(BF16)


def _dilated(hq3, slopes, dil_width):
    b, s, _ = hq3.shape
    npair = dil_width // LANES
    blk = (1, s, LANES)
    dils = [d for _, d in DILATED_PATTERNS]
    assert dils[0] == 1 and dils[2] == dils[1] * dils[1]
    n_cases = sum(len(_dil_cases(s, d)[1]) for d in dils)
    pad = HALF_WINDOW
    scratch = [pltpu.VMEM((s, LANES), F32)] * 6
    scratch += [pltpu.VMEM((n_cases, 2 * ATT_TILE, 2 * ATT_TILE), F32)]
    scratch += [pltpu.VMEM((s, LANES), F32)] * 10
    scratch += [pltpu.VMEM((s, LANES), BF16), pltpu.VMEM((s, LANES), BF16),
                pltpu.VMEM((s + 2 * pad, LANES), BF16),
                pltpu.VMEM((s + 2 * pad, 2 * LANES), BF16)] * len(dils)
    return pl.pallas_call(
        _dil_kernel,
        grid_spec=pltpu.PrefetchScalarGridSpec(
            num_scalar_prefetch=1,
            grid=(npair, b),
            in_specs=[pl.BlockSpec(blk, lambda j, i, sl: (i, 0, j)),
                      pl.BlockSpec(blk, lambda j, i, sl: (i, 0, npair + j)),
                      pl.BlockSpec(blk, lambda j, i, sl: (i, 0, 2 * npair + j))],
            out_specs=pl.BlockSpec(blk, lambda j, i, sl: (i, 0, j)),
            scratch_shapes=scratch),
        out_shape=jax.ShapeDtypeStruct((b, s, dil_width), BF16),
        compiler_params=_params("arbitrary", "arbitrary"),
        name="dilated_attn",
    )(slopes, hq3, hq3, hq3)


def _memkv_kernel(m_ref, g_ref, w_ref, o_ref):
    mn = _rms(m_ref[...], g_ref[...]).astype(BF16)
    o_ref[...] = jnp.dot(mn, w_ref[...], preferred_element_type=F32).astype(BF16)


def _memkv(mem2, g, w):
    n, d = mem2.shape
    nw = w.shape[1]
    return pl.pallas_call(
        _memkv_kernel,
        grid=(n // ROW_TILE,),
        in_specs=[pl.BlockSpec((ROW_TILE, d), lambda i: (i, 0)),
                  pl.BlockSpec((1, d), lambda i: (0, 0)),
                  pl.BlockSpec((d, nw), lambda i: (0, 0))],
        out_specs=pl.BlockSpec((ROW_TILE, nw), lambda i: (i, 0)),
        out_shape=jax.ShapeDtypeStruct((n, nw), BF16),
        compiler_params=_params("arbitrary"),
        name="mem_kv",
    )(mem2, g, w)


def _memattn_kernel(q_ref, km_ref, vm_ref, o_ref, qp0, qp1, vext):
    seq = q_ref.shape[1]
    tq = 2 * ATT_TILE
    lane = lax.broadcasted_iota(jnp.int32, (1, LANES), 1)
    head0 = lane < HEAD_DIM
    m0 = head0.astype(F32)
    q = q_ref[0].astype(F32) * (HEAD_DIM ** -0.5)
    qp0[...] = (q * m0).astype(BF16)
    qp1[...] = (q * (1.0 - m0)).astype(BF16)
    vext[:, 0:LANES] = vm_ref[0]
    vext[:, LANES:2 * LANES] = jnp.ones((vext.shape[0], LANES), BF16)
    km = km_ref[0]

    def tile(m, carry):
        rows = pl.ds(pl.multiple_of(m * tq, tq), tq)
        qc = jnp.concatenate([qp0[rows, :], qp1[rows, :]], axis=0)
        s = _nt_dot(qc, km)
        pe = jnp.exp(s - jnp.max(s, axis=-1, keepdims=True))
        pv = jnp.dot(pe.astype(BF16), vext[...], preferred_element_type=F32)
        o = pv[:, 0:LANES] / pv[:, LANES:2 * LANES]
        o_ref[0, rows, :] = jnp.where(head0, o[0:tq], o[tq:2 * tq]).astype(BF16)
        return carry

    lax.fori_loop(0, seq // tq, tile, 0, unroll=True)


def _memattn(hq3, kvm3, layer, qm_block0, mem_width):
    b, s, _ = hq3.shape
    m = kvm3.shape[1]
    npair = mem_width // LANES
    kv_blocks = 2 * npair
    return pl.pallas_call(
        _memattn_kernel,
        grid=(b, npair),
        in_specs=[pl.BlockSpec((1, s, LANES), lambda i, j: (i, 0, qm_block0 + j)),
                  pl.BlockSpec((1, m, LANES), lambda i, j: (i, 0, layer * kv_blocks + j)),
                  pl.BlockSpec((1, m, LANES), lambda i, j: (i, 0, layer * kv_blocks + npair + j))],
        out_specs=pl.BlockSpec((1, s, LANES), lambda i, j: (i, 0, j)),
        out_shape=jax.ShapeDtypeStruct((b, s, mem_width), BF16),
        scratch_shapes=[pltpu.VMEM((s, LANES), BF16), pltpu.VMEM((s, LANES), BF16),
                        pltpu.VMEM((m, 2 * LANES), BF16)],
        compiler_params=_params("arbitrary", "arbitrary"),
        name="mem_attn",
    )(hq3, kvm3, kvm3)


OUT_SUB_ROWS = 512
TOKEN_TILE_ROWS = 8


def _outproj_kernel(x_ref, c_ref, d_ref, m_ref, wf_ref, g_ref, wr_ref,
                    x1_ref, xn_ref, aff_ref, w_ref, *, n_exp):
    @pl.when(pl.program_id(0) == 0)
    def _():
        w_ref[...] = wf_ref[0].astype(BF16)

    valid = lax.broadcasted_iota(jnp.int32, (1, LANES), 1) < n_exp
    for r0 in range(0, x_ref.shape[0], OUT_SUB_ROWS):
        rs = slice(r0, r0 + OUT_SUB_ROWS)
        cat = jnp.concatenate([c_ref[rs, :], d_ref[rs, :], m_ref[rs, :]], axis=1)
        x1 = x_ref[rs, :] + jnp.dot(cat, w_ref[...], preferred_element_type=F32)
        x1_ref[rs, :] = x1
        xn = _rms(x1, g_ref[...])
        for j in range(xn.shape[1] // LANES):
            xn_ref[pl.ds(r0 * TOKEN_TILE_ROWS + j, OUT_SUB_ROWS, stride=TOKEN_TILE_ROWS), :] = (
                xn[:, j * LANES:(j + 1) * LANES])
        xh = xn.astype(BF16)
        xl = (xn - xh.astype(F32)).astype(BF16)
        both = jnp.dot(xh, wr_ref[...], preferred_element_type=F32)
        logits = (both[:, 0:LANES] + both[:, LANES:2 * LANES]
                  + jnp.dot(xl, wr_ref[:, 0:LANES], preferred_element_type=F32))
        logits = jnp.where(valid, logits, NEG_INF)
        ex = jnp.exp(logits - jnp.max(logits, axis=1, keepdims=True))
        aff_ref[rs, :] = ex / jnp.sum(ex, axis=1, keepdims=True)


def _outproj(x2, conv_o, dil_o, mem_o, w_all, layer, g, wr_pieces, n_exp):
    n, d = x2.shape
    assert d == TOKEN_TILE_ROWS * LANES
    row = lambda i: (i, 0)
    fixed = lambda i: (0, 0)
    return pl.pallas_call(
        functools.partial(_outproj_kernel, n_exp=n_exp),
        grid=(n // PROJ_ROW_TILE,),
        in_specs=[pl.BlockSpec((PROJ_ROW_TILE, d), row),
                  pl.BlockSpec((PROJ_ROW_TILE, conv_o.shape[1]), row),
                  pl.BlockSpec((PROJ_ROW_TILE, dil_o.shape[1]), row),
                  pl.BlockSpec((PROJ_ROW_TILE, mem_o.shape[1]), row),
                  pl.BlockSpec((1,) + w_all.shape[1:], lambda i: (layer, 0, 0)),
                  pl.BlockSpec((1, d), fixed),
                  pl.BlockSpec((d, 2 * LANES), fixed)],
        out_specs=[pl.BlockSpec((PROJ_ROW_TILE, d), row),
                   pl.BlockSpec((PROJ_ROW_TILE * TOKEN_TILE_ROWS, LANES), row),
                   pl.BlockSpec((PROJ_ROW_TILE, LANES), row)],
        out_shape=[jax.ShapeDtypeStruct((n, d), F32),
                   jax.ShapeDtypeStruct((n * TOKEN_TILE_ROWS, LANES), F32),
                   jax.ShapeDtypeStruct((n, LANES), F32)],
        scratch_shapes=[pltpu.VMEM(w_all.shape[1:], BF16)],
        compiler_params=_params("arbitrary"),
        name="outproj_router",
    )(x2, conv_o, dil_o, mem_o, w_all, g, wr_pieces)


TOK_DIGIT_BITS = 6
PICK_GATE_LANE0 = 2


def _fill_strict_upper(tri):
    s = tri.shape[0]
    chunk = 2 * LANES
    for c in range(0, s, chunk):
        src = lax.broadcasted_iota(jnp.int32, (chunk, s), 0) + c
        dst = lax.broadcasted_iota(jnp.int32, (chunk, s), 1)
        tri[c:c + chunk, :] = jnp.where(src < dst, 1.0, 0.0).astype(BF16)


def _topk_kernel(aff_ref, afft_ref, slot_ref, pick_ref, idx_ref, rhs, tri, *, cap, n_batch, n_exp):
    aff = aff_ref[...]
    rows, seq = aff.shape
    _fill_strict_upper(tri)
    bits = jnp.zeros((rows, 1), jnp.int32)
    for bit in range(30, -1, -1):
        cand = bits | (1 << bit)
        cnt = jnp.sum(jnp.where(aff >= pltpu.bitcast(cand, F32), 1.0, 0.0), axis=1, keepdims=True)
        bits = jnp.where(cnt >= cap, cand, bits)
    thr = pltpu.bitcast(bits, F32)
    gt = aff > thr
    eq = aff == thr
    need = cap - jnp.sum(jnp.where(gt, 1.0, 0.0), axis=1, keepdims=True)
    rank_eq = jnp.dot(jnp.where(eq, 1.0, 0.0).astype(BF16), tri[...], preferred_element_type=F32)
    sel = jnp.where(gt, 1.0, jnp.where(eq & (rank_eq < need), 1.0, 0.0))
    rank = jnp.dot(sel.astype(BF16), tri[...], preferred_element_type=F32)
    slot_ref[...] = jnp.where(sel > 0.5, rank, -1.0).astype(jnp.int32)

    half = seq // 2
    lane = lax.broadcasted_iota(jnp.int32, (1, LANES), 1)
    tok = lax.broadcasted_iota(jnp.int32, (seq, LANES), 0)
    tok_part = jnp.where(lane == 0, tok >> TOK_DIGIT_BITS,
                         jnp.where(lane == 1, tok & ((1 << TOK_DIGIT_BITS) - 1),
                                   jnp.where(lane == LANES - 1, 1, 0))).astype(F32)
    src = lax.broadcasted_iota(jnp.int32, (LANES, LANES), 0)
    dst = lax.broadcasted_iota(jnp.int32, (LANES, LANES), 1)

    def place(piece):
        return jnp.where((src < n_exp) & (dst == src + PICK_GATE_LANE0 + piece * n_exp),
                         1.0, 0.0).astype(BF16)

    for bi in range(n_batch):
        a = afft_ref[bi * seq:(bi + 1) * seq, :]
        a_hi = a.astype(BF16)
        a_lo = (a - a_hi.astype(F32)).astype(BF16)
        full = (tok_part
                + jnp.dot(a_hi, place(0), preferred_element_type=F32)
                + jnp.dot(a_lo, place(1), preferred_element_type=F32)).astype(BF16)
        rhs[bi, :, 0:LANES] = full[0:half]
        rhs[bi, :, LANES:2 * LANES] = full[half:seq]

    sidx = lax.broadcasted_iota(jnp.int32, (cap, half), 0)

    def invert(r, carry):
        bi = lax.rem(r, n_batch)
        srow = slot_ref[pl.ds(r, 1), :]
        onehot = jnp.where(srow[:, 0:half] == sidx, 1.0,
                           jnp.where(srow[:, half:seq] == sidx, 2.0, 0.0)).astype(BF16)
        both = jnp.dot(onehot, rhs[bi], preferred_element_type=F32)
        lower = both[:, LANES - 1:LANES] < 1.5
        pick = jnp.where(lower, both[:, 0:LANES], both[:, LANES:2 * LANES] * 0.5)
        pick_ref[pl.ds(pl.multiple_of(r * cap, cap), cap), :] = pick
        digits = jnp.transpose(pick)
        token = (digits[0:1] * (1 << TOK_DIGIT_BITS) + digits[1:2]).astype(jnp.int32)
        idx_ref[pl.ds(r, 1), :] = (token + bi * seq) * TOKEN_TILE_ROWS
        return carry

    lax.fori_loop(0, rows, invert, 0, unroll=8)


def _topk(aff_rows, aff_tok, cap, n_batch, n_exp):
    rows, seq = aff_rows.shape
    return pl.pallas_call(
        functools.partial(_topk_kernel, cap=cap, n_batch=n_batch, n_exp=n_exp),
        out_shape=[jax.ShapeDtypeStruct(aff_rows.shape, jnp.int32),
                   jax.ShapeDtypeStruct((rows * cap, LANES), F32),
                   jax.ShapeDtypeStruct((rows, cap), jnp.int32)],
        scratch_shapes=[pltpu.VMEM((n_batch, seq // 2, 2 * LANES), BF16),
                        pltpu.VMEM((seq, seq), BF16)],
        compiler_params=pltpu.CompilerParams(vmem_limit_bytes=V7X_VMEM_LIMIT_BYTES),
        name="expert_topk",
    )(aff_rows, aff_tok)


FF_CHUNK = 512
EXPERT_ROWS = 512


def _expert_kernel(idx_ref, xn_hbm, wg_ref, wu_ref, wd_ref, pick_ref, y_ref,
                   xbuf, sem, acc, wgb, wub, wdb, *, cap, n_f):
    e = pl.program_id(0)
    f = pl.program_id(1)
    n_exp = pl.num_programs(0)
    rows = acc.shape[0]
    slot = lax.rem(e, 2)
    nxt = 1 - slot
    e_nxt = jnp.minimum(e + 1, n_exp - 1)

    tile = TOKEN_TILE_ROWS

    def row_copy(src_tile_row, buf, dst_row):
        return pltpu.make_async_copy(
            xn_hbm.at[pl.ds(pl.multiple_of(src_tile_row, tile), tile), :],
            xbuf.at[pl.ds(pl.multiple_of((buf * rows + dst_row) * tile, tile), tile), :], sem.at[buf])

    def wait_rows(buf):
        pltpu.make_async_copy(xn_hbm.at[pl.ds(0, rows * tile), :],
                              xbuf.at[pl.ds(pl.multiple_of(buf * rows * tile, tile), rows * tile), :],
                              sem.at[buf]).wait()

    @pl.when((e == 0) & (f == 0))
    def _():
        def first(j, carry):
            row_copy(idx_ref[j], 0, j).start()
            return carry
        lax.fori_loop(0, rows, first, 0, unroll=8)

    @pl.when(f == 0)
    def _():
        wait_rows(slot)
        acc[...] = jnp.zeros(acc.shape, F32)

    wgb[...] = wg_ref[0, 0].astype(BF16)
    wub[...] = wu_ref[0, 0].astype(BF16)
    wdb[...] = wd_ref[0, 0].astype(BF16)

    per_step = rows // n_f
    per_chunk = per_step // (rows // EXPERT_ROWS)
    base = pl.multiple_of(f * per_step, per_step)
    idx_base = e_nxt * rows + base
    for ci, r0 in enumerate(range(0, rows, EXPERT_ROWS)):
        for k in range(ci * per_chunk, (ci + 1) * per_chunk):
            row_copy(idx_ref[idx_base + k], nxt, base + k).start()
        rsl = slice(r0, r0 + EXPERT_ROWS)
        xm = jnp.concatenate(
            [xbuf[pl.ds((slot * rows + r0) * tile + j, EXPERT_ROWS, stride=tile), :]
             for j in range(tile)], axis=1).astype(BF16)
        gate = jnp.dot(xm, wgb[...], preferred_element_type=F32)
        up = jnp.dot(xm, wub[...], preferred_element_type=F32)
        hid = (gate * (1.0 / (1.0 + jnp.exp(-gate))) * up).astype(BF16)
        acc[rsl, :] += jnp.dot(hid, wdb[...], preferred_element_type=F32)

    @pl.when(f == n_f - 1)
    def _():
        lane = lax.broadcasted_iota(jnp.int32, (1, LANES), 1) - (PICK_GATE_LANE0 + e)
        mine = (lane == 0) | (lane == n_exp)
        for r0 in range(0, rows, EXPERT_ROWS):
            rsl = slice(r0, r0 + EXPERT_ROWS)
            gate_col = jnp.sum(jnp.where(mine, pick_ref[rsl, :], 0.0), axis=1, keepdims=True)
            y_ref[0, rsl, :] = (acc[rsl, :] * gate_col).astype(BF16)

    @pl.when((e == n_exp - 1) & (f == n_f - 1))
    def _():
        wait_rows(nxt)


def _experts(idx2, xn_tiles, w_gate, w_up, w_down, layer, picked, cap):
    n_exp, rows = idx2.shape
    d = w_gate.shape[2]
    d_ff = w_gate.shape[3]
    n_f = d_ff // FF_CHUNK
    assert rows % (n_f * (rows // EXPERT_ROWS)) == 0
    return pl.pallas_call(
        functools.partial(_expert_kernel, cap=cap, n_f=n_f),
        grid_spec=pltpu.PrefetchScalarGridSpec(
            num_scalar_prefetch=1,
            grid=(n_exp, n_f),
            in_specs=[pl.BlockSpec(memory_space=pl.ANY),
                      pl.BlockSpec((1, 1, d, FF_CHUNK), lambda e, f, ix: (layer, e, 0, f)),
                      pl.BlockSpec((1, 1, d, FF_CHUNK), lambda e, f, ix: (layer, e, 0, f)),
                      pl.BlockSpec((1, 1, FF_CHUNK, d), lambda e, f, ix: (layer, e, f, 0)),
                      pl.BlockSpec((rows, LANES), lambda e, f, ix: (e, 0))],
            out_specs=pl.BlockSpec((1, rows, d), lambda e, f, ix: (e, 0, 0)),
            scratch_shapes=[pltpu.VMEM((2 * rows * TOKEN_TILE_ROWS, LANES), F32),
                            pltpu.SemaphoreType.DMA((2,)),
                            pltpu.VMEM((rows, d), F32),
                            pltpu.VMEM((d, FF_CHUNK), BF16),
                            pltpu.VMEM((d, FF_CHUNK), BF16),
                            pltpu.VMEM((FF_CHUNK, d), BF16)]),
        out_shape=jax.ShapeDtypeStruct((n_exp, rows, d), BF16),
        compiler_params=_params("arbitrary", "arbitrary"),
        name="expert_ffn",
    )(idx2.reshape(-1), xn_tiles, w_gate, w_up, w_down, picked)


COMBINE_GROUP = 4
COMBINE_WINDOW = 128
BF16_ROWS = 16


def _combine_kernel(win_ref, fits_ref, x_ref, slott_ref, y_ref, g_ref, o_ref, *, cap, final):
    bi = pl.program_id(0)
    ti = pl.program_id(1)
    tile_id = bi * pl.num_programs(1) + ti
    slott = slott_ref[0]
    t, n_exp = slott.shape
    d = x_ref.shape[2]

    def finish(acc):
        if final:
            acc = _rms(acc, g_ref[...])
        o_ref[0] = acc

    @pl.when(fits_ref[tile_id] != 0)
    def _():
        lane = lax.broadcasted_iota(jnp.int32, (t, COMBINE_WINDOW), 1)
        acc = x_ref[0]
        for e0 in range(0, n_exp, COMBINE_GROUP):
            parts, ys = [], []
            for e in range(e0, e0 + COMBINE_GROUP):
                start = pl.multiple_of(win_ref[tile_id * n_exp + e], BF16_ROWS)
                parts.append(jnp.where(slott[:, e:e + 1] - start == lane, 1.0, 0.0).astype(BF16))
                ys.append(y_ref[e, 0, pl.ds(start, COMBINE_WINDOW), :])
            acc = acc + jnp.dot(jnp.concatenate(parts, axis=1), jnp.concatenate(ys, axis=0),
                                preferred_element_type=F32)
        finish(acc)

    @pl.when(fits_ref[tile_id] == 0)
    def _():
        sidx = lax.broadcasted_iota(jnp.int32, (t, cap), 1)
        acc = x_ref[0]
        for e0 in range(0, n_exp, COMBINE_GROUP):
            parts = [jnp.where(slott[:, e:e + 1] == sidx, 1.0, 0.0).astype(BF16)
                     for e in range(e0, e0 + COMBINE_GROUP)]
            yg = y_ref[e0:e0 + COMBINE_GROUP, 0].reshape(COMBINE_GROUP * cap, d)
            acc = acc + jnp.dot(jnp.concatenate(parts, axis=1), yg, preferred_element_type=F32)
        finish(acc)


def _combine_windows(slot_rows, n_exp, b, s, cap):
    nt = s // ROW_TILE
    cnt = jnp.sum((slot_rows >= 0).reshape(n_exp, b, nt, ROW_TILE), axis=-1, dtype=jnp.int32)
    end = jnp.cumsum(cnt, axis=-1)
    start = jnp.minimum((end - cnt) // BF16_ROWS * BF16_ROWS, cap - COMBINE_WINDOW)
    fits = jnp.all(end - start <= COMBINE_WINDOW, axis=0)
    return jnp.transpose(start, (1, 2, 0)).reshape(-1), fits.reshape(-1).astype(jnp.int32)


def _combine(x3, slott, y4, g, win, fits, cap, final):
    b, s, d = x3.shape
    n_exp = y4.shape[0]
    assert cap >= COMBINE_WINDOW and cap % BF16_ROWS == 0
    return pl.pallas_call(
        functools.partial(_combine_kernel, cap=cap, final=final),
        grid_spec=pltpu.PrefetchScalarGridSpec(
            num_scalar_prefetch=2,
            grid=(b, s // ROW_TILE),
            in_specs=[pl.BlockSpec((1, ROW_TILE, d), lambda i, t, w, f: (i, t, 0)),
                      pl.BlockSpec((1, ROW_TILE, n_exp), lambda i, t, w, f: (i, t, 0)),
                      pl.BlockSpec((n_exp, 1, cap, d), lambda i, t, w, f: (0, i, 0, 0)),
                      pl.BlockSpec((1, d), lambda i, t, w, f: (0, 0))],
            out_specs=pl.BlockSpec((1, ROW_TILE, d), lambda i, t, w, f: (i, t, 0))),
        out_shape=jax.ShapeDtypeStruct((b, s, d), F32),
        compiler_params=_params("arbitrary", "arbitrary"),
        name="combine",
    )(win, fits, x3, slott, y4, g)


def kernel(x, mem, mem_norm, norm_mix, w_in, conv_w, w_mem_kv, w_out, norm_ffn,
           w_router, w_gate, w_up, w_down, norm_final):
    b, s, d = x.shape
    depth = w_in.shape[0]
    n_exp = w_router.shape[2]
    conv_ch = conv_w.shape[2]
    mem_width = w_mem_kv.shape[2] // 2
    dil_width = (w_in.shape[2] - 3 * conv_ch - mem_width) // 3
    n_dil_heads = dil_width // HEAD_DIM
    cap = EC_CAPACITY * s // n_exp
    assert n_exp == N_EXPERTS and mem_width == N_MEM_HEADS * HEAD_DIM
    assert all(w // (2 * dl) == HALF_WINDOW for w, dl in DILATED_PATTERNS)

    slopes = jnp.asarray([2.0 ** (-8.0 * (h + 1) / n_dil_heads) for h in range(n_dil_heads)], F32)
    kv_all = jnp.transpose(w_mem_kv, (1, 0, 2)).reshape(d, depth * 2 * mem_width).astype(BF16)
    kvm3 = _memkv(mem.reshape(-1, d), mem_norm.reshape(1, d), kv_all).reshape(b, mem.shape[1], -1)

    x2 = x.reshape(b * s, d)
    out = None
    for l in range(depth):
        conv_o, hq = _inproj(x2, norm_mix[l].reshape(1, d), w_in, l, conv_w[l], conv_ch, s)
        hq3 = hq.reshape(b, s, -1)
        dil_o = _dilated(hq3, slopes, dil_width)
        mem_o = _memattn(hq3, kvm3, l, 3 * dil_width // LANES, mem_width)
        wr = jnp.pad(w_router[l], ((0, 0), (0, LANES - n_exp)))
        wr_hi = wr.astype(BF16)
        wr_pieces = jnp.concatenate([wr_hi, (wr - wr_hi.astype(F32)).astype(BF16)], axis=1)
        x1, xn, aff_tok = _outproj(x2, conv_o, dil_o.reshape(b * s, -1),
                                   mem_o.reshape(b * s, -1), w_out, l,
                                   norm_ffn[l].reshape(1, d), wr_pieces, n_exp)
        aff_rows = jnp.transpose(aff_tok[:, :n_exp]).reshape(n_exp * b, s)
        slot_rows, picked, idx_rows = _topk(aff_rows, aff_tok, cap, b, n_exp)
        y = _experts(idx_rows.reshape(n_exp, b * cap), xn, w_gate, w_up, w_down, l, picked, cap)
        slott = jnp.transpose(slot_rows.reshape(n_exp, b, s), (1, 2, 0))
        win, fits = _combine_windows(slot_rows, n_exp, b, s, cap)
        final = l == depth - 1
        x3 = _combine(x1.reshape(b, s, d), slott, y.reshape(n_exp, b, cap, d),
                      norm_final.reshape(1, d), win, fits, cap, final)
        x2 = x3.reshape(b * s, d)
        out = x3
    return out
```

```python
import functools

import jax
import jax.numpy as jnp
from jax import lax
from jax.experimental import pallas as pl
from jax.experimental.pallas import tpu as pltpu

F32 = jnp.float32
BF16 = jnp.bfloat16

HEAD_DIM = 64
N_MEM_HEADS = 4
DILATED_PATTERNS = ((128, 1), (512, 4), (2048, 16))
HALF_WINDOW = 64
N_EXPERTS = 16
EC_CAPACITY = 2
RMS_EPS = 1e-6
NEG_INF = -1e30

LANES = 128
ROW_TILE = 512
PROJ_ROW_TILE = 1024
ATT_TILE = 128
V7X_VMEM_LIMIT_BYTES = 56 * 1024 * 1024


def _params(*semantics):
    return pltpu.CompilerParams(dimension_semantics=semantics,
                                vmem_limit_bytes=V7X_VMEM_LIMIT_BYTES)


def _rms(x, g):
    ms = jnp.mean(x * x, axis=-1, keepdims=True)
    return x * lax.rsqrt(ms + RMS_EPS) * g


def _nt_dot(a, b, **kw):
    return lax.dot_general(a, b, (((1,), (1,)), ((), ())), preferred_element_type=F32, **kw)


HALO_ROWS = 8


def _inproj_kernel(x_ref, xprev_ref, xnext_ref, g_ref, wf_ref, cw_ref, conv_ref, hq_ref, hc, w_ref,
                   *, conv_ch, tiles_per_seq):
    @pl.when(pl.program_id(0) == 0)
    def _():
        w_ref[...] = wf_ref[0].astype(BF16)

    t = x_ref.shape[0]
    n_conv = 3 * conv_ch
    g = g_ref[...]
    halo = jnp.concatenate([xprev_ref[...], xnext_ref[...]], axis=0)
    xa = jnp.concatenate([_rms(x_ref[...], g), _rms(halo, g)], axis=0).astype(BF16)
    chunk = 2 * LANES
    for c in range(0, w_ref.shape[1], chunk):
        if c >= n_conv:
            hq_ref[:, c - n_conv:c - n_conv + chunk] = jnp.dot(
                xa[0:t], w_ref[:, c:c + chunk], preferred_element_type=F32).astype(BF16)
            continue
        h = jnp.dot(xa, w_ref[:, c:c + chunk], preferred_element_type=F32)
        if c + chunk <= n_conv:
            hc[:, c:c + chunk] = h
        else:
            k = n_conv - c
            hc[:, c:n_conv] = h[:, :k]
            hq_ref[:, 0:chunk - k] = h[0:t, k:].astype(BF16)

    u_all = hc[:, 2 * conv_ch:3 * conv_ch] * hc[:, 0:conv_ch]
    u = u_all[0:t]
    pos = lax.rem(pl.program_id(0), tiles_per_seq)
    u_before = jnp.where(pos == 0, 0.0, u_all[t + HALO_ROWS - 1:t + HALO_ROWS])
    u_after = jnp.where(pos == tiles_per_seq - 1, 0.0, u_all[t + HALO_ROWS:t + HALO_ROWS + 1])
    row = lax.broadcasted_iota(jnp.int32, u.shape, 0)
    prev = jnp.where(row == 0, u_before, pltpu.roll(u, 1, 0))
    nxt = jnp.where(row == t - 1, u_after, pltpu.roll(u, t - 1, 0))
    w = cw_ref[...]
    conv = w[0:1] * prev + w[1:2] * u + w[2:3] * nxt
    conv_ref[...] = (hc[0:t, conv_ch:2 * conv_ch] * conv).astype(BF16)


def _inproj(x2, g, w_all, layer, conv_w, conv_ch, seq):
    n, d = x2.shape
    n_all = w_all.shape[2]
    n_conv = 3 * conv_ch
    per_tile = PROJ_ROW_TILE // HALO_ROWS
    last = n // HALO_ROWS - 1
    assert seq % PROJ_ROW_TILE == 0
    return pl.pallas_call(
        functools.partial(_inproj_kernel, conv_ch=conv_ch, tiles_per_seq=seq // PROJ_ROW_TILE),
        grid=(n // PROJ_ROW_TILE,),
        in_specs=[pl.BlockSpec((PROJ_ROW_TILE, d), lambda i: (i, 0)),
                  pl.BlockSpec((HALO_ROWS, d), lambda i: (jnp.maximum(i * per_tile - 1, 0), 0)),
                  pl.BlockSpec((HALO_ROWS, d), lambda i: (jnp.minimum((i + 1) * per_tile, last), 0)),
                  pl.BlockSpec((1, d), lambda i: (0, 0)),
                  pl.BlockSpec((1, d, n_all), lambda i: (layer, 0, 0)),
                  pl.BlockSpec(conv_w.shape, lambda i: (0, 0))],
        out_specs=[pl.BlockSpec((PROJ_ROW_TILE, conv_ch), lambda i: (i, 0)),
                   pl.BlockSpec((PROJ_ROW_TILE, n_all - n_conv), lambda i: (i, 0))],
        out_shape=[jax.ShapeDtypeStruct((n, conv_ch), BF16),
                   jax.ShapeDtypeStruct((n, n_all - n_conv), BF16)],
        scratch_shapes=[pltpu.VMEM((PROJ_ROW_TILE + 2 * HALO_ROWS, n_conv), F32),
                        pltpu.VMEM((d, n_all), BF16)],
        compiler_params=_params("arbitrary"),
        name="inproj_conv",
    )(x2, x2, x2, g, w_all, conv_w)


DIL_TILE_UNROLL = 16


def _dil_cases(seq, dil):
    tps = (seq // dil) // ATT_TILE
    return tps, (("single",) if tps == 1 else ("first", "interior", "last"))


def _dil_kernel(slopes_ref, q_ref, k_ref, v_ref, o_ref,
                natq, natk, natv, perq, perk, perv, bias,
                num1, mx1, sum1, num2, mx2, sum2, num3, mx3, sum3, onat, *operands):
    seq = q_ref.shape[1]
    pad = HALF_WINDOW
    tq = ATT_TILE
    step = DILATED_PATTERNS[1][1]
    seg2 = seq // step
    seg3 = seg2 // step
    log_seg2 = seg2.bit_length() - 1
    log_step = step.bit_length() - 1
    pair = pl.program_id(0)
    lane = lax.broadcasted_iota(jnp.int32, (1, LANES), 1)
    head0 = lane < HEAD_DIM
    m0 = head0.astype(F32)
    m1 = 1.0 - m0

    @pl.when(pl.program_id(1) == 0)
    def _():
        for pi in range(len(DILATED_PATTERNS)):
            _, _, kp, vp = operands[4 * pi:4 * pi + 4]
            kp[0:pad, :] = jnp.zeros((pad, LANES), BF16)
            kp[pad + seq:pad + seq + pad, :] = jnp.zeros((pad, LANES), BF16)
            vp[0:pad, :] = jnp.zeros((pad, 2 * LANES), BF16)
            vp[pad + seq:pad + seq + pad, :] = jnp.zeros((pad, 2 * LANES), BF16)
            vp[pad:pad + seq, LANES:2 * LANES] = jnp.ones((seq, LANES), BF16)
        row = lax.broadcasted_iota(jnp.int32, (2 * tq, 2 * tq), 0)
        col = lax.broadcasted_iota(jnp.int32, (2 * tq, 2 * tq), 1)
        kc = col - pad
        arel = jnp.abs(kc - (row & (tq - 1)))
        band = arel <= HALF_WINDOW
        slope_rows = jnp.where(row < tq, slopes_ref[2 * pair], slopes_ref[2 * pair + 1])
        idx = 0
        for _, dil in DILATED_PATTERNS:
            dist_bias = -slope_rows * (arel * dil).astype(F32)
            for case in _dil_cases(seq, dil)[1]:
                valid = band
                if case in ("first", "single"):
                    valid = valid & (kc >= 0)
                if case in ("last", "single"):
                    valid = valid & (kc < tq)
                bias[idx] = jnp.where(valid, dist_bias, NEG_INF)
                idx += 1

    natq[...] = q_ref[0].astype(F32) * (HEAD_DIM ** -0.5)
    natk[...] = k_ref[0].astype(F32)
    natv[...] = v_ref[0].astype(F32)

    def set_operands(pi, rows, qv, kv, vv):
        qp0, qp1, kp, vp = operands[4 * pi:4 * pi + 4]
        lo, hi = rows
        qp0[lo:hi, :] = (qv * m0).astype(BF16)
        qp1[lo:hi, :] = (qv * m1).astype(BF16)
        kp[pad + lo:pad + hi, :] = kv.astype(BF16)
        vp[pad + lo:pad + hi, 0:LANES] = vv.astype(BF16)

    def run_tiles(pi, case0, tps, dests):
        num_ref, mx_ref, sum_ref = dests
        qp0, qp1, kp, vp = operands[4 * pi:4 * pi + 4]

        def tile(m, carry):
            r0 = pl.multiple_of(m * tq, tq)
            qc = jnp.concatenate([qp0[pl.ds(r0, tq), :], qp1[pl.ds(r0, tq), :]], axis=0)
            kt = kp[pl.ds(r0, 2 * tq), :]
            vt = vp[pl.ds(r0, 2 * tq), :]
            if tps == 1:
                case = case0
                dst = pl.ds((m >> log_step) * seg2 + (m & (step - 1)), tq, stride=step)
            else:
                pos = m & (tps - 1)
                case = case0 + jnp.where(pos == 0, 0, jnp.where(pos == tps - 1, 2, 1))
                dst = pl.ds(r0, tq)
            s = _nt_dot(qc, kt) + bias[case]
            mx = jnp.max(s, axis=-1, keepdims=True)
            pe = jnp.exp(s - mx)
            pv = jnp.dot(pe.astype(BF16), vt, preferred_element_type=F32)
            num_ref[dst, :] = jnp.where(head0, pv[0:tq, 0:LANES], pv[tq:2 * tq, 0:LANES])
            sum_ref[dst, :] = jnp.where(head0, pv[0:tq, LANES:2 * LANES],
                                        pv[tq:2 * tq, LANES:2 * LANES])
            mx_ref[dst, :] = jnp.where(head0, mx[0:tq], mx[tq:2 * tq])
            return carry

        lax.fori_loop(0, seq // tq, tile, 0, unroll=DIL_TILE_UNROLL)

    case0 = 0
    set_operands(0, (0, seq), natq[...], natk[...], natv[...])
    tps1, cases1 = _dil_cases(seq, DILATED_PATTERNS[0][1])
    run_tiles(0, case0, tps1, (num1, mx1, sum1))
    case0 += len(cases1)

    for b in range(step):
        rows = (b * seg2, (b + 1) * seg2)
        qv = natq[pl.ds(b, seg2, stride=step), :]
        kv = natk[pl.ds(b, seg2, stride=step), :]
        vv = natv[pl.ds(b, seg2, stride=step), :]
        perq[rows[0]:rows[1], :] = qv
        perk[rows[0]:rows[1], :] = kv
        perv[rows[0]:rows[1], :] = vv
        set_operands(1, rows, qv, kv, vv)
    tps2, cases2 = _dil_cases(seq, DILATED_PATTERNS[1][1])
    run_tiles(1, case0, tps2, (num2, mx2, sum2))
    case0 += len(cases2)

    for sgm in range(step * step):
        b, a = divmod(sgm, step)
        src = pl.ds(b * seg2 + a, seg3, stride=step)
        set_operands(2, (sgm * seg3, (sgm + 1) * seg3), perq[src, :], perk[src, :], perv[src, :])
    tps3, _ = _dil_cases(seq, DILATED_PATTERNS[2][1])
    assert tps3 == 1 and seg3 == tq
    run_tiles(2, case0, tps3, (num3, mx3, sum3))

    def merge(c, carry):
        p0 = pl.multiple_of(c * tq, tq)
        per_rows = pl.ds(p0, tq)
        nat_rows = pl.ds((p0 & (seg2 - 1)) * step + (p0 >> log_seg2), tq, stride=step)
        a1, a2, a3 = mx1[nat_rows, :], mx2[per_rows, :], mx3[per_rows, :]
        top = jnp.maximum(jnp.maximum(a1, a2), a3)
        e1, e2, e3 = jnp.exp(a1 - top), jnp.exp(a2 - top), jnp.exp(a3 - top)
        den = e1 * sum1[nat_rows, :] + e2 * sum2[per_rows, :] + e3 * sum3[per_rows, :]
        num = e1 * num1[nat_rows, :] + e2 * num2[per_rows, :] + e3 * num3[per_rows, :]
        onat[nat_rows, :] = num / den
        return carry

    lax.fori_loop(0, seq // tq, merge, 0, unroll=2)
    o_ref[0] = onat[...].astype(BF16)


def _dilated(hq3, slopes, dil_width):
    b, s, _ = hq3.shape
    npair = dil_width // LANES
    blk = (1, s, LANES)
    dils = [d for _, d in DILATED_PATTERNS]
    assert dils[0] == 1 and dils[2] == dils[1] * dils[1]
    n_cases = sum(len(_dil_cases(s, d)[1]) for d in dils)
    pad = HALF_WINDOW
    scratch = [pltpu.VMEM((s, LANES), F32)] * 6
    scratch += [pltpu.VMEM((n_cases, 2 * ATT_TILE, 2 * ATT_TILE), F32)]
    scratch += [pltpu.VMEM((s, LANES), F32)] * 10
    scratch += [pltpu.VMEM((s, LANES), BF16), pltpu.VMEM((s, LANES), BF16),
                pltpu.VMEM((s + 2 * pad, LANES), BF16),
                pltpu.VMEM((s + 2 * pad, 2 * LANES), BF16)] * len(dils)
    return pl.pallas_call(
        _dil_kernel,
        grid_spec=pltpu.PrefetchScalarGridSpec(
            num_scalar_prefetch=1,
            grid=(npair, b),
            in_specs=[pl.BlockSpec(blk, lambda j, i, sl: (i, 0, j)),
                      pl.BlockSpec(blk, lambda j, i, sl: (i, 0, npair + j)),
                      pl.BlockSpec(blk, lambda j, i, sl: (i, 0, 2 * npair + j))],
            out_specs=pl.BlockSpec(blk, lambda j, i, sl: (i, 0, j)),
            scratch_shapes=scratch),
        out_shape=jax.ShapeDtypeStruct((b, s, dil_width), BF16),
        compiler_params=_params("arbitrary", "arbitrary"),
        name="dilated_attn",
    )(slopes, hq3, hq3, hq3)


def _memkv_kernel(m_ref, g_ref, w_ref, o_ref):
    mn = _rms(m_ref[...], g_ref[...]).astype(BF16)
    o_ref[...] = jnp.dot(mn, w_ref[...], preferred_element_type=F32).astype(BF16)


def _memkv(mem2, g, w):
    n, d = mem2.shape
    nw = w.shape[1]
    return pl.pallas_call(
        _memkv_kernel,
        grid=(n // ROW_TILE,),
        in_specs=[pl.BlockSpec((ROW_TILE, d), lambda i: (i, 0)),
                  pl.BlockSpec((1, d), lambda i: (0, 0)),
                  pl.BlockSpec((d, nw), lambda i: (0, 0))],
        out_specs=pl.BlockSpec((ROW_TILE, nw), lambda i: (i, 0)),
        out_shape=jax.ShapeDtypeStruct((n, nw), BF16),
        compiler_params=_params("arbitrary"),
        name="mem_kv",
    )(mem2, g, w)


def _memattn_kernel(q_ref, km_ref, vm_ref, o_ref, qp0, qp1, vext):
    seq = q_ref.shape[1]
    tq = 2 * ATT_TILE
    lane = lax.broadcasted_iota(jnp.int32, (1, LANES), 1)
    head0 = lane < HEAD_DIM
    m0 = head0.astype(F32)
    q = q_ref[0].astype(F32) * (HEAD_DIM ** -0.5)
    qp0[...] = (q * m0).astype(BF16)
    qp1[...] = (q * (1.0 - m0)).astype(BF16)
    vext[:, 0:LANES] = vm_ref[0]
    vext[:, LANES:2 * LANES] = jnp.ones((vext.shape[0], LANES), BF16)
    km = km_ref[0]

    def tile(m, carry):
        rows = pl.ds(pl.multiple_of(m * tq, tq), tq)
        qc = jnp.concatenate([qp0[rows, :], qp1[rows, :]], axis=0)
        s = _nt_dot(qc, km)
        pe = jnp.exp(s - jnp.max(s, axis=-1, keepdims=True))
        pv = jnp.dot(pe.astype(BF16), vext[...], preferred_element_type=F32)
        o = pv[:, 0:LANES] / pv[:, LANES:2 * LANES]
        o_ref[0, rows, :] = jnp.where(head0, o[0:tq], o[tq:2 * tq]).astype(BF16)
        return carry

    lax.fori_loop(0, seq // tq, tile, 0, unroll=True)


def _memattn(hq3, kvm3, layer, qm_block0, mem_width):
    b, s, _ = hq3.shape
    m = kvm3.shape[1]
    npair = mem_width // LANES
    kv_blocks = 2 * npair
    return pl.pallas_call(
        _memattn_kernel,
        grid=(b, npair),
        in_specs=[pl.BlockSpec((1, s, LANES), lambda i, j: (i, 0, qm_block0 + j)),
                  pl.BlockSpec((1, m, LANES), lambda i, j: (i, 0, layer * kv_blocks + j)),
                  pl.BlockSpec((1, m, LANES), lambda i, j: (i, 0, layer * kv_blocks + npair + j))],
        out_specs=pl.BlockSpec((1, s, LANES), lambda i, j: (i, 0, j)),
        out_shape=jax.ShapeDtypeStruct((b, s, mem_width), BF16),
        scratch_shapes=[pltpu.VMEM((s, LANES), BF16), pltpu.VMEM((s, LANES), BF16),
                        pltpu.VMEM((m, 2 * LANES), BF16)],
        compiler_params=_params("arbitrary", "arbitrary"),
        name="mem_attn",
    )(hq3, kvm3, kvm3)


OUT_SUB_ROWS = 512
TOKEN_TILE_ROWS = 8


def _outproj_kernel(x_ref, c_ref, d_ref, m_ref, wf_ref, g_ref, wr_ref,
                    x1_ref, xn_ref, aff_ref, w_ref, *, n_exp):
    @pl.when(pl.program_id(0) == 0)
    def _():
        w_ref[...] = wf_ref[0].astype(BF16)

    valid = lax.broadcasted_iota(jnp.int32, (1, LANES), 1) < n_exp
    for r0 in range(0, x_ref.shape[0], OUT_SUB_ROWS):
        rs = slice(r0, r0 + OUT_SUB_ROWS)
        cat = jnp.concatenate([c_ref[rs, :], d_ref[rs, :], m_ref[rs, :]], axis=1)
        x1 = x_ref[rs, :] + jnp.dot(cat, w_ref[...], preferred_element_type=F32)
        x1_ref[rs, :] = x1
        xn = _rms(x1, g_ref[...])
        for j in range(xn.shape[1] // LANES):
            xn_ref[pl.ds(r0 * TOKEN_TILE_ROWS + j, OUT_SUB_ROWS, stride=TOKEN_TILE_ROWS), :] = (
                xn[:, j * LANES:(j + 1) * LANES])
        xh = xn.astype(BF16)
        xl = (xn - xh.astype(F32)).astype(BF16)
        both = jnp.dot(xh, wr_ref[...], preferred_element_type=F32)
        logits = (both[:, 0:LANES] + both[:, LANES:2 * LANES]
                  + jnp.dot(xl, wr_ref[:, 0:LANES], preferred_element_type=F32))
        logits = jnp.where(valid, logits, NEG_INF)
        ex = jnp.exp(logits - jnp.max(logits, axis=1, keepdims=True))
        aff_ref[rs, :] = ex / jnp.sum(ex, axis=1, keepdims=True)


def _outproj(x2, conv_o, dil_o, mem_o, w_all, layer, g, wr_pieces, n_exp):
    n, d = x2.shape
    assert d == TOKEN_TILE_ROWS * LANES
    row = lambda i: (i, 0)
    fixed = lambda i: (0, 0)
    return pl.pallas_call(
        functools.partial(_outproj_kernel, n_exp=n_exp),
        grid=(n // PROJ_ROW_TILE,),
        in_specs=[pl.BlockSpec((PROJ_ROW_TILE, d), row),
                  pl.BlockSpec((PROJ_ROW_TILE, conv_o.shape[1]), row),
                  pl.BlockSpec((PROJ_ROW_TILE, dil_o.shape[1]), row),
                  pl.BlockSpec((PROJ_ROW_TILE, mem_o.shape[1]), row),
                  pl.BlockSpec((1,) + w_all.shape[1:], lambda i: (layer, 0, 0)),
                  pl.BlockSpec((1, d), fixed),
                  pl.BlockSpec((d, 2 * LANES), fixed)],
        out_specs=[pl.BlockSpec((PROJ_ROW_TILE, d), row),
                   pl.BlockSpec((PROJ_ROW_TILE * TOKEN_TILE_ROWS, LANES), row),
                   pl.BlockSpec((PROJ_ROW_TILE, LANES), row)],
        out_shape=[jax.ShapeDtypeStruct((n, d), F32),
                   jax.ShapeDtypeStruct((n * TOKEN_TILE_ROWS, LANES), F32),
                   jax.ShapeDtypeStruct((n, LANES), F32)],
        scratch_shapes=[pltpu.VMEM(w_all.shape[1:], BF16)],
        compiler_params=_params("arbitrary"),
        name="outproj_router",
    )(x2, conv_o, dil_o, mem_o, w_all, g, wr_pieces)


TOK_DIGIT_BITS = 6
PICK_GATE_LANE0 = 2


def _fill_strict_upper(tri):
    s = tri.shape[0]
    chunk = 2 * LANES
    for c in range(0, s, chunk):
        src = lax.broadcasted_iota(jnp.int32, (chunk, s), 0) + c
        dst = lax.broadcasted_iota(jnp.int32, (chunk, s), 1)
        tri[c:c + chunk, :] = jnp.where(src < dst, 1.0, 0.0).astype(BF16)


def _topk_kernel(aff_ref, afft_ref, slot_ref, pick_ref, idx_ref, rhs, tri, *, cap, n_batch, n_exp):
    aff = aff_ref[...]
    rows, seq = aff.shape
    _fill_strict_upper(tri)
    bits = jnp.zeros((rows, 1), jnp.int32)
    for bit in range(30, -1, -1):
        cand = bits | (1 << bit)
        cnt = jnp.sum(jnp.where(aff >= pltpu.bitcast(cand, F32), 1.0, 0.0), axis=1, keepdims=True)
        bits = jnp.where(cnt >= cap, cand, bits)
    thr = pltpu.bitcast(bits, F32)
    gt = aff > thr
    eq = aff == thr
    need = cap - jnp.sum(jnp.where(gt, 1.0, 0.0), axis=1, keepdims=True)
    rank_eq = jnp.dot(jnp.where(eq, 1.0, 0.0).astype(BF16), tri[...], preferred_element_type=F32)
    sel = jnp.where(gt, 1.0, jnp.where(eq & (rank_eq < need), 1.0, 0.0))
    rank = jnp.dot(sel.astype(BF16), tri[...], preferred_element_type=F32)
    slot_ref[...] = jnp.where(sel > 0.5, rank, -1.0).astype(jnp.int32)

    half = seq // 2
    lane = lax.broadcasted_iota(jnp.int32, (1, LANES), 1)
    tok = lax.broadcasted_iota(jnp.int32, (seq, LANES), 0)
    tok_part = jnp.where(lane == 0, tok >> TOK_DIGIT_BITS,
                         jnp.where(lane == 1, tok & ((1 << TOK_DIGIT_BITS) - 1),
                                   jnp.where(lane == LANES - 1, 1, 0))).astype(F32)
    src = lax.broadcasted_iota(jnp.int32, (LANES, LANES), 0)
    dst = lax.broadcasted_iota(jnp.int32, (LANES, LANES), 1)

    def place(piece):
        return jnp.where((src < n_exp) & (dst == src + PICK_GATE_LANE0 + piece * n_exp),
                         1.0, 0.0).astype(BF16)

    for bi in range(n_batch):
        a = afft_ref[bi * seq:(bi + 1) * seq, :]
        a_hi = a.astype(BF16)
        a_lo = (a - a_hi.astype(F32)).astype(BF16)
        full = (tok_part
                + jnp.dot(a_hi, place(0), preferred_element_type=F32)
                + jnp.dot(a_lo, place(1), preferred_element_type=F32)).astype(BF16)
        rhs[bi, :, 0:LANES] = full[0:half]
        rhs[bi, :, LANES:2 * LANES] = full[half:seq]

    sidx = lax.broadcasted_iota(jnp.int32, (cap, half), 0)

    def invert(r, carry):
        bi = lax.rem(r, n_batch)
        srow = slot_ref[pl.ds(r, 1), :]
        onehot = jnp.where(srow[:, 0:half] == sidx, 1.0,
                           jnp.where(srow[:, half:seq] == sidx, 2.0, 0.0)).astype(BF16)
        both = jnp.dot(onehot, rhs[bi], preferred_element_type=F32)
        lower = both[:, LANES - 1:LANES] < 1.5
        pick = jnp.where(lower, both[:, 0:LANES], both[:, LANES:2 * LANES] * 0.5)
        pick_ref[pl.ds(pl.multiple_of(r * cap, cap), cap), :] = pick
        digits = jnp.transpose(pick)
        token = (digits[0:1] * (1 << TOK_DIGIT_BITS) + digits[1:2]).astype(jnp.int32)
        idx_ref[pl.ds(r, 1), :] = (token + bi * seq) * TOKEN_TILE_ROWS
        return carry

    lax.fori_loop(0, rows, invert, 0, unroll=8)


def _topk(aff_rows, aff_tok, cap, n_batch, n_exp):
    rows, seq = aff_rows.shape
    return pl.pallas_call(
        functools.partial(_topk_kernel, cap=cap, n_batch=n_batch, n_exp=n_exp),
        out_shape=[jax.ShapeDtypeStruct(aff_rows.shape, jnp.int32),
                   jax.ShapeDtypeStruct((rows * cap, LANES), F32),
                   jax.ShapeDtypeStruct((rows, cap), jnp.int32)],
        scratch_shapes=[pltpu.VMEM((n_batch, seq // 2, 2 * LANES), BF16),
                        pltpu.VMEM((seq, seq), BF16)],
        compiler_params=pltpu.CompilerParams(vmem_limit_bytes=V7X_VMEM_LIMIT_BYTES),
        name="expert_topk",
    )(aff_rows, aff_tok)


FF_CHUNK = 512
EXPERT_ROWS = 1024


def _expert_kernel(idx_ref, xn_hbm, wg_ref, wu_ref, wd_ref, pick_ref, y_ref,
                   xbuf, sem, acc, wgb, wub, wdb, *, cap, n_f):
    e = pl.program_id(0)
    f = pl.program_id(1)
    n_exp = pl.num_programs(0)
    rows = acc.shape[0]
    slot = lax.rem(e, 2)
    nxt = 1 - slot
    e_nxt = jnp.minimum(e + 1, n_exp - 1)

    tile = TOKEN_TILE_ROWS

    def row_copy(src_tile_row, buf, dst_row):
        return pltpu.make_async_copy(
            xn_hbm.at[pl.ds(pl.multiple_of(src_tile_row, tile), tile), :],
            xbuf.at[pl.ds(pl.multiple_of((buf * rows + dst_row) * tile, tile), tile), :], sem.at[buf])

    def wait_rows(buf):
        pltpu.make_async_copy(xn_hbm.at[pl.ds(0, rows * tile), :],
                              xbuf.at[pl.ds(pl.multiple_of(buf * rows * tile, tile), rows * tile), :],
                              sem.at[buf]).wait()

    @pl.when((e == 0) & (f == 0))
    def _():
        def first(j, carry):
            row_copy(idx_ref[j], 0, j).start()
            return carry
        lax.fori_loop(0, rows, first, 0, unroll=8)

    @pl.when(f == 0)
    def _():
        wait_rows(slot)
        acc[...] = jnp.zeros(acc.shape, F32)

    wgb[...] = wg_ref[0, 0].astype(BF16)
    wub[...] = wu_ref[0, 0].astype(BF16)
    wdb[...] = wd_ref[0, 0].astype(BF16)

    per_step = rows // n_f
    per_chunk = per_step // (rows // EXPERT_ROWS)
    base = pl.multiple_of(f * per_step, per_step)
    idx_base = e_nxt * rows + base
    for ci, r0 in enumerate(range(0, rows, EXPERT_ROWS)):
        for k in range(ci * per_chunk, (ci + 1) * per_chunk):
            row_copy(idx_ref[idx_base + k], nxt, base + k).start(priority=1)
        rsl = slice(r0, r0 + EXPERT_ROWS)
        xm = jnp.concatenate(
            [xbuf[pl.ds((slot * rows + r0) * tile + j, EXPERT_ROWS, stride=tile), :]
             for j in range(tile)], axis=1).astype(BF16)
        gate = jnp.dot(xm, wgb[...], preferred_element_type=F32)
        up = jnp.dot(xm, wub[...], preferred_element_type=F32)
        hid = (gate * (1.0 / (1.0 + jnp.exp(-gate))) * up).astype(BF16)
        acc[rsl, :] += jnp.dot(hid, wdb[...], preferred_element_type=F32)

    @pl.when(f == n_f - 1)
    def _():
        lane = lax.broadcasted_iota(jnp.int32, (1, LANES), 1) - (PICK_GATE_LANE0 + e)
        mine = (lane == 0) | (lane == n_exp)
        for r0 in range(0, rows, EXPERT_ROWS):
            rsl = slice(r0, r0 + EXPERT_ROWS)
            gate_col = jnp.sum(jnp.where(mine, pick_ref[rsl, :], 0.0), axis=1, keepdims=True)
            y_ref[0, rsl, :] = (acc[rsl, :] * gate_col).astype(BF16)

    @pl.when((e == n_exp - 1) & (f == n_f - 1))
    def _():
        wait_rows(nxt)


def _experts(idx2, xn_tiles, w_gate, w_up, w_down, layer, picked, cap):
    n_exp, rows = idx2.shape
    d = w_gate.shape[2]
    d_ff = w_gate.shape[3]
    n_f = d_ff // FF_CHUNK
    assert rows % (n_f * (rows // EXPERT_ROWS)) == 0
    return pl.pallas_call(
        functools.partial(_expert_kernel, cap=cap, n_f=n_f),
        grid_spec=pltpu.PrefetchScalarGridSpec(
            num_scalar_prefetch=1,
            grid=(n_exp, n_f),
            in_specs=[pl.BlockSpec(memory_space=pl.ANY),
                      pl.BlockSpec((1, 1, d, FF_CHUNK), lambda e, f, ix: (layer, e, 0, f)),
                      pl.BlockSpec((1, 1, d, FF_CHUNK), lambda e, f, ix: (layer, e, 0, f)),
                      pl.BlockSpec((1, 1, FF_CHUNK, d), lambda e, f, ix: (layer, e, f, 0)),
                      pl.BlockSpec((rows, LANES), lambda e, f, ix: (e, 0))],
            out_specs=pl.BlockSpec((1, rows, d), lambda e, f, ix: (e, 0, 0)),
            scratch_shapes=[pltpu.VMEM((2 * rows * TOKEN_TILE_ROWS, LANES), F32),
                            pltpu.SemaphoreType.DMA((2,)),
                            pltpu.VMEM((rows, d), F32),
                            pltpu.VMEM((d, FF_CHUNK), BF16),
                            pltpu.VMEM((d, FF_CHUNK), BF16),
                            pltpu.VMEM((FF_CHUNK, d), BF16)]),
        out_shape=jax.ShapeDtypeStruct((n_exp, rows, d), BF16),
        compiler_params=_params("arbitrary", "arbitrary"),
        name="expert_ffn",
    )(idx2.reshape(-1), xn_tiles, w_gate, w_up, w_down, picked)


COMBINE_GROUP = 4
COMBINE_WINDOW = 128
BF16_ROWS = 16


def _combine_kernel(win_ref, fits_ref, x_ref, slott_ref, y_ref, g_ref, o_ref, *, cap, final):
    bi = pl.program_id(0)
    ti = pl.program_id(1)
    tile_id = bi * pl.num_programs(1) + ti
    slott = slott_ref[0]
    t, n_exp = slott.shape
    d = x_ref.shape[2]

    def finish(acc):
        if final:
            acc = _rms(acc, g_ref[...])
        o_ref[0] = acc

    @pl.when(fits_ref[tile_id] != 0)
    def _():
        lane = lax.broadcasted_iota(jnp.int32, (t, COMBINE_WINDOW), 1)
        acc = x_ref[0]
        for e0 in range(0, n_exp, COMBINE_GROUP):
            parts, ys = [], []
            for e in range(e0, e0 + COMBINE_GROUP):
                start = pl.multiple_of(win_ref[tile_id * n_exp + e], BF16_ROWS)
                parts.append(jnp.where(slott[:, e:e + 1] - start == lane, 1.0, 0.0).astype(BF16))
                ys.append(y_ref[e, 0, pl.ds(start, COMBINE_WINDOW), :])
            acc = acc + jnp.dot(jnp.concatenate(parts, axis=1), jnp.concatenate(ys, axis=0),
                                preferred_element_type=F32)
        finish(acc)

    @pl.when(fits_ref[tile_id] == 0)
    def _():
        sidx = lax.broadcasted_iota(jnp.int32, (t, cap), 1)
        acc = x_ref[0]
        for e0 in range(0, n_exp, COMBINE_GROUP):
            parts = [jnp.where(slott[:, e:e + 1] == sidx, 1.0, 0.0).astype(BF16)
                     for e in range(e0, e0 + COMBINE_GROUP)]
            yg = y_ref[e0:e0 + COMBINE_GROUP, 0].reshape(COMBINE_GROUP * cap, d)
            acc = acc + jnp.dot(jnp.concatenate(parts, axis=1), yg, preferred_element_type=F32)
        finish(acc)


def _combine_windows(slot_rows, n_exp, b, s, cap):
    nt = s // ROW_TILE
    cnt = jnp.sum((slot_rows >= 0).reshape(n_exp, b, nt, ROW_TILE), axis=-1, dtype=jnp.int32)
    end = jnp.cumsum(cnt, axis=-1)
    start = jnp.minimum((end - cnt) // BF16_ROWS * BF16_ROWS, cap - COMBINE_WINDOW)
    fits = jnp.all(end - start <= COMBINE_WINDOW, axis=0)
    return jnp.transpose(start, (1, 2, 0)).reshape(-1), fits.reshape(-1).astype(jnp.int32)


def _combine(x3, slott, y4, g, win, fits, cap, final):
    b, s, d = x3.shape
    n_exp = y4.shape[0]
    assert cap >= COMBINE_WINDOW and cap % BF16_ROWS == 0
    return pl.pallas_call(
        functools.partial(_combine_kernel, cap=cap, final=final),
        grid_spec=pltpu.PrefetchScalarGridSpec(
            num_scalar_prefetch=2,
            grid=(b, s // ROW_TILE),
            in_specs=[pl.BlockSpec((1, ROW_TILE, d), lambda i, t, w, f: (i, t, 0)),
                      pl.BlockSpec((1, ROW_TILE, n_exp), lambda i, t, w, f: (i, t, 0)),
                      pl.BlockSpec((n_exp, 1, cap, d), lambda i, t, w, f: (0, i, 0, 0)),
                      pl.BlockSpec((1, d), lambda i, t, w, f: (0, 0))],
            out_specs=pl.BlockSpec((1, ROW_TILE, d), lambda i, t, w, f: (i, t, 0))),
        out_shape=jax.ShapeDtypeStruct((b, s, d), F32),
        compiler_params=_params("arbitrary", "arbitrary"),
        name="combine",
    )(win, fits, x3, slott, y4, g)


def kernel(x, mem, mem_norm, norm_mix, w_in, conv_w, w_mem_kv, w_out, norm_ffn,
           w_router, w_gate, w_up, w_down, norm_final):
    b, s, d = x.shape
    depth = w_in.shape[0]
    n_exp = w_router.shape[2]
    conv_ch = conv_w.shape[2]
    mem_width = w_mem_kv.shape[2] // 2
    dil_width = (w_in.shape[2] - 3 * conv_ch - mem_width) // 3
    n_dil_heads = dil_width // HEAD_DIM
    cap = EC_CAPACITY * s // n_exp
    assert n_exp == N_EXPERTS and mem_width == N_MEM_HEADS * HEAD_DIM
    assert all(w // (2 * dl) == HALF_WINDOW for w, dl in DILATED_PATTERNS)

    slopes = jnp.asarray([2.0 ** (-8.0 * (h + 1) / n_dil_heads) for h in range(n_dil_heads)], F32)
    kv_all = jnp.transpose(w_mem_kv, (1, 0, 2)).reshape(d, depth * 2 * mem_width).astype(BF16)
    kvm3 = _memkv(mem.reshape(-1, d), mem_norm.reshape(1, d), kv_all).reshape(b, mem.shape[1], -1)

    x2 = x.reshape(b * s, d)
    out = None
    for l in range(depth):
        conv_o, hq = _inproj(x2, norm_mix[l].reshape(1, d), w_in, l, conv_w[l], conv_ch, s)
        hq3 = hq.reshape(b, s, -1)
        dil_o = _dilated(hq3, slopes, dil_width)
        mem_o = _memattn(hq3, kvm3, l, 3 * dil_width // LANES, mem_width)
        wr = jnp.pad(w_router[l], ((0, 0), (0, LANES - n_exp)))
        wr_hi = wr.astype(BF16)
        wr_pieces = jnp.concatenate([wr_hi, (wr - wr_hi.astype(F32)).astype(BF16)], axis=1)
        x1, xn, aff_tok = _outproj(x2, conv_o, dil_o.reshape(b * s, -1),
                                   mem_o.reshape(b * s, -1), w_out, l,
                                   norm_ffn[l].reshape(1, d), wr_pieces, n_exp)
        aff_rows = jnp.transpose(aff_tok[:, :n_exp]).reshape(n_exp * b, s)
        slot_rows, picked, idx_rows = _topk(aff_rows, aff_tok, cap, b, n_exp)
        y = _experts(idx_rows.reshape(n_exp, b * cap), xn, w_gate, w_up, w_down, l, picked, cap)
        slott = jnp.transpose(slot_rows.reshape(n_exp, b, s), (1, 2, 0))
        win, fits = _combine_windows(slot_rows, n_exp, b, s, cap)
        final = l == depth - 1
        x3 = _combine(x1.reshape(b, s, d), slott, y.reshape(n_exp, b, cap, d),
                      norm_final.reshape(1, d), win, fits, cap, final)
        x2 = x3.reshape(b * s, d)
        out = x3
    return out
```

```python
import functools

import jax
import jax.numpy as jnp
from jax import lax
from jax.experimental import pallas as pl
from jax.experimental.pallas import tpu as pltpu

F32 = jnp.float32
BF16 = jnp.bfloat16

HEAD_DIM = 64
N_MEM_HEADS = 4
DILATED_PATTERNS = ((128, 1), (512, 4), (2048, 16))
HALF_WINDOW = 64
N_EXPERTS = 16
EC_CAPACITY = 2
RMS_EPS = 1e-6
NEG_INF = -1e30

LANES = 128
ROW_TILE = 512
PROJ_ROW_TILE = 1024
ATT_TILE = 128
V7X_VMEM_LIMIT_BYTES = 56 * 1024 * 1024


def _params(*semantics):
    return pltpu.CompilerParams(dimension_semantics=semantics,
                                vmem_limit_bytes=V7X_VMEM_LIMIT_BYTES)


def _rms(x, g):
    ms = jnp.mean(x * x, axis=-1, keepdims=True)
    return x * lax.rsqrt(ms + RMS_EPS) * g


def _nt_dot(a, b, **kw):
    return lax.dot_general(a, b, (((1,), (1,)), ((), ())), preferred_element_type=F32, **kw)


HALO_ROWS = 8


def _inproj_kernel(x_ref, xprev_ref, xnext_ref, g_ref, wf_ref, cw_ref, conv_ref, hq_ref, hc, w_ref,
                   *, conv_ch, tiles_per_seq):
    @pl.when(pl.program_id(0) == 0)
    def _():
        w_ref[...] = wf_ref[0].astype(BF16)

    t = x_ref.shape[0]
    n_conv = 3 * conv_ch
    g = g_ref[...]
    halo = jnp.concatenate([xprev_ref[...], xnext_ref[...]], axis=0)
    xa = jnp.concatenate([_rms(x_ref[...], g), _rms(halo, g)], axis=0).astype(BF16)
    chunk = 2 * LANES
    for c in range(0, w_ref.shape[1], chunk):
        if c >= n_conv:
            hq_ref[:, c - n_conv:c - n_conv + chunk] = jnp.dot(
                xa[0:t], w_ref[:, c:c + chunk], preferred_element_type=F32).astype(BF16)
            continue
        h = jnp.dot(xa, w_ref[:, c:c + chunk], preferred_element_type=F32)
        if c + chunk <= n_conv:
            hc[:, c:c + chunk] = h
        else:
            k = n_conv - c
            hc[:, c:n_conv] = h[:, :k]
            hq_ref[:, 0:chunk - k] = h[0:t, k:].astype(BF16)

    u_all = hc[:, 2 * conv_ch:3 * conv_ch] * hc[:, 0:conv_ch]
    u = u_all[0:t]
    pos = lax.rem(pl.program_id(0), tiles_per_seq)
    u_before = jnp.where(pos == 0, 0.0, u_all[t + HALO_ROWS - 1:t + HALO_ROWS])
    u_after = jnp.where(pos == tiles_per_seq - 1, 0.0, u_all[t + HALO_ROWS:t + HALO_ROWS + 1])
    row = lax.broadcasted_iota(jnp.int32, u.shape, 0)
    prev = jnp.where(row == 0, u_before, pltpu.roll(u, 1, 0))
    nxt = jnp.where(row == t - 1, u_after, pltpu.roll(u, t - 1, 0))
    w = cw_ref[...]
    conv = w[0:1] * prev + w[1:2] * u + w[2:3] * nxt
    conv_ref[...] = (hc[0:t, conv_ch:2 * conv_ch] * conv).astype(BF16)


def _inproj(x2, g, w_all, layer, conv_w, conv_ch, seq):
    n, d = x2.shape
    n_all = w_all.shape[2]
    n_conv = 3 * conv_ch
    per_tile = PROJ_ROW_TILE // HALO_ROWS
    last = n // HALO_ROWS - 1
    assert seq % PROJ_ROW_TILE == 0
    return pl.pallas_call(
        functools.partial(_inproj_kernel, conv_ch=conv_ch, tiles_per_seq=seq // PROJ_ROW_TILE),
        grid=(n // PROJ_ROW_TILE,),
        in_specs=[pl.BlockSpec((PROJ_ROW_TILE, d), lambda i: (i, 0)),
                  pl.BlockSpec((HALO_ROWS, d), lambda i: (jnp.maximum(i * per_tile - 1, 0), 0)),
                  pl.BlockSpec((HALO_ROWS, d), lambda i: (jnp.minimum((i + 1) * per_tile, last), 0)),
                  pl.BlockSpec((1, d), lambda i: (0, 0)),
                  pl.BlockSpec((1, d, n_all), lambda i: (layer, 0, 0)),
                  pl.BlockSpec(conv_w.shape, lambda i: (0, 0))],
        out_specs=[pl.BlockSpec((PROJ_ROW_TILE, conv_ch), lambda i: (i, 0)),
                   pl.BlockSpec((PROJ_ROW_TILE, n_all - n_conv), lambda i: (i, 0))],
        out_shape=[jax.ShapeDtypeStruct((n, conv_ch), BF16),
                   jax.ShapeDtypeStruct((n, n_all - n_conv), BF16)],
        scratch_shapes=[pltpu.VMEM((PROJ_ROW_TILE + 2 * HALO_ROWS, n_conv), F32),
                        pltpu.VMEM((d, n_all), BF16)],
        compiler_params=_params("arbitrary"),
        name="inproj_conv",
    )(x2, x2, x2, g, w_all, conv_w)


DIL_TILE_UNROLL = 16


def _dil_cases(seq, dil):
    tps = (seq // dil) // ATT_TILE
    return tps, (("single",) if tps == 1 else ("first", "interior", "last"))


def _dil_kernel(slopes_ref, q_ref, k_ref, v_ref, o_ref,
                natq, natk, natv, perq, perk, perv, bias,
                num1, mx1, sum1, num2, mx2, sum2, num3, mx3, sum3, onat, *operands):
    seq = q_ref.shape[1]
    pad = HALF_WINDOW
    tq = ATT_TILE
    step = DILATED_PATTERNS[1][1]
    seg2 = seq // step
    seg3 = seg2 // step
    log_seg2 = seg2.bit_length() - 1
    log_step = step.bit_length() - 1
    pair = pl.program_id(0)
    lane = lax.broadcasted_iota(jnp.int32, (1, LANES), 1)
    head0 = lane < HEAD_DIM
    m0 = head0.astype(F32)
    m1 = 1.0 - m0

    @pl.when(pl.program_id(1) == 0)
    def _():
        for pi in range(len(DILATED_PATTERNS)):
            _, _, kp, vp = operands[4 * pi:4 * pi + 4]
            kp[0:pad, :] = jnp.zeros((pad, LANES), BF16)
            kp[pad + seq:pad + seq + pad, :] = jnp.zeros((pad, LANES), BF16)
            vp[0:pad, :] = jnp.zeros((pad, 2 * LANES), BF16)
            vp[pad + seq:pad + seq + pad, :] = jnp.zeros((pad, 2 * LANES), BF16)
            vp[pad:pad + seq, LANES:2 * LANES] = jnp.ones((seq, LANES), BF16)
        row = lax.broadcasted_iota(jnp.int32, (2 * tq, 2 * tq), 0)
        col = lax.broadcasted_iota(jnp.int32, (2 * tq, 2 * tq), 1)
        kc = col - pad
        arel = jnp.abs(kc - (row & (tq - 1)))
        band = arel <= HALF_WINDOW
        slope_rows = jnp.where(row < tq, slopes_ref[2 * pair], slopes_ref[2 * pair + 1])
        idx = 0
        for _, dil in DILATED_PATTERNS:
            dist_bias = -slope_rows * (arel * dil).astype(F32)
            for case in _dil_cases(seq, dil)[1]:
                valid = band
                if case in ("first", "single"):
                    valid = valid & (kc >= 0)
                if case in ("last", "single"):
                    valid = valid & (kc < tq)
                bias[idx] = jnp.where(valid, dist_bias, NEG_INF)
                idx += 1

    natq[...] = q_ref[0].astype(F32) * (HEAD_DIM ** -0.5)
    natk[...] = k_ref[0].astype(F32)
    natv[...] = v_ref[0].astype(F32)

    def set_operands(pi, rows, qv, kv, vv):
        qp0, qp1, kp, vp = operands[4 * pi:4 * pi + 4]
        lo, hi = rows
        qp0[lo:hi, :] = (qv * m0).astype(BF16)
        qp1[lo:hi, :] = (qv * m1).astype(BF16)
        kp[pad + lo:pad + hi, :] = kv.astype(BF16)
        vp[pad + lo:pad + hi, 0:LANES] = vv.astype(BF16)

    def run_tiles(pi, case0, tps, dests):
        num_ref, mx_ref, sum_ref = dests
        qp0, qp1, kp, vp = operands[4 * pi:4 * pi + 4]

        def tile(m, carry):
            r0 = pl.multiple_of(m * tq, tq)
            qc = jnp.concatenate([qp0[pl.ds(r0, tq), :], qp1[pl.ds(r0, tq), :]], axis=0)
            kt = kp[pl.ds(r0, 2 * tq), :]
            vt = vp[pl.ds(r0, 2 * tq), :]
            if tps == 1:
                case = case0
                dst = pl.ds((m >> log_step) * seg2 + (m & (step - 1)), tq, stride=step)
            else:
                pos = m & (tps - 1)
                case = case0 + jnp.where(pos == 0, 0, jnp.where(pos == tps - 1, 2, 1))
                dst = pl.ds(r0, tq)
            s = _nt_dot(qc, kt) + bias[case]
            mx = jnp.max(s, axis=-1, keepdims=True)
            pe = jnp.exp(s - mx)
            pv = jnp.dot(pe.astype(BF16), vt, preferred_element_type=F32)
            num_ref[dst, :] = jnp.where(head0, pv[0:tq, 0:LANES], pv[tq:2 * tq, 0:LANES])
            sum_ref[dst, :] = jnp.where(head0, pv[0:tq, LANES:2 * LANES],
                                        pv[tq:2 * tq, LANES:2 * LANES])
            mx_ref[dst, :] = jnp.where(head0, mx[0:tq], mx[tq:2 * tq])
            return carry

        lax.fori_loop(0, seq // tq, tile, 0, unroll=DIL_TILE_UNROLL)

    case0 = 0
    set_operands(0, (0, seq), natq[...], natk[...], natv[...])
    tps1, cases1 = _dil_cases(seq, DILATED_PATTERNS[0][1])
    run_tiles(0, case0, tps1, (num1, mx1, sum1))
    case0 += len(cases1)

    for b in range(step):
        rows = (b * seg2, (b + 1) * seg2)
        qv = natq[pl.ds(b, seg2, stride=step), :]
        kv = natk[pl.ds(b, seg2, stride=step), :]
        vv = natv[pl.ds(b, seg2, stride=step), :]
        perq[rows[0]:rows[1], :] = qv
        perk[rows[0]:rows[1], :] = kv
        perv[rows[0]:rows[1], :] = vv
        set_operands(1, rows, qv, kv, vv)
    tps2, cases2 = _dil_cases(seq, DILATED_PATTERNS[1][1])
    run_tiles(1, case0, tps2, (num2, mx2, sum2))
    case0 += len(cases2)

    for sgm in range(step * step):
        b, a = divmod(sgm, step)
        src = pl.ds(b * seg2 + a, seg3, stride=step)
        set_operands(2, (sgm * seg3, (sgm + 1) * seg3), perq[src, :], perk[src, :], perv[src, :])
    tps3, _ = _dil_cases(seq, DILATED_PATTERNS[2][1])
    assert tps3 == 1 and seg3 == tq
    run_tiles(2, case0, tps3, (num3, mx3, sum3))

    def merge(c, carry):
        p0 = pl.multiple_of(c * tq, tq)
        per_rows = pl.ds(p0, tq)
        nat_rows = pl.ds((p0 & (seg2 - 1)) * step + (p0 >> log_seg2), tq, stride=step)
        a1, a2, a3 = mx1[nat_rows, :], mx2[per_rows, :], mx3[per_rows, :]
        top = jnp.maximum(jnp.maximum(a1, a2), a3)
        e1, e2, e3 = jnp.exp(a1 - top), jnp.exp(a2 - top), jnp.exp(a3 - top)
        den = e1 * sum1[nat_rows, :] + e2 * sum2[per_rows, :] + e3 * sum3[per_rows, :]
        num = e1 * num1[nat_rows, :] + e2 * num2[per_rows, :] + e3 * num3[per_rows, :]
        onat[nat_rows, :] = num / den
        return carry

    lax.fori_loop(0, seq // tq, merge, 0, unroll=2)
    o_ref[0] = onat[...].astype(BF16)


def _dilated(hq3, slopes, dil_width):
    b, s, _ = hq3.shape
    npair = dil_width // LANES
    blk = (1, s, LANES)
    dils = [d for _, d in DILATED_PATTERNS]
    assert dils[0] == 1 and dils[2] == dils[1] * dils[1]
    n_cases = sum(len(_dil_cases(s, d)[1]) for d in dils)
    pad = HALF_WINDOW
    scratch = [pltpu.VMEM((s, LANES), F32)] * 6
    scratch += [pltpu.VMEM((n_cases, 2 * ATT_TILE, 2 * ATT_TILE), F32)]
    scratch += [pltpu.VMEM((s, LANES), F32)] * 10
    scratch += [pltpu.VMEM((s, LANES), BF16), pltpu.VMEM((s, LANES), BF16),
                pltpu.VMEM((s + 2 * pad, LANES), BF16),
                pltpu.VMEM((s + 2 * pad, 2 * LANES), BF16)] * len(dils)
    return pl.pallas_call(
        _dil_kernel,
        grid_spec=pltpu.PrefetchScalarGridSpec(
            num_scalar_prefetch=1,
            grid=(npair, b),
            in_specs=[pl.BlockSpec(blk, lambda j, i, sl: (i, 0, j)),
                      pl.BlockSpec(blk, lambda j, i, sl: (i, 0, npair + j)),
                      pl.BlockSpec(blk, lambda j, i, sl: (i, 0, 2 * npair + j))],
            out_specs=pl.BlockSpec(blk, lambda j, i, sl: (i, 0, j)),
            scratch_shapes=scratch),
        out_shape=jax.ShapeDtypeStruct((b, s, dil_width), BF16),
        compiler_params=_params("arbitrary", "arbitrary"),
        name="dilated_attn",
    )(slopes, hq3, hq3, hq3)


def _memkv_kernel(m_ref, g_ref, w_ref, o_ref):
    mn = _rms(m_ref[...], g_ref[...]).astype(BF16)
    o_ref[...] = jnp.dot(mn, w_ref[...], preferred_element_type=F32).astype(BF16)


def _memkv(mem2, g, w):
    n, d = mem2.shape
    nw = w.shape[1]
    return pl.pallas_call(
        _memkv_kernel,
        grid=(n // ROW_TILE,),
        in_specs=[pl.BlockSpec((ROW_TILE, d), lambda i: (i, 0)),
                  pl.BlockSpec((1, d), lambda i: (0, 0)),
                  pl.BlockSpec((d, nw), lambda i: (0, 0))],
        out_specs=pl.BlockSpec((ROW_TILE, nw), lambda i: (i, 0)),
        out_shape=jax.ShapeDtypeStruct((n, nw), BF16),
        compiler_params=_params("arbitrary"),
        name="mem_kv",
    )(mem2, g, w)


def _memattn_kernel(q_ref, km_ref, vm_ref, o_ref, qp0, qp1, vext):
    seq = q_ref.shape[1]
    tq = 2 * ATT_TILE
    lane = lax.broadcasted_iota(jnp.int32, (1, LANES), 1)
    head0 = lane < HEAD_DIM
    m0 = head0.astype(F32)
    q = q_ref[0].astype(F32) * (HEAD_DIM ** -0.5)
    qp0[...] = (q * m0).astype(BF16)
    qp1[...] = (q * (1.0 - m0)).astype(BF16)
    vext[:, 0:LANES] = vm_ref[0]
    vext[:, LANES:2 * LANES] = jnp.ones((vext.shape[0], LANES), BF16)
    km = km_ref[0]

    def tile(m, carry):
        rows = pl.ds(pl.multiple_of(m * tq, tq), tq)
        qc = jnp.concatenate([qp0[rows, :], qp1[rows, :]], axis=0)
        s = _nt_dot(qc, km)
        pe = jnp.exp(s - jnp.max(s, axis=-1, keepdims=True))
        pv = jnp.dot(pe.astype(BF16), vext[...], preferred_element_type=F32)
        o = pv[:, 0:LANES] / pv[:, LANES:2 * LANES]
        o_ref[0, rows, :] = jnp.where(head0, o[0:tq], o[tq:2 * tq]).astype(BF16)
        return carry

    lax.fori_loop(0, seq // tq, tile, 0, unroll=True)


def _memattn(hq3, kvm3, layer, qm_block0, mem_width):
    b, s, _ = hq3.shape
    m = kvm3.shape[1]
    npair = mem_width // LANES
    kv_blocks = 2 * npair
    return pl.pallas_call(
        _memattn_kernel,
        grid=(b, npair),
        in_specs=[pl.BlockSpec((1, s, LANES), lambda i, j: (i, 0, qm_block0 + j)),
                  pl.BlockSpec((1, m, LANES), lambda i, j: (i, 0, layer * kv_blocks + j)),
                  pl.BlockSpec((1, m, LANES), lambda i, j: (i, 0, layer * kv_blocks + npair + j))],
        out_specs=pl.BlockSpec((1, s, LANES), lambda i, j: (i, 0, j)),
        out_shape=jax.ShapeDtypeStruct((b, s, mem_width), BF16),
        scratch_shapes=[pltpu.VMEM((s, LANES), BF16), pltpu.VMEM((s, LANES), BF16),
                        pltpu.VMEM((m, 2 * LANES), BF16)],
        compiler_params=_params("arbitrary", "arbitrary"),
        name="mem_attn",
    )(hq3, kvm3, kvm3)


OUT_SUB_ROWS = 512
TOKEN_TILE_ROWS = 8


def _outproj_kernel(x_ref, c_ref, d_ref, m_ref, wf_ref, g_ref, wr_ref,
                    x1_ref, xn_ref, aff_ref, w_ref, *, n_exp):
    @pl.when(pl.program_id(0) == 0)
    def _():
        w_ref[...] = wf_ref[0].astype(BF16)

    valid = lax.broadcasted_iota(jnp.int32, (1, LANES), 1) < n_exp
    for r0 in range(0, x_ref.shape[0], OUT_SUB_ROWS):
        rs = slice(r0, r0 + OUT_SUB_ROWS)
        cat = jnp.concatenate([c_ref[rs, :], d_ref[rs, :], m_ref[rs, :]], axis=1)
        x1 = x_ref[rs, :] + jnp.dot(cat, w_ref[...], preferred_element_type=F32)
        x1_ref[rs, :] = x1
        xn = _rms(x1, g_ref[...])
        for j in range(xn.shape[1] // LANES):
            xn_ref[pl.ds(r0 * TOKEN_TILE_ROWS + j, OUT_SUB_ROWS, stride=TOKEN_TILE_ROWS), :] = (
                xn[:, j * LANES:(j + 1) * LANES])
        xh = xn.astype(BF16)
        xl = (xn - xh.astype(F32)).astype(BF16)
        both = jnp.dot(xh, wr_ref[...], preferred_element_type=F32)
        logits = (both[:, 0:LANES] + both[:, LANES:2 * LANES]
                  + jnp.dot(xl, wr_ref[:, 0:LANES], preferred_element_type=F32))
        logits = jnp.where(valid, logits, NEG_INF)
        ex = jnp.exp(logits - jnp.max(logits, axis=1, keepdims=True))
        aff_ref[rs, :] = ex / jnp.sum(ex, axis=1, keepdims=True)


def _outproj(x2, conv_o, dil_o, mem_o, w_all, layer, g, wr_pieces, n_exp):
    n, d = x2.shape
    assert d == TOKEN_TILE_ROWS * LANES
    row = lambda i: (i, 0)
    fixed = lambda i: (0, 0)
    return pl.pallas_call(
        functools.partial(_outproj_kernel, n_exp=n_exp),
        grid=(n // PROJ_ROW_TILE,),
        in_specs=[pl.BlockSpec((PROJ_ROW_TILE, d), row),
                  pl.BlockSpec((PROJ_ROW_TILE, conv_o.shape[1]), row),
                  pl.BlockSpec((PROJ_ROW_TILE, dil_o.shape[1]), row),
                  pl.BlockSpec((PROJ_ROW_TILE, mem_o.shape[1]), row),
                  pl.BlockSpec((1,) + w_all.shape[1:], lambda i: (layer, 0, 0)),
                  pl.BlockSpec((1, d), fixed),
                  pl.BlockSpec((d, 2 * LANES), fixed)],
        out_specs=[pl.BlockSpec((PROJ_ROW_TILE, d), row),
                   pl.BlockSpec((PROJ_ROW_TILE * TOKEN_TILE_ROWS, LANES), row),
                   pl.BlockSpec((PROJ_ROW_TILE, LANES), row)],
        out_shape=[jax.ShapeDtypeStruct((n, d), F32),
                   jax.ShapeDtypeStruct((n * TOKEN_TILE_ROWS, LANES), F32),
                   jax.ShapeDtypeStruct((n, LANES), F32)],
        scratch_shapes=[pltpu.VMEM(w_all.shape[1:], BF16)],
        compiler_params=_params("arbitrary"),
        name="outproj_router",
    )(x2, conv_o, dil_o, mem_o, w_all, g, wr_pieces)


TOK_DIGIT_BITS = 6
PICK_GATE_LANE0 = 2


def _fill_strict_upper(tri):
    s = tri.shape[0]
    chunk = 2 * LANES
    for c in range(0, s, chunk):
        src = lax.broadcasted_iota(jnp.int32, (chunk, s), 0) + c
        dst = lax.broadcasted_iota(jnp.int32, (chunk, s), 1)
        tri[c:c + chunk, :] = jnp.where(src < dst, 1.0, 0.0).astype(BF16)


def _topk_kernel(aff_ref, afft_ref, slot_ref, pick_ref, idx_ref, rhs, tri, *, cap, n_batch, n_exp):
    aff = aff_ref[...]
    rows, seq = aff.shape
    _fill_strict_upper(tri)
    bits = jnp.zeros((rows, 1), jnp.int32)
    for bit in range(30, -1, -1):
        cand = bits | (1 << bit)
        cnt = jnp.sum(jnp.where(aff >= pltpu.bitcast(cand, F32), 1.0, 0.0), axis=1, keepdims=True)
        bits = jnp.where(cnt >= cap, cand, bits)
    thr = pltpu.bitcast(bits, F32)
    gt = aff > thr
    eq = aff == thr
    need = cap - jnp.sum(jnp.where(gt, 1.0, 0.0), axis=1, keepdims=True)
    rank_eq = jnp.dot(jnp.where(eq, 1.0, 0.0).astype(BF16), tri[...], preferred_element_type=F32)
    sel = jnp.where(gt, 1.0, jnp.where(eq & (rank_eq < need), 1.0, 0.0))
    rank = jnp.dot(sel.astype(BF16), tri[...], preferred_element_type=F32)
    slot_ref[...] = jnp.where(sel > 0.5, rank, -1.0).astype(jnp.int32)

    half = seq // 2
    lane = lax.broadcasted_iota(jnp.int32, (1, LANES), 1)
    tok = lax.broadcasted_iota(jnp.int32, (seq, LANES), 0)
    tok_part = jnp.where(lane == 0, tok >> TOK_DIGIT_BITS,
                         jnp.where(lane == 1, tok & ((1 << TOK_DIGIT_BITS) - 1),
                                   jnp.where(lane == LANES - 1, 1, 0))).astype(F32)
    src = lax.broadcasted_iota(jnp.int32, (LANES, LANES), 0)
    dst = lax.broadcasted_iota(jnp.int32, (LANES, LANES), 1)

    def place(piece):
        return jnp.where((src < n_exp) & (dst == src + PICK_GATE_LANE0 + piece * n_exp),
                         1.0, 0.0).astype(BF16)

    for bi in range(n_batch):
        a = afft_ref[bi * seq:(bi + 1) * seq, :]
        a_hi = a.astype(BF16)
        a_lo = (a - a_hi.astype(F32)).astype(BF16)
        full = (tok_part
                + jnp.dot(a_hi, place(0), preferred_element_type=F32)
                + jnp.dot(a_lo, place(1), preferred_element_type=F32)).astype(BF16)
        rhs[bi, :, 0:LANES] = full[0:half]
        rhs[bi, :, LANES:2 * LANES] = full[half:seq]

    sidx = lax.broadcasted_iota(jnp.int32, (cap, half), 0)

    def invert(r, carry):
        bi = lax.rem(r, n_batch)
        srow = slot_ref[pl.ds(r, 1), :]
        onehot = jnp.where(srow[:, 0:half] == sidx, 1.0,
                           jnp.where(srow[:, half:seq] == sidx, 2.0, 0.0)).astype(BF16)
        both = jnp.dot(onehot, rhs[bi], preferred_element_type=F32)
        lower = both[:, LANES - 1:LANES] < 1.5
        pick = jnp.where(lower, both[:, 0:LANES], both[:, LANES:2 * LANES] * 0.5)
        pick_ref[pl.ds(pl.multiple_of(r * cap, cap), cap), :] = pick
        digits = jnp.transpose(pick)
        token = (digits[0:1] * (1 << TOK_DIGIT_BITS) + digits[1:2]).astype(jnp.int32)
        idx_ref[pl.ds(r, 1), :] = (token + bi * seq) * TOKEN_TILE_ROWS
        return carry

    lax.fori_loop(0, rows, invert, 0, unroll=8)


def _topk(aff_rows, aff_tok, cap, n_batch, n_exp):
    rows, seq = aff_rows.shape
    return pl.pallas_call(
        functools.partial(_topk_kernel, cap=cap, n_batch=n_batch, n_exp=n_exp),
        out_shape=[jax.ShapeDtypeStruct(aff_rows.shape, jnp.int32),
                   jax.ShapeDtypeStruct((rows * cap, LANES), F32),
                   jax.ShapeDtypeStruct((rows, cap), jnp.int32)],
        scratch_shapes=[pltpu.VMEM((n_batch, seq // 2, 2 * LANES), BF16),
                        pltpu.VMEM((seq, seq), BF16)],
        compiler_params=pltpu.CompilerParams(vmem_limit_bytes=V7X_VMEM_LIMIT_BYTES),
        name="expert_topk",
    )(aff_rows, aff_tok)


FF_CHUNK = 512
EXPERT_ROWS = 1024


def _expert_kernel(idx_ref, xn_hbm, wg_ref, wu_ref, wd_ref, pick_ref, y_ref,
                   xbuf, sem, acc, wgb, wub, wdb, *, cap, n_f):
    e = pl.program_id(0)
    f = pl.program_id(1)
    n_exp = pl.num_programs(0)
    rows = acc.shape[0]
    slot = lax.rem(e, 2)
    nxt = 1 - slot
    e_nxt = jnp.minimum(e + 1, n_exp - 1)

    tile = TOKEN_TILE_ROWS

    def row_copy(src_tile_row, buf, dst_row):
        return pltpu.make_async_copy(
            xn_hbm.at[pl.ds(pl.multiple_of(src_tile_row, tile), tile), :],
            xbuf.at[pl.ds(pl.multiple_of((buf * rows + dst_row) * tile, tile), tile), :], sem.at[buf])

    def wait_rows(buf):
        pltpu.make_async_copy(xn_hbm.at[pl.ds(0, rows * tile), :],
                              xbuf.at[pl.ds(pl.multiple_of(buf * rows * tile, tile), rows * tile), :],
                              sem.at[buf]).wait()

    @pl.when((e == 0) & (f == 0))
    def _():
        def first(j, carry):
            row_copy(idx_ref[j], 0, j).start()
            return carry
        lax.fori_loop(0, rows, first, 0, unroll=8)

    @pl.when(f == 0)
    def _():
        wait_rows(slot)
        acc[...] = jnp.zeros(acc.shape, F32)

    wgb[...] = wg_ref[0, 0].astype(BF16)
    wub[...] = wu_ref[0, 0].astype(BF16)
    wdb[...] = wd_ref[0, 0].astype(BF16)

    per_step = rows // n_f
    per_chunk = per_step // (rows // EXPERT_ROWS)
    base = pl.multiple_of(f * per_step, per_step)
    idx_base = e_nxt * rows + base
    for ci, r0 in enumerate(range(0, rows, EXPERT_ROWS)):
        for k in range(ci * per_chunk, (ci + 1) * per_chunk):
            row_copy(idx_ref[idx_base + k], nxt, base + k).start(priority=k % 2)
        rsl = slice(r0, r0 + EXPERT_ROWS)
        xm = jnp.concatenate(
            [xbuf[pl.ds((slot * rows + r0) * tile + j, EXPERT_ROWS, stride=tile), :]
             for j in range(tile)], axis=1).astype(BF16)
        gate = jnp.dot(xm, wgb[...], preferred_element_type=F32)
        up = jnp.dot(xm, wub[...], preferred_element_type=F32)
        hid = (gate * (1.0 / (1.0 + jnp.exp(-gate))) * up).astype(BF16)
        acc[rsl, :] += jnp.dot(hid, wdb[...], preferred_element_type=F32)

    @pl.when(f == n_f - 1)
    def _():
        lane = lax.broadcasted_iota(jnp.int32, (1, LANES), 1) - (PICK_GATE_LANE0 + e)
        mine = (lane == 0) | (lane == n_exp)
        for r0 in range(0, rows, EXPERT_ROWS):
            rsl = slice(r0, r0 + EXPERT_ROWS)
            gate_col = jnp.sum(jnp.where(mine, pick_ref[rsl, :], 0.0), axis=1, keepdims=True)
            y_ref[0, rsl, :] = (acc[rsl, :] * gate_col).astype(BF16)

    @pl.when((e == n_exp - 1) & (f == n_f - 1))
    def _():
        wait_rows(nxt)


def _experts(idx2, xn_tiles, w_gate, w_up, w_down, layer, picked, cap):
    n_exp, rows = idx2.shape
    d = w_gate.shape[2]
    d_ff = w_gate.shape[3]
    n_f = d_ff // FF_CHUNK
    assert rows % (n_f * (rows // EXPERT_ROWS)) == 0
    return pl.pallas_call(
        functools.partial(_expert_kernel, cap=cap, n_f=n_f),
        grid_spec=pltpu.PrefetchScalarGridSpec(
            num_scalar_prefetch=1,
            grid=(n_exp, n_f),
            in_specs=[pl.BlockSpec(memory_space=pl.ANY),
                      pl.BlockSpec((1, 1, d, FF_CHUNK), lambda e, f, ix: (layer, e, 0, f)),
                      pl.BlockSpec((1, 1, d, FF_CHUNK), lambda e, f, ix: (layer, e, 0, f)),
                      pl.BlockSpec((1, 1, FF_CHUNK, d), lambda e, f, ix: (layer, e, f, 0)),
                      pl.BlockSpec((rows, LANES), lambda e, f, ix: (e, 0))],
            out_specs=pl.BlockSpec((1, rows, d), lambda e, f, ix: (e, 0, 0)),
            scratch_shapes=[pltpu.VMEM((2 * rows * TOKEN_TILE_ROWS, LANES), F32),
                            pltpu.SemaphoreType.DMA((2,)),
                            pltpu.VMEM((rows, d), F32),
                            pltpu.VMEM((d, FF_CHUNK), BF16),
                            pltpu.VMEM((d, FF_CHUNK), BF16),
                            pltpu.VMEM((FF_CHUNK, d), BF16)]),
        out_shape=jax.ShapeDtypeStruct((n_exp, rows, d), BF16),
        compiler_params=_params("arbitrary", "arbitrary"),
        name="expert_ffn",
    )(idx2.reshape(-1), xn_tiles, w_gate, w_up, w_down, picked)


COMBINE_GROUP = 4
COMBINE_WINDOW = 128
BF16_ROWS = 16


def _combine_kernel(win_ref, fits_ref, x_ref, slott_ref, y_ref, g_ref, o_ref, *, cap, final):
    bi = pl.program_id(0)
    ti = pl.program_id(1)
    tile_id = bi * pl.num_programs(1) + ti
    slott = slott_ref[0]
    t, n_exp = slott.shape
    d = x_ref.shape[2]

    def finish(acc):
        if final:
            acc = _rms(acc, g_ref[...])
        o_ref[0] = acc

    @pl.when(fits_ref[tile_id] != 0)
    def _():
        lane = lax.broadcasted_iota(jnp.int32, (t, COMBINE_WINDOW), 1)
        acc = x_ref[0]
        for e0 in range(0, n_exp, COMBINE_GROUP):
            parts, ys = [], []
            for e in range(e0, e0 + COMBINE_GROUP):
                start = pl.multiple_of(win_ref[tile_id * n_exp + e], BF16_ROWS)
                parts.append(jnp.where(slott[:, e:e + 1] - start == lane, 1.0, 0.0).astype(BF16))
                ys.append(y_ref[e, 0, pl.ds(start, COMBINE_WINDOW), :])
            acc = acc + jnp.dot(jnp.concatenate(parts, axis=1), jnp.concatenate(ys, axis=0),
                                preferred_element_type=F32)
        finish(acc)

    @pl.when(fits_ref[tile_id] == 0)
    def _():
        sidx = lax.broadcasted_iota(jnp.int32, (t, cap), 1)
        acc = x_ref[0]
        for e0 in range(0, n_exp, COMBINE_GROUP):
            parts = [jnp.where(slott[:, e:e + 1] == sidx, 1.0, 0.0).astype(BF16)
                     for e in range(e0, e0 + COMBINE_GROUP)]
            yg = y_ref[e0:e0 + COMBINE_GROUP, 0].reshape(COMBINE_GROUP * cap, d)
            acc = acc + jnp.dot(jnp.concatenate(parts, axis=1), yg, preferred_element_type=F32)
        finish(acc)


def _combine_windows(slot_rows, n_exp, b, s, cap):
    nt = s // ROW_TILE
    cnt = jnp.sum((slot_rows >= 0).reshape(n_exp, b, nt, ROW_TILE), axis=-1, dtype=jnp.int32)
    end = jnp.cumsum(cnt, axis=-1)
    start = jnp.minimum((end - cnt) // BF16_ROWS * BF16_ROWS, cap - COMBINE_WINDOW)
    fits = jnp.all(end - start <= COMBINE_WINDOW, axis=0)
    return jnp.transpose(start, (1, 2, 0)).reshape(-1), fits.reshape(-1).astype(jnp.int32)


def _combine(x3, slott, y4, g, win, fits, cap, final):
    b, s, d = x3.shape
    n_exp = y4.shape[0]
    assert cap >= COMBINE_WINDOW and cap % BF16_ROWS == 0
    return pl.pallas_call(
        functools.partial(_combine_kernel, cap=cap, final=final),
        grid_spec=pltpu.PrefetchScalarGridSpec(
            num_scalar_prefetch=2,
            grid=(b, s // ROW_TILE),
            in_specs=[pl.BlockSpec((1, ROW_TILE, d), lambda i, t, w, f: (i, t, 0)),
                      pl.BlockSpec((1, ROW_TILE, n_exp), lambda i, t, w, f: (i, t, 0)),
                      pl.BlockSpec((n_exp, 1, cap, d), lambda i, t, w, f: (0, i, 0, 0)),
                      pl.BlockSpec((1, d), lambda i, t, w, f: (0, 0))],
            out_specs=pl.BlockSpec((1, ROW_TILE, d), lambda i, t, w, f: (i, t, 0))),
        out_shape=jax.ShapeDtypeStruct((b, s, d), F32),
        compiler_params=_params("arbitrary", "arbitrary"),
        name="combine",
    )(win, fits, x3, slott, y4, g)


def kernel(x, mem, mem_norm, norm_mix, w_in, conv_w, w_mem_kv, w_out, norm_ffn,
           w_router, w_gate, w_up, w_down, norm_final):
    b, s, d = x.shape
    depth = w_in.shape[0]
    n_exp = w_router.shape[2]
    conv_ch = conv_w.shape[2]
    mem_width = w_mem_kv.shape[2] // 2
    dil_width = (w_in.shape[2] - 3 * conv_ch - mem_width) // 3
    n_dil_heads = dil_width // HEAD_DIM
    cap = EC_CAPACITY * s // n_exp
    assert n_exp == N_EXPERTS and mem_width == N_MEM_HEADS * HEAD_DIM
    assert all(w // (2 * dl) == HALF_WINDOW for w, dl in DILATED_PATTERNS)

    slopes = jnp.asarray([2.0 ** (-8.0 * (h + 1) / n_dil_heads) for h in range(n_dil_heads)], F32)
    kv_all = jnp.transpose(w_mem_kv, (1, 0, 2)).reshape(d, depth * 2 * mem_width).astype(BF16)
    kvm3 = _memkv(mem.reshape(-1, d), mem_norm.reshape(1, d), kv_all).reshape(b, mem.shape[1], -1)

    x2 = x.reshape(b * s, d)
    out = None
    for l in range(depth):
        conv_o, hq = _inproj(x2, norm_mix[l].reshape(1, d), w_in, l, conv_w[l], conv_ch, s)
        hq3 = hq.reshape(b, s, -1)
        dil_o = _dilated(hq3, slopes, dil_width)
        mem_o = _memattn(hq3, kvm3, l, 3 * dil_width // LANES, mem_width)
        wr = jnp.pad(w_router[l], ((0, 0), (0, LANES - n_exp)))
        wr_hi = wr.astype(BF16)
        wr_pieces = jnp.concatenate([wr_hi, (wr - wr_hi.astype(F32)).astype(BF16)], axis=1)
        x1, xn, aff_tok = _outproj(x2, conv_o, dil_o.reshape(b * s, -1),
                                   mem_o.reshape(b * s, -1), w_out, l,
                                   norm_ffn[l].reshape(1, d), wr_pieces, n_exp)
        aff_rows = jnp.transpose(aff_tok[:, :n_exp]).reshape(n_exp * b, s)
        slot_rows, picked, idx_rows = _topk(aff_rows, aff_tok, cap, b, n_exp)
        y = _experts(idx_rows.reshape(n_exp, b * cap), xn, w_gate, w_up, w_down, l, picked, cap)
        slott = jnp.transpose(slot_rows.reshape(n_exp, b, s), (1, 2, 0))
        win, fits = _combine_windows(slot_rows, n_exp, b, s, cap)
        final = l == depth - 1
        x3 = _combine(x1.reshape(b, s, d), slott, y.reshape(n_exp, b, cap, d),
                      norm_final.reshape(1, d), win, fits, cap, final)
        x2 = x3.reshape(b * s, d)
        out = x3
    return out
```
